```python
import math
import jax, jax.numpy as jnp
from jax import lax
import numpy as np


D_MODEL = 1024
BATCH = 4
SEQ = 8192
DEPTH = 1

HEAD_DIM = 64
N_HEADS_A = 8
N_HEADS_B = 8
DIL_PATTERNS = ((128, 1), (512, 4), (2048, 16))
BAND_BLOCK = 128
ROT_DIM = HEAD_DIM // 4
ROPE_THETA = 500000.0
NSA_GROUPS = 2
NSA_HPG = N_HEADS_B // NSA_GROUPS
CMP_LEN = 32
CMP_STRIDE = 16
CMP_HIDDEN = 4 * HEAD_DIM
SLC_BLOCK = 64
SLC_TOPK = 16
WIN = 512
Q_CHUNK = 128
D_FF = 2816
CONV_W = 3
ALPHA = (2 * DEPTH) ** 0.25
BETA = (8 * DEPTH) ** -0.25
LN_EPS = 1e-5
NEG = -1e30
MIX_WIDTH = (N_HEADS_A + N_HEADS_B) * HEAD_DIM
IN_SPLITS = (N_HEADS_A * HEAD_DIM,) * 3 + (N_HEADS_B * HEAD_DIM,) + (NSA_GROUPS * HEAD_DIM,) * 6 + (N_HEADS_B * 3,)
IN_WIDTH = sum(IN_SPLITS)

kernel_name = 'hybrid_dilated_nsa_convffn'


def layer_norm(x, g=None, b=None):
    xf = x.astype(jnp.float32)
    mu = jnp.mean(xf, -1, keepdims=True)
    var = jnp.mean(jnp.square(xf - mu), -1, keepdims=True)
    y = (xf - mu) * lax.rsqrt(var + LN_EPS)
    if g is not None:
        y = y * g.astype(jnp.float32) + b.astype(jnp.float32)
    return y.astype(x.dtype)


def partial_rope(t, pos):
    inv = 1.0 / (ROPE_THETA ** (jnp.arange(0, ROT_DIM, 2, dtype=jnp.float32) / ROT_DIM))
    ang = pos.astype(jnp.float32)[:, None] * inv[None, :]
    cos, sin = jnp.cos(ang), jnp.sin(ang)
    half = ROT_DIM // 2
    t1 = t[..., :half].astype(jnp.float32)
    t2 = t[..., half:ROT_DIM].astype(jnp.float32)
    rot = jnp.concatenate([t1 * cos - t2 * sin, t1 * sin + t2 * cos], -1).astype(t.dtype)
    return jnp.concatenate([rot, t[..., ROT_DIM:]], -1)


def masked_exp(s, mask, axis):
    s = jnp.where(mask, s, NEG)
    m = jnp.max(s, axis=axis, keepdims=True)
    p = jnp.where(mask, jnp.exp(s - m), 0.0)
    return p, m, jnp.sum(p, axis=axis, keepdims=True)


def band_attention(q, k, v, reach):
    L, hd = q.shape[-2], q.shape[-1]
    n = -(-L // BAND_BLOCK)
    lead = q.shape[:-2]
    padq = [(0, 0)] * len(lead) + [(0, n * BAND_BLOCK - L), (0, 0)]
    padk = [(0, 0)] * len(lead) + [(BAND_BLOCK, n * BAND_BLOCK - L), (0, 0)]
    qb = jnp.pad(q, padq).reshape(*lead, n, BAND_BLOCK, hd)

    def kv_blocks(t):
        t = jnp.pad(t, padk).reshape(*lead, n + 1, BAND_BLOCK, hd)
        return jnp.concatenate([t[..., :-1, :, :], t[..., 1:, :, :]], axis=-2)

    kb, vb = kv_blocks(k), kv_blocks(v)
    s = jnp.einsum('...nqd,...nkd->...nqk', qb, kb).astype(jnp.float32) * (hd ** -0.5)
    blk = jnp.arange(n)[:, None, None] * BAND_BLOCK
    qpos = blk + jnp.arange(BAND_BLOCK)[None, :, None]
    kpos = blk - BAND_BLOCK + jnp.arange(2 * BAND_BLOCK)[None, None, :]
    dist = qpos - kpos
    mask = (dist >= 0) & (dist <= reach) & (kpos >= 0)
    p, m, den = masked_exp(s, mask, -1)
    o = jnp.einsum('...nqk,...nkd->...nqd', p, vb) / den
    lse = (m + jnp.log(den))[..., 0]
    o = o.reshape(*lead, n * BAND_BLOCK, hd)[..., :L, :]
    lse = lse.reshape(*lead, n * BAND_BLOCK)[..., :L]
    return o, lse


def dilated_attention(q, k, v):
    B, H, S, hd = q.shape
    outs, lses = [], []
    for window, dil in DIL_PATTERNS:
        L = -(-S // dil)

        def by_residue(t):
            t = jnp.pad(t, ((0, 0), (0, 0), (0, L * dil - S), (0, 0)))
            return t.reshape(B, H, L, dil, hd).transpose(0, 1, 3, 2, 4)

        o, lse = band_attention(by_residue(q), by_residue(k), by_residue(v), window // dil)
        outs.append(o.transpose(0, 1, 3, 2, 4).reshape(B, H, L * dil, hd)[:, :, :S])
        lses.append(lse.transpose(0, 1, 3, 2).reshape(B, H, L * dil)[:, :, :S])
    w = jax.nn.softmax(jnp.stack(lses, 0), axis=0)[..., None]
    return jnp.sum(w * jnp.stack(outs, 0), axis=0).astype(q.dtype)


def compress(t, pe, w1, w2):
    B, G, S, hd = t.shape
    n = S // CMP_STRIDE
    r = CMP_LEN // CMP_STRIDE
    chunks = t.reshape(B, G, n, CMP_STRIDE, hd)
    blocks = jnp.concatenate([chunks[:, :, j:n - r + 1 + j] for j in range(r)], axis=3)
    blocks = (blocks + pe).reshape(B, G, n - r + 1, CMP_LEN * hd)
    return jax.nn.gelu(blocks @ w1) @ w2


def nsa_attention(q, kc, vc, ks, vs, kw, vw, gate_logits, pe, w_ck1, w_ck2, w_cv1, w_cv2):
    B, Hb, S, hd = q.shape
    G, HG = NSA_GROUPS, NSA_HPG
    scale = hd ** -0.5
    qg = q.reshape(B, G, HG, S, hd)
    gates = jax.nn.sigmoid(gate_logits.astype(jnp.float32)).reshape(B, S, G, HG, 3).transpose(0, 2, 3, 1, 4)
    kcc = compress(kc, pe, w_ck1, w_ck2)
    vcc = compress(vc, pe, w_cv1, w_cv2)
    NC = kcc.shape[2]
    NS = S // SLC_BLOCK
    topk = min(SLC_TOPK, NS)
    cmp_start = jnp.arange(NC) * CMP_STRIDE
    cmp_end = cmp_start + CMP_LEN - 1
    sel_start = jnp.arange(NS) * SLC_BLOCK
    overlap = jnp.clip(jnp.minimum(cmp_start[:, None] + CMP_LEN, sel_start[None, :] + SLC_BLOCK)
                       - jnp.maximum(cmp_start[:, None], sel_start[None, :]), 0, None).astype(jnp.float32) / CMP_LEN
    ksb = ks.reshape(B, G, NS, SLC_BLOCK, hd)
    vsb = vs.reshape(B, G, NS, SLC_BLOCK, hd)
    kw_pad = jnp.pad(kw, ((0, 0), (0, 0), (WIN, 0), (0, 0)))
    vw_pad = jnp.pad(vw, ((0, 0), (0, 0), (WIN, 0), (0, 0)))
    bi = jnp.arange(B)[:, None, None, None]
    gi = jnp.arange(G)[None, :, None, None]
    blk_ids = jnp.arange(NS)
    win_off = jnp.arange(Q_CHUNK + WIN) - WIN

    def chunk(c0):
        t = c0 + jnp.arange(Q_CHUNK)
        qc = lax.dynamic_slice_in_dim(qg, c0, Q_CHUNK, axis=3)
        gc = lax.dynamic_slice_in_dim(gates, c0, Q_CHUNK, axis=3)
        s = jnp.einsum('bghqd,bgnd->bghqn', qc, kcc).astype(jnp.float32) * scale
        p, _, den = masked_exp(s, cmp_end[None, :] <= t[:, None], -1)
        p = p / jnp.where(den > 0, den, 1.0)
        o_cmp = jnp.einsum('bghqn,bgnd->bghqd', p, vcc)
        imp = jnp.einsum('bghqn,ns->bgqs', p, overlap)
        cur = (t // SLC_BLOCK)[:, None]
        forced = (blk_ids[None] == cur) | (blk_ids[None] == cur - 1) | (blk_ids[None] == 0)
        valid = blk_ids[None] * SLC_BLOCK <= t[:, None]
        score = jnp.where(forced, 1e9, jnp.where(valid, imp, -1e9))
        _, idx = lax.top_k(score, topk)
        kg = ksb[bi, gi, idx]
        vg = vsb[bi, gi, idx]
        kpos = idx[..., None] * SLC_BLOCK + jnp.arange(SLC_BLOCK)
        smask = (kpos <= t[None, None, :, None, None])[:, :, None]
        s = jnp.einsum('bghqd,bgqkld->bghqkl', qc, kg).astype(jnp.float32) * scale
        p, _, den = masked_exp(s, smask, (-2, -1))
        o_slc = jnp.einsum('bghqkl,bgqkld->bghqd', p, vg) / den[..., 0]
        kwc = lax.dynamic_slice_in_dim(kw_pad, c0, Q_CHUNK + WIN, axis=2)
        vwc = lax.dynamic_slice_in_dim(vw_pad, c0, Q_CHUNK + WIN, axis=2)
        kpos_w = c0 + win_off
        dist = t[:, None] - kpos_w[None, :]
        wmask = (dist >= 0) & (dist < WIN) & (kpos_w[None, :] >= 0)
        s = jnp.einsum('bghqd,bgkd->bghqk', qc, kwc).astype(jnp.float32) * scale
        p, _, den = masked_exp(s, wmask, -1)
        o_win = jnp.einsum('bghqk,bgkd->bghqd', p, vwc) / den
        o = gc[..., 0:1] * o_cmp + gc[..., 1:2] * o_slc + gc[..., 2:3] * o_win
        return o.astype(q.dtype)

    outs = lax.map(chunk, jnp.arange(S // Q_CHUNK) * Q_CHUNK)
    return outs.transpose(1, 0, 4, 2, 3, 5).reshape(B, S, Hb * hd)


def causal_dwconv(a, w, b):
    F = a.shape[-1]
    y = lax.conv_general_dilated(a, w[:, None, :], window_strides=(1,), padding=[(CONV_W - 1, 0)],
                                 dimension_numbers=('NWC', 'WIO', 'NWC'), feature_group_count=F)
    return y + b


def setup_inputs(seed: int = 0) -> dict:
    key = jax.random.key(seed)
    ks = jax.random.split(key, 20)
    L, D, hd, F = DEPTH, D_MODEL, HEAD_DIM, D_FF

    def n(k, shape, s):
        return jax.random.normal(k, shape, jnp.float32) * s

    return {
        'x': n(ks[0], (BATCH, SEQ, D), 1.0),
        'c': n(ks[1], (BATCH, D), 1.0),
        'w_ada': n(ks[2], (L, D, 6 * D), 0.5 * D ** -0.5),
        'b_ada': n(ks[3], (L, 6 * D), 0.01),
        'w_in': n(ks[4], (L, D, IN_WIDTH), D ** -0.5),
        'pe_cmp': n(ks[5], (L, CMP_LEN, hd), 0.02),
        'w_ck1': n(ks[6], (L, CMP_LEN * hd, CMP_HIDDEN), (CMP_LEN * hd) ** -0.5),
        'w_ck2': n(ks[7], (L, CMP_HIDDEN, hd), CMP_HIDDEN ** -0.5),
        'w_cv1': n(ks[8], (L, CMP_LEN * hd, CMP_HIDDEN), (CMP_LEN * hd) ** -0.5),
        'w_cv2': n(ks[9], (L, CMP_HIDDEN, hd), CMP_HIDDEN ** -0.5),
        'w_o': n(ks[10], (L, MIX_WIDTH, D), BETA * MIX_WIDTH ** -0.5),
        'ln1_g': 1.0 + n(ks[11], (L, D), 0.01),
        'ln1_b': n(ks[12], (L, D), 0.01),
        'w_up': n(ks[13], (L, D, 2 * F), D ** -0.5),
        'conv_w': n(ks[14], (L, CONV_W, F), CONV_W ** -0.5),
        'conv_b': n(ks[15], (L, F), 0.01),
        'w_down': n(ks[16], (L, F, D), BETA * F ** -0.5),
        'ln2_g': 1.0 + n(ks[17], (L, D), 0.01),
        'ln2_b': n(ks[18], (L, D), 0.01),
    }


def reference(x, c, w_ada, b_ada, w_in, pe_cmp, w_ck1, w_ck2, w_cv1, w_cv2, w_o, ln1_g, ln1_b,
              w_up, conv_w, conv_b, w_down, ln2_g, ln2_b):
    B, S, D = x.shape
    hd = HEAD_DIM
    pos = jnp.arange(S)
    split_at = np.cumsum(IN_SPLITS)[:-1].tolist()

    def heads(t, nh):
        return t.reshape(B, S, nh, hd).transpose(0, 2, 1, 3)

    for l in range(DEPTH):
        mod = jax.nn.silu(c) @ w_ada[l] + b_ada[l]
        sh1, sc1, g1, sh2, sc2, g2 = [m[:, None, :] for m in jnp.split(mod, 6, axis=-1)]
        u = layer_norm(x) * (1 + sc1) + sh1
        qa, ka, va, qb, kc, vc, ksl, vsl, kw, vw, gl = jnp.split(u @ w_in[l], split_at, axis=-1)
        o_a = dilated_attention(partial_rope(heads(qa, N_HEADS_A), pos),
                                partial_rope(heads(ka, N_HEADS_A), pos),
                                heads(va, N_HEADS_A))
        o_a = o_a.transpose(0, 2, 1, 3).reshape(B, S, N_HEADS_A * hd)
        o_b = nsa_attention(partial_rope(heads(qb, N_HEADS_B), pos),
                            partial_rope(heads(kc, NSA_GROUPS), pos), heads(vc, NSA_GROUPS),
                            partial_rope(heads(ksl, NSA_GROUPS), pos), heads(vsl, NSA_GROUPS),
                            partial_rope(heads(kw, NSA_GROUPS), pos), heads(vw, NSA_GROUPS),
                            gl, pe_cmp[l], w_ck1[l], w_ck2[l], w_cv1[l], w_cv2[l])
        y = jnp.concatenate([o_a, o_b], axis=-1) @ w_o[l]
        x = layer_norm(ALPHA * x + g1 * y, ln1_g[l], ln1_b[l])
        u = layer_norm(x) * (1 + sc2) + sh2
        a_gate, a_val = jnp.split(u @ w_up[l], 2, axis=-1)
        h = jax.nn.gelu(causal_dwconv(a_gate, conv_w[l], conv_b[l])) * a_val
        x = layer_norm(ALPHA * x + g2 * (h @ w_down[l]), ln2_g[l], ln2_b[l])
    return x
```

```python
import functools
import math

import numpy as np
import jax
import jax.numpy as jnp
from jax import lax
from jax.experimental import pallas as pl
from jax.experimental.pallas import tpu as pltpu

HEAD_DIM = 64
N_HEADS_A = 8
N_HEADS_B = 8
DIL_PATTERNS = ((128, 1), (512, 4), (2048, 16))
BAND_BLOCK = 128
ROT_DIM = HEAD_DIM // 4
ROPE_THETA = 500000.0
NSA_GROUPS = 2
NSA_HPG = N_HEADS_B // NSA_GROUPS
CMP_LEN = 32
CMP_STRIDE = 16
CMP_HIDDEN = 4 * HEAD_DIM
SLC_BLOCK = 64
SLC_TOPK = 16
WIN = 512
D_FF = 2816
CONV_W = 3
DEPTH = 1
ALPHA = (2 * DEPTH) ** 0.25
LN_EPS = 1e-5
NEG = -1e30

LANES = 128
WA = N_HEADS_A * HEAD_DIM
WB = N_HEADS_B * HEAD_DIM
WG = NSA_GROUPS * HEAD_DIM
OFF_QA, OFF_KA, OFF_VA, OFF_QB = 0, WA, 2 * WA, 3 * WA
OFF_KC = OFF_QB + WB
OFF_VC, OFF_KS, OFF_VS, OFF_KW, OFF_VW = (OFF_KC + WG * i for i in range(1, 6))
OFF_GL = OFF_KC + 6 * WG
IN_WIDTH = OFF_GL + N_HEADS_B * 3
IN_WIDTH_PAD = OFF_GL + LANES

VMEM_LIMIT = 56 * 1024 * 1024

_NT = (((1,), (1,)), ((), ()))


def _cparams(sem):
    return pltpu.CompilerParams(dimension_semantics=sem, vmem_limit_bytes=VMEM_LIMIT)


def _gelu_tanh(x):
    return 0.5 * x * (1.0 + jnp.tanh(math.sqrt(2.0 / math.pi) * (x + 0.044715 * (x * x * x))))


def _ln(x):
    mu = jnp.mean(x, -1, keepdims=True)
    xc = x - mu
    var = jnp.mean(xc * xc, -1, keepdims=True)
    return xc * lax.rsqrt(var + LN_EPS)


def _ada_kernel(c_ref, w_ref, b_ref, o_ref):
    c = c_ref[...]
    a = c * jax.nn.sigmoid(c)
    o_ref[...] = jnp.dot(a, w_ref[...], preferred_element_type=jnp.float32) + b_ref[...]


def _ada(c, w, b):
    B, D = c.shape
    N = w.shape[1]
    tn = D
    return pl.pallas_call(
        _ada_kernel,
        grid=(N // tn,),
        in_specs=[pl.BlockSpec((B, D), lambda j: (0, 0)),
                  pl.BlockSpec((D, tn), lambda j: (0, j)),
                  pl.BlockSpec((1, tn), lambda j: (0, j))],
        out_specs=pl.BlockSpec((B, tn), lambda j: (0, j)),
        out_shape=jax.ShapeDtypeStruct((B, N), jnp.float32),
        compiler_params=_cparams(("arbitrary",)),
        name="adaln",
    )(c, w, b.reshape(1, N))


def _rope_tile(t, cos, s_lo, s_hi):
    half = ROT_DIM // 2
    return t * cos + pltpu.roll(t, half, 1) * s_hi + pltpu.roll(t, LANES - half, 1) * s_lo


def _inproj_kernel(x_ref, mod_ref, w_ref, cos_ref, slo_ref, shi_ref,
                   qa_ref, ka_ref, va_ref, qbx_ref, kc_ref, vc_ref, ksa_ref, vs_ref, kw_ref, vw_ref, gate_ref,
                   u_ref, *, tm):
    i = pl.program_id(1)
    x = x_ref[...]
    u = _ln(x) * (1.0 + mod_ref[1:2, :]) + mod_ref[0:1, :]
    u_ref[...] = u.astype(jnp.bfloat16)
    cos, slo, shi = cos_ref[...], slo_ref[...], shi_ref[...]
    lane = lax.broadcasted_iota(jnp.int32, (1, LANES), 1)

    def proj(off, width=LANES):
        return jnp.dot(u_ref[...], w_ref[:, off:off + width], preferred_element_type=jnp.float32)

    for j in range(WA // LANES):
        sl = slice(j * LANES, (j + 1) * LANES)
        qa_ref[:, sl] = _rope_tile(proj(OFF_QA + j * LANES), cos, slo, shi).astype(jnp.bfloat16)
        ka_ref[:, sl] = _rope_tile(proj(OFF_KA + j * LANES), cos, slo, shi).astype(jnp.bfloat16)
        va_ref[:, sl] = proj(OFF_VA + j * LANES).astype(jnp.bfloat16)
    for j in range(WB // LANES):
        t = _rope_tile(proj(OFF_QB + j * LANES), cos, slo, shi)
        g = (2 * j) // NSA_HPG
        tr = pltpu.roll(t, HEAD_DIM, 1)
        lo = lane < HEAD_DIM
        in_g = lo if g == 0 else jnp.logical_not(lo)
        even = jnp.where(in_g, t if g == 0 else tr, 0.0)
        odd = jnp.where(in_g, tr if g == 0 else t, 0.0)
        qbx_ref[:, (2 * j) * LANES:(2 * j + 1) * LANES] = even.astype(jnp.bfloat16)
        qbx_ref[:, (2 * j + 1) * LANES:(2 * j + 2) * LANES] = odd.astype(jnp.bfloat16)
    kc_ref[...] = _rope_tile(proj(OFF_KC), cos, slo, shi)
    vc_ref[...] = proj(OFF_VC)
    tpos = i * tm + lax.broadcasted_iota(jnp.int32, (tm, LANES), 0)
    blk = lax.broadcasted_iota(jnp.int32, (tm, LANES), 1)
    ksa_ref[:, 0:LANES] = jnp.where((tpos // SLC_BLOCK) == blk, 1.0, 0.0).astype(jnp.bfloat16)
    ksa_ref[:, LANES:2 * LANES] = _rope_tile(proj(OFF_KS), cos, slo, shi).astype(jnp.bfloat16)
    vs_ref[...] = proj(OFF_VS).astype(jnp.bfloat16)
    kw_ref[...] = _rope_tile(proj(OFF_KW), cos, slo, shi).astype(jnp.bfloat16)
    vw_ref[...] = proj(OFF_VW).astype(jnp.bfloat16)
    gate_ref[...] = jax.nn.sigmoid(proj(OFF_GL))


def _rope_tables(S):
    inv = 1.0 / (ROPE_THETA ** (jnp.arange(0, ROT_DIM, 2, dtype=jnp.float32) / ROT_DIM))
    ang = jnp.arange(S, dtype=jnp.float32)[:, None] * inv[None, :]
    cos, sin = jnp.cos(ang), jnp.sin(ang)
    half = ROT_DIM // 2
    d = np.arange(LANES) % HEAD_DIM
    idx = jnp.asarray(d % half)
    rot = jnp.asarray(d < ROT_DIM)
    lo = jnp.asarray(d < half)
    hi = jnp.asarray((d >= half) & (d < ROT_DIM))
    c_t = jnp.where(rot[None], cos[:, idx], 1.0)
    s_lo = jnp.where(lo[None], -sin[:, idx], 0.0)
    s_hi = jnp.where(hi[None], sin[:, idx], 0.0)
    return c_t, s_lo, s_hi


def _inproj(x, mod6, w_bf16, tables, tm):
    B, S, D = x.shape
    NW = w_bf16.shape[1]
    cos, slo, shi = tables
    bs = lambda w: pl.BlockSpec((None, tm, w), lambda b, i: (b, i, 0))
    tab = pl.BlockSpec((tm, LANES), lambda b, i: (i, 0))
    sd = lambda w, dt: jax.ShapeDtypeStruct((B, S, w), dt)
    bf, f32 = jnp.bfloat16, jnp.float32
    return pl.pallas_call(
        functools.partial(_inproj_kernel, tm=tm),
        grid=(B, S // tm),
        in_specs=[bs(D),
                  pl.BlockSpec((None, 6, D), lambda b, i: (b, 0, 0)),
                  pl.BlockSpec((D, NW), lambda b, i: (0, 0)),
                  tab, tab, tab],
        out_specs=[bs(WA), bs(WA), bs(WA), bs(2 * WB), bs(WG), bs(WG), bs(2 * LANES), bs(WG), bs(WG), bs(WG),
                   bs(LANES)],
        out_shape=[sd(WA, bf), sd(WA, bf), sd(WA, bf), sd(2 * WB, bf), sd(WG, f32), sd(WG, f32),
                   sd(2 * LANES, bf), sd(WG, bf), sd(WG, bf), sd(WG, bf), sd(LANES, f32)],
        scratch_shapes=[pltpu.VMEM((tm, D), jnp.bfloat16)],
        compiler_params=_cparams(("parallel", "parallel")),
        name="inproj",
    )(x, mod6, w_bf16, cos, slo, shi)


def _dil_kernel(q_ref, kc_ref, kp_ref, vc_ref, vp_ref, o_ref, lse_ref, *, nblk):
    i = pl.program_id(2)
    blk_rows = BAND_BLOCK
    qi = lax.broadcasted_iota(jnp.int32, (blk_rows, blk_rows), 0)
    kj = lax.broadcasted_iota(jnp.int32, (blk_rows, blk_rows), 1)
    bias_cur = jnp.where(kj <= qi, 0.0, NEG)
    bias_prev = jnp.where(kj >= qi, 0.0, NEG)
    bias_prev0 = jnp.where(i > 0, bias_prev, NEG)
    lane = lax.broadcasted_iota(jnp.int32, (1, LANES), 1)
    lo = lane < HEAD_DIM
    for blk in range(nblk):
        rows = slice(blk * blk_rows, (blk + 1) * blk_rows)
        prow = slice((blk - 1) * blk_rows, blk * blk_rows)
        bp = bias_prev0 if blk == 0 else bias_prev
        for hp in range(WA // LANES):
            cols = slice(hp * LANES, (hp + 1) * LANES)
            qt = q_ref[rows, cols]
            k_cur, v_cur = kc_ref[rows, cols], vc_ref[rows, cols]
            if blk == 0:
                k_prev, v_prev = kp_ref[:, cols], vp_ref[:, cols]
            else:
                k_prev, v_prev = kc_ref[prow, cols], vc_ref[prow, cols]
            outs, lses = [], []
            for h in range(2):
                qh = jnp.where(lo if h == 0 else jnp.logical_not(lo), qt, jnp.zeros_like(qt))
                s_c = lax.dot_general(qh, k_cur, _NT, preferred_element_type=jnp.float32) + bias_cur
                s_p = lax.dot_general(qh, k_prev, _NT, preferred_element_type=jnp.float32) + bp
                m = jnp.maximum(jnp.max(s_c, -1, keepdims=True), jnp.max(s_p, -1, keepdims=True))
                p_c = jnp.exp(s_c - m)
                p_p = jnp.exp(s_p - m)
                den = jnp.sum(p_c, -1, keepdims=True) + jnp.sum(p_p, -1, keepdims=True)
                o = (jnp.dot(p_c.astype(jnp.bfloat16), v_cur, preferred_element_type=jnp.float32)
                     + jnp.dot(p_p.astype(jnp.bfloat16), v_prev, preferred_element_type=jnp.float32))
                outs.append(o / den)
                lses.append(m + jnp.log(den))
            o_ref[rows, cols] = jnp.where(lo, outs[0], outs[1])
            lse_ref[rows, cols] = jnp.where(lo, lses[0], lses[1])


def _dilated(qa, ka, va, dil):
    B, S, W = qa.shape
    L = S // dil
    rows = min(L, 4 * BAND_BLOCK)
    nblk = rows // BAND_BLOCK
    view = lambda t: t.reshape(B, L, dil * W)
    cur = pl.BlockSpec((None, rows, W), lambda b, r, i: (b, i, r))
    prev = pl.BlockSpec((None, BAND_BLOCK, W), lambda b, r, i: (b, jnp.maximum(i * nblk - 1, 0), r))
    o, lse = pl.pallas_call(
        functools.partial(_dil_kernel, nblk=nblk),
        grid=(B, dil, L // rows),
        in_specs=[cur, cur, prev, cur, prev],
        out_specs=[cur, cur],
        out_shape=[jax.ShapeDtypeStruct((B, L, dil * W), jnp.float32)] * 2,
        compiler_params=_cparams(("parallel", "parallel", "arbitrary")),
        name=f"dilated{dil}",
    )(view(qa), view(ka), view(ka), view(va), view(va))
    return o.reshape(B, S, W), lse.reshape(B, S, W)


def _cmp_kernel(ck_ref, cv_ref, pe_ref, wk1_ref, wk2_ref, wv1_ref, wv2_ref, ok_ref, ov_ref):
    half = pe_ref.shape[1] // 2
    n = ck_ref.shape[0]

    def run(c_ref, w1_ref, w2_ref, o_ref):
        c = c_ref[...]
        a = jnp.dot((c + pe_ref[:, :half]).astype(jnp.bfloat16), w1_ref[:half, :],
                    preferred_element_type=jnp.float32)
        b = jnp.dot((c + pe_ref[:, half:]).astype(jnp.bfloat16), w1_ref[half:, :],
                    preferred_element_type=jnp.float32)
        hid = a + pltpu.roll(b, n - 1, 0)
        o_ref[...] = jnp.dot(_gelu_tanh(hid).astype(jnp.bfloat16), w2_ref[...],
                             preferred_element_type=jnp.float32).astype(o_ref.dtype)

    run(ck_ref, wk1_ref, wk2_ref, ok_ref)
    run(cv_ref, wv1_ref, wv2_ref, ov_ref)


def _compress(kc, vc, pe, w_ck1, w_ck2, w_cv1, w_cv2):
    B, S, _ = kc.shape
    n = S // CMP_STRIDE
    cw = CMP_STRIDE * HEAD_DIM

    def chunks(t):
        t = t.reshape(B, n, CMP_STRIDE, NSA_GROUPS, HEAD_DIM).transpose(0, 3, 1, 2, 4)
        return t.reshape(B, NSA_GROUPS, n, cw)

    dup = lambda w: jnp.concatenate([w, w], axis=-1).astype(jnp.bfloat16)
    cspec = pl.BlockSpec((None, None, n, cw), lambda b, g: (b, g, 0, 0))
    full = lambda a: pl.BlockSpec(a.shape, lambda b, g: (0,) * a.ndim)
    ospec = pl.BlockSpec((None, None, n, LANES), lambda b, g: (b, g, 0, 0))
    args = (chunks(kc), chunks(vc), pe.reshape(1, CMP_LEN * HEAD_DIM),
            w_ck1.astype(jnp.bfloat16), dup(w_ck2), w_cv1.astype(jnp.bfloat16), dup(w_cv2))
    return pl.pallas_call(
        _cmp_kernel,
        grid=(B, NSA_GROUPS),
        in_specs=[cspec, cspec] + [full(a) for a in args[2:]],
        out_specs=[ospec, ospec],
        out_shape=[jax.ShapeDtypeStruct((B, NSA_GROUPS, n, LANES), jnp.bfloat16)] * 2,
        compiler_params=_cparams(("parallel", "parallel")),
        name="compress",
    )(*args)


def _nsa_cmp_kernel(q_ref, kcc_ref, vcc_ref, ovt_ref, o_ref, sel_ref, *, tq, nsel, topk):
    qi = pl.program_id(2)
    t0 = qi * tq
    nc = kcc_ref.shape[0]
    tpos = t0 + lax.broadcasted_iota(jnp.int32, (tq, nc), 0)
    cend = lax.broadcasted_iota(jnp.int32, (tq, nc), 1) * CMP_STRIDE + (CMP_LEN - 1)
    cmask = cend <= tpos
    lane = lax.broadcasted_iota(jnp.int32, (1, LANES), 1)
    lo = lane < HEAD_DIM
    kcc, vcc = kcc_ref[...], vcc_ref[...]
    psum = jnp.zeros((tq, nc), jnp.float32)
    outs = []
    for h in range(NSA_HPG):
        qh = q_ref[:, h * LANES:(h + 1) * LANES]
        s = lax.dot_general(qh, kcc, _NT, preferred_element_type=jnp.float32)
        s = jnp.where(cmask, s, NEG)
        m = jnp.max(s, -1, keepdims=True)
        p = jnp.where(cmask, jnp.exp(s - m), 0.0)
        den = jnp.sum(p, -1, keepdims=True)
        p = p / jnp.where(den > 0, den, 1.0)
        outs.append(jnp.dot(p.astype(jnp.bfloat16), vcc, preferred_element_type=jnp.float32))
        psum = psum + p
    for j in range(NSA_HPG // 2):
        o_ref[:, j * LANES:(j + 1) * LANES] = jnp.where(lo, outs[2 * j], outs[2 * j + 1])
    imp = lax.dot_general(ovt_ref[...], psum.astype(jnp.bfloat16), _NT, preferred_element_type=jnp.float32)
    blk = lax.broadcasted_iota(jnp.int32, (LANES, tq), 0)
    tq_pos = t0 + lax.broadcasted_iota(jnp.int32, (LANES, tq), 1)
    cur = tq_pos // SLC_BLOCK
    forced = (blk == cur) | (blk == cur - 1) | (blk == 0)
    valid = blk * SLC_BLOCK <= tq_pos
    score = jnp.where(forced, 1e9, jnp.where(valid, imp, -1e9))
    score = jnp.where(blk < nsel, score, -jnp.inf)
    blk_f = blk.astype(jnp.float32)
    sel = jnp.zeros((LANES, tq), jnp.float32)
    for _ in range(topk):
        mx = jnp.max(score, 0, keepdims=True)
        first = jnp.min(jnp.where(score == mx, blk_f, float(2 * LANES)), 0, keepdims=True)
        hit = blk_f == first
        sel = jnp.where(hit, 1.0, sel)
        score = jnp.where(hit, -jnp.inf, score)
    sel_ref[...] = ((sel.T - 1.0) * (-NEG)).astype(jnp.bfloat16)


def _overlap_t(S):
    nc = S // CMP_STRIDE
    ns = S // SLC_BLOCK
    cs = np.arange(nc) * CMP_STRIDE
    ss = np.arange(LANES) * SLC_BLOCK
    ov = np.clip(np.minimum(cs[None, :] + CMP_LEN, ss[:, None] + SLC_BLOCK)
                 - np.maximum(cs[None, :], ss[:, None]), 0, None).astype(np.float32) / CMP_LEN
    ov[ns:, :] = 0.0
    ov[:, nc - CMP_LEN // CMP_STRIDE + 1:] = 0.0
    return jnp.asarray(ov, jnp.bfloat16)


def _nsa_cmp(qbx, kcc, vcc, tq):
    B, S, _ = qbx.shape
    nc = kcc.shape[2]
    ns = S // SLC_BLOCK
    gw = NSA_HPG * LANES
    ovt = _overlap_t(S)
    return pl.pallas_call(
        functools.partial(_nsa_cmp_kernel, tq=tq, nsel=ns, topk=min(SLC_TOPK, ns)),
        grid=(B, NSA_GROUPS, S // tq),
        in_specs=[pl.BlockSpec((None, tq, gw), lambda b, g, i: (b, i, g)),
                  pl.BlockSpec((None, None, nc, LANES), lambda b, g, i: (b, g, 0, 0)),
                  pl.BlockSpec((None, None, nc, LANES), lambda b, g, i: (b, g, 0, 0)),
                  pl.BlockSpec((LANES, nc), lambda b, g, i: (0, 0))],
        out_specs=[pl.BlockSpec((None, tq, NSA_HPG * HEAD_DIM), lambda b, g, i: (b, i, g)),
                   pl.BlockSpec((None, None, tq, LANES), lambda b, g, i: (b, g, i, 0))],
        out_shape=[jax.ShapeDtypeStruct((B, S, WB), jnp.float32),
                   jax.ShapeDtypeStruct((B, NSA_GROUPS, S, LANES), jnp.bfloat16)],
        compiler_params=_cparams(("parallel", "parallel", "parallel")),
        name="nsa_cmp",
    )(qbx, kcc, vcc, ovt)


def _nsa_main_kernel(q_ref, sel_ref, ksa_ref, vs_ref, kw_ref, vw_ref, ocmp_ref, gate_ref, o_ref,
                     qa_scr, m_scr, l_scr, acc_scr, *, tq, tk):
    g = pl.program_id(1)
    qi = pl.program_id(2)
    t0 = qi * tq
    rows = NSA_HPG * tq
    selb = sel_ref[...]
    for h in range(NSA_HPG):
        qa_scr[h * tq:(h + 1) * tq, 0:LANES] = selb
        qa_scr[h * tq:(h + 1) * tq, LANES:2 * LANES] = q_ref[:, h * LANES:(h + 1) * LANES]
    m_scr[...] = jnp.full((rows, LANES), -1e37, jnp.float32)
    l_scr[...] = jnp.zeros((rows, LANES), jnp.float32)
    acc_scr[...] = jnp.zeros((rows, LANES), jnp.float32)

    def update(start, width, causal):
        k = ksa_ref[pl.ds(start, width), :]
        v = vs_ref[pl.ds(start, width), :]
        s = lax.dot_general(qa_scr[...], k, _NT, preferred_element_type=jnp.float32)
        if causal:
            tl = lax.broadcasted_iota(jnp.int32, (tq, width), 0) + t0
            kp = lax.broadcasted_iota(jnp.int32, (tq, width), 1) + start
            cb = jnp.where(kp <= tl, 0.0, NEG)
            s = s + jnp.concatenate([cb] * NSA_HPG, axis=0)
        nch = width // LANES
        chunks = [s[:, c * LANES:(c + 1) * LANES] for c in range(nch)]
        mx = chunks[0]
        for c in chunks[1:]:
            mx = jnp.maximum(mx, c)
        m_prev = m_scr[...]
        m_new = jnp.maximum(m_prev, jnp.max(mx, -1, keepdims=True))
        alpha = jnp.exp(m_prev - m_new)
        ps = [jnp.exp(c - m_new) for c in chunks]
        tot = ps[0]
        for c in ps[1:]:
            tot = tot + c
        l_scr[...] = alpha * l_scr[...] + jnp.sum(tot, -1, keepdims=True)
        p = jnp.concatenate(ps, axis=1) if nch > 1 else ps[0]
        acc_scr[...] = alpha * acc_scr[...] + jnp.dot(p.astype(jnp.bfloat16), v, preferred_element_type=jnp.float32)
        m_scr[...] = m_new

    per = tk // tq
    nfull = qi // per

    def full_body(j, carry):
        update(pl.multiple_of(j * tk, tk), tk, False)
        return carry

    lax.fori_loop(0, nfull, full_body, 0)

    def tail_body(j, carry):
        update(pl.multiple_of(j * tq, tq), tq, True)
        return carry

    lax.fori_loop(nfull * per, qi + 1, tail_body, 0)
    o_slc = acc_scr[...] / l_scr[...]

    nw = WIN // tq + 1
    qw = jnp.concatenate([q_ref[:, h * LANES:(h + 1) * LANES] for h in range(NSA_HPG)], axis=0)
    tl = lax.broadcasted_iota(jnp.int32, (tq, tq), 0)
    kk = lax.broadcasted_iota(jnp.int32, (tq, tq), 1)
    s_list, v_list = [], []
    for j in range(nw):
        start = t0 - WIN + j * tq
        ok = start >= 0
        cs = pl.multiple_of(jnp.maximum(start, 0), tq)
        kt = kw_ref[pl.ds(cs, tq), :]
        v_list.append(vw_ref[pl.ds(cs, tq), :])
        s = lax.dot_general(qw, kt, _NT, preferred_element_type=jnp.float32)
        if j == 0:
            bias = jnp.where(kk > tl, 0.0, NEG)
        elif j == nw - 1:
            bias = jnp.where(kk <= tl, 0.0, NEG)
        else:
            bias = jnp.zeros((tq, tq), jnp.float32)
        bias = jnp.where(ok, bias, NEG)
        s_list.append(s + jnp.concatenate([bias] * NSA_HPG, axis=0))
    mx = s_list[0]
    for s in s_list[1:]:
        mx = jnp.maximum(mx, s)
    m = jnp.max(mx, -1, keepdims=True)
    ps = [jnp.exp(s - m) for s in s_list]
    tot = ps[0]
    for p in ps[1:]:
        tot = tot + p
    den = jnp.sum(tot, -1, keepdims=True)
    acc = jnp.dot(ps[0].astype(jnp.bfloat16), v_list[0], preferred_element_type=jnp.float32)
    for p, v in zip(ps[1:], v_list[1:]):
        acc = acc + jnp.dot(p.astype(jnp.bfloat16), v, preferred_element_type=jnp.float32)
    o_win = acc / den

    lane = lax.broadcasted_iota(jnp.int32, (1, LANES), 1)
    lo = lane < HEAD_DIM
    in_g = (lane >= g * HEAD_DIM) & (lane < (g + 1) * HEAD_DIM)
    gates = gate_ref[...]
    glane = lax.broadcasted_iota(jnp.int32, (tq, LANES), 1)

    def gate_col(idx):
        return jnp.sum(jnp.where(glane == idx, gates, 0.0), -1, keepdims=True)

    base = g * (NSA_HPG * 3)
    per_head = []
    for h in range(NSA_HPG):
        r = slice(h * tq, (h + 1) * tq)
        mix = gate_col(base + 3 * h + 1) * o_slc[r] + gate_col(base + 3 * h + 2) * o_win[r]
        mix = jnp.where(in_g, mix, 0.0)
        per_head.append((mix + pltpu.roll(mix, HEAD_DIM, 1), gate_col(base + 3 * h)))
    for j in range(NSA_HPG // 2):
        (me, ge), (mo, go) = per_head[2 * j], per_head[2 * j + 1]
        oc = ocmp_ref[:, j * LANES:(j + 1) * LANES]
        o_ref[:, j * LANES:(j + 1) * LANES] = (jnp.where(lo, me, mo) + jnp.where(lo, ge, go) * oc).astype(o_ref.dtype)


def _nsa_main(qbx, selb, ksa, vs, kw, vw, o_cmp, gates, tq, tk):
    B, S, _ = qbx.shape
    gw = NSA_HPG * LANES
    ow = NSA_HPG * HEAD_DIM
    rows = NSA_HPG * tq
    seq = lambda w: pl.BlockSpec((None, S, w), lambda b, g, i: (b, 0, 0))
    return pl.pallas_call(
        functools.partial(_nsa_main_kernel, tq=tq, tk=tk),
        grid=(B, NSA_GROUPS, S // tq),
        in_specs=[pl.BlockSpec((None, tq, gw), lambda b, g, i: (b, i, g)),
                  pl.BlockSpec((None, None, tq, LANES), lambda b, g, i: (b, g, i, 0)),
                  seq(2 * LANES), seq(WG), seq(WG), seq(WG),
                  pl.BlockSpec((None, tq, ow), lambda b, g, i: (b, i, g)),
                  pl.BlockSpec((None, tq, LANES), lambda b, g, i: (b, i, 0))],
        out_specs=pl.BlockSpec((None, tq, ow), lambda b, g, i: (b, i, g)),
        out_shape=jax.ShapeDtypeStruct((B, S, WB), jnp.bfloat16),
        scratch_shapes=[pltpu.VMEM((rows, 2 * LANES), jnp.bfloat16),
                        pltpu.VMEM((rows, LANES), jnp.float32),
                        pltpu.VMEM((rows, LANES), jnp.float32),
                        pltpu.VMEM((rows, LANES), jnp.float32)],
        compiler_params=_cparams(("parallel", "parallel", "arbitrary")),
        name="nsa_main",
    )(qbx, selb, ksa, vs, kw, vw, o_cmp, gates)


def _outproj_kernel(x_ref, o1_ref, o2_ref, o3_ref, l1_ref, l2_ref, l3_ref, ob_ref, mod_ref, wo_ref, g_ref, b_ref,
                    y_ref):
    l1, l2, l3 = l1_ref[...], l2_ref[...], l3_ref[...]
    m = jnp.maximum(jnp.maximum(l1, l2), l3)
    e1, e2, e3 = jnp.exp(l1 - m), jnp.exp(l2 - m), jnp.exp(l3 - m)
    o_a = (e1 * o1_ref[...] + e2 * o2_ref[...] + e3 * o3_ref[...]) / (e1 + e2 + e3)
    y = (jnp.dot(o_a.astype(jnp.bfloat16), wo_ref[:WA, :], preferred_element_type=jnp.float32)
         + jnp.dot(ob_ref[...], wo_ref[WA:, :], preferred_element_type=jnp.float32))
    z = ALPHA * x_ref[...] + mod_ref[2:3, :] * y
    y_ref[...] = _ln(z) * g_ref[...] + b_ref[...]


def _outproj(x, o_pat, lse_pat, o_b, mod6, wo_bf16, ln_g, ln_b, tm):
    B, S, D = x.shape
    bs = lambda w: pl.BlockSpec((None, tm, w), lambda b, i: (b, i, 0))
    vec = pl.BlockSpec((1, D), lambda b, i: (0, 0))
    return pl.pallas_call(
        _outproj_kernel,
        grid=(B, S // tm),
        in_specs=[bs(D)] + [bs(WA)] * 6 + [bs(WB),
                  pl.BlockSpec((None, 6, D), lambda b, i: (b, 0, 0)),
                  pl.BlockSpec((WA + WB, D), lambda b, i: (0, 0)), vec, vec],
        out_specs=bs(D),
        out_shape=jax.ShapeDtypeStruct((B, S, D), jnp.float32),
        compiler_params=_cparams(("parallel", "parallel")),
        name="outproj",
    )(x, *o_pat, *lse_pat, o_b, mod6, wo_bf16, ln_g.reshape(1, D), ln_b.reshape(1, D))


def _ffn_kernel(x_ref, mod_ref, wup_ref, cw_ref, cb_ref, wdn_ref, g_ref, b_ref, y_ref,
                u_scr, buf_scr, carry_scr, acc_scr, *, tm, fc):
    i = pl.program_id(1)
    F = wdn_ref.shape[0]
    pad = 8

    @pl.when(i == 0)
    def _():
        carry_scr[...] = jnp.zeros_like(carry_scr)

    x = x_ref[...]
    u_scr[...] = (_ln(x) * (1.0 + mod_ref[4:5, :]) + mod_ref[3:4, :]).astype(jnp.bfloat16)
    for c in range(F // fc):
        cols = slice(c * fc, (c + 1) * fc)
        a_gate = jnp.dot(u_scr[...], wup_ref[:, c * fc:(c + 1) * fc], preferred_element_type=jnp.float32)
        a_val = jnp.dot(u_scr[...], wup_ref[:, F + c * fc:F + (c + 1) * fc], preferred_element_type=jnp.float32)
        buf_scr[0:pad, :] = carry_scr[:, cols]
        buf_scr[pad:pad + tm, :] = a_gate
        carry_scr[:, cols] = a_gate[tm - pad:tm, :]
        conv = (cw_ref[0:1, cols] * buf_scr[pad - 2:pad - 2 + tm, :]
                + cw_ref[1:2, cols] * buf_scr[pad - 1:pad - 1 + tm, :]
                + cw_ref[2:3, cols] * a_gate + cb_ref[:, cols])
        h = (_gelu_tanh(conv) * a_val).astype(jnp.bfloat16)
        d = jnp.dot(h, wdn_ref[cols, :], preferred_element_type=jnp.float32)
        if c == 0:
            acc_scr[...] = d
        else:
            acc_scr[...] += d
    z = ALPHA * x + mod_ref[5:6, :] * acc_scr[...]
    y_ref[...] = _ln(z) * g_ref[...] + b_ref[...]


def _ffn(x, mod6, wup_bf16, conv_w, conv_b, wdn_bf16, ln_g, ln_b, tm, fc):
    B, S, D = x.shape
    F = wdn_bf16.shape[0]
    bs = pl.BlockSpec((None, tm, D), lambda b, i: (b, i, 0))
    full = lambda a: pl.BlockSpec(a.shape, lambda b, i: (0,) * a.ndim)
    args = (wup_bf16, conv_w, conv_b.reshape(1, F), wdn_bf16, ln_g.reshape(1, D), ln_b.reshape(1, D))
    return pl.pallas_call(
        functools.partial(_ffn_kernel, tm=tm, fc=fc),
        grid=(B, S // tm),
        in_specs=[bs, pl.BlockSpec((None, 6, D), lambda b, i: (b, 0, 0))] + [full(a) for a in args],
        out_specs=bs,
        out_shape=jax.ShapeDtypeStruct((B, S, D), jnp.float32),
        scratch_shapes=[pltpu.VMEM((tm, D), jnp.bfloat16),
                        pltpu.VMEM((tm + 8, fc), jnp.float32),
                        pltpu.VMEM((8, F), jnp.float32),
                        pltpu.VMEM((tm, D), jnp.float32)],
        compiler_params=_cparams(("arbitrary", "arbitrary")),
        name="ffn",
    )(x, mod6, *args)


def _prep_w_in(w):
    D = w.shape[0]
    w = jnp.pad(w, ((0, 0), (0, IN_WIDTH_PAD - w.shape[1])))
    scale = np.ones((IN_WIDTH_PAD,), np.float32)
    scale[OFF_QA:OFF_QA + WA] = HEAD_DIM ** -0.5
    scale[OFF_QB:OFF_QB + WB] = HEAD_DIM ** -0.5
    return (w * jnp.asarray(scale)[None, :]).astype(jnp.bfloat16)


def kernel(x, c, w_ada, b_ada, w_in, pe_cmp, w_ck1, w_ck2, w_cv1, w_cv2, w_o, ln1_g, ln1_b, w_up, conv_w, conv_b,
           w_down, ln2_g, ln2_b):
    B, S, D = x.shape
    tables = _rope_tables(S)
    tm = min(512, S)
    tq = 128
    tk = min(512, S)
    for l in range(DEPTH):
        mod6 = _ada(c, w_ada[l], b_ada[l]).reshape(B, 6, D)
        (qa, ka, va, qbx, kc, vc, ksa, vs, kw, vw, gates) = _inproj(x, mod6, _prep_w_in(w_in[l]), tables, tm)
        pats = [_dilated(qa, ka, va, dil) for _, dil in DIL_PATTERNS]
        kcc, vcc = _compress(kc, vc, pe_cmp[l], w_ck1[l], w_ck2[l], w_cv1[l], w_cv2[l])
        o_cmp, selb = _nsa_cmp(qbx, kcc, vcc, tq)
        o_b = _nsa_main(qbx, selb, ksa, vs, kw, vw, o_cmp, gates, tq, tk)
        x = _outproj(x, [p[0] for p in pats], [p[1] for p in pats], o_b, mod6, w_o[l].astype(jnp.bfloat16),
                     ln1_g[l], ln1_b[l], tm)
        x = _ffn(x, mod6, w_up[l].astype(jnp.bfloat16), conv_w[l], conv_b[l], w_down[l].astype(jnp.bfloat16),
                 ln2_g[l], ln2_b[l], tm, 256)
    return x
```

```python
import functools
import math

import numpy as np
import jax
import jax.numpy as jnp
from jax import lax
from jax.experimental import pallas as pl
from jax.experimental.pallas import tpu as pltpu

HEAD_DIM = 64
N_HEADS_A = 8
N_HEADS_B = 8
DIL_PATTERNS = ((128, 1), (512, 4), (2048, 16))
BAND_BLOCK = 128
ROT_DIM = HEAD_DIM // 4
ROPE_THETA = 500000.0
NSA_GROUPS = 2
NSA_HPG = N_HEADS_B // NSA_GROUPS
CMP_LEN = 32
CMP_STRIDE = 16
CMP_HIDDEN = 4 * HEAD_DIM
SLC_BLOCK = 64
SLC_TOPK = 16
WIN = 512
D_FF = 2816
CONV_W = 3
DEPTH = 1
ALPHA = (2 * DEPTH) ** 0.25
LN_EPS = 1e-5
NEG = -1e30

LANES = 128
MXU_N = 256
WA = N_HEADS_A * HEAD_DIM
WB = N_HEADS_B * HEAD_DIM
WG = NSA_GROUPS * HEAD_DIM
N_GATES = N_HEADS_B * 3
SRC_VS = 3 * WA + WB + 3 * WG
SRC_KW = SRC_VS + WG
SRC_VW = SRC_KW + WG
SRC_GL = SRC_VW + WG
OFF_QA, OFF_KA, OFF_VA, OFF_QB = 0, WA, 2 * WA, 3 * WA
OFF_KC = OFF_QB + WB
OFF_VC = OFF_KC + WG
OFF_KS = OFF_VC + WG
OFF_KW = OFF_KS + WG
NAT_WIDTH = OFF_KW + WG
TR_VS, TR_VW, TR_GL = 0, WG, 2 * WG
TR_WIDTH = 3 * WG

VMEM_LIMIT = 56 * 1024 * 1024

_NT = (((1,), (1,)), ((), ()))


def _cparams(sem):
    return pltpu.CompilerParams(dimension_semantics=sem, vmem_limit_bytes=VMEM_LIMIT)


def _gelu_tanh(x):
    return 0.5 * x * (1.0 + jnp.tanh(math.sqrt(2.0 / math.pi) * (x + 0.044715 * (x * x * x))))


def _ln(x):
    mu = jnp.mean(x, -1, keepdims=True)
    xc = x - mu
    var = jnp.mean(xc * xc, -1, keepdims=True)
    return xc * lax.rsqrt(var + LN_EPS)


def _ada_kernel(c_ref, w_ref, b_ref, o_ref):
    c = c_ref[...]
    a = c * jax.nn.sigmoid(c)
    o_ref[...] = jnp.dot(a, w_ref[...], preferred_element_type=jnp.float32) + b_ref[...]


def _ada(c, w, b):
    B, D = c.shape
    N = w.shape[1]
    tn = D
    return pl.pallas_call(
        _ada_kernel,
        grid=(N // tn,),
        in_specs=[pl.BlockSpec((B, D), lambda j: (0, 0)),
                  pl.BlockSpec((D, tn), lambda j: (0, j)),
                  pl.BlockSpec((1, tn), lambda j: (0, j))],
        out_specs=pl.BlockSpec((B, tn), lambda j: (0, j)),
        out_shape=jax.ShapeDtypeStruct((B, N), jnp.float32),
        compiler_params=_cparams(("arbitrary",)),
        name="adaln",
    )(c, w, b.reshape(1, N))


def _rope_tile(t, cos, s_lo, s_hi):
    half = ROT_DIM // 2
    return t * cos + pltpu.roll(t, half, 1) * s_hi + pltpu.roll(t, LANES - half, 1) * s_lo


def _inproj_kernel(x_ref, mod_ref, w_ref, wt_ref, cos_ref, slo_ref, shi_ref,
                   qa_ref, ka_ref, va_ref, qbx_ref, kc_ref, vc_ref, ksa_ref, kw_ref, vst_ref, vwt_ref, gate_ref,
                   u_ref, *, tm):
    i = pl.program_id(1)
    x = x_ref[...]
    u = _ln(x) * (1.0 + mod_ref[1:2, :]) + mod_ref[0:1, :]
    u_ref[...] = u.astype(jnp.bfloat16)
    cos, slo, shi = cos_ref[...], slo_ref[...], shi_ref[...]
    lane = lax.broadcasted_iota(jnp.int32, (1, LANES), 1)
    bf = jnp.bfloat16

    def proj(off):
        r = jnp.dot(u_ref[...], w_ref[:, off:off + MXU_N], preferred_element_type=jnp.float32)
        return r[:, :LANES], r[:, LANES:]

    def rope(t):
        return _rope_tile(t, cos, slo, shi)

    for j in range(WA // MXU_N):
        for k, t in enumerate(proj(OFF_QA + j * MXU_N)):
            qa_ref[:, (2 * j + k) * LANES:(2 * j + k + 1) * LANES] = rope(t).astype(bf)
        for k, t in enumerate(proj(OFF_KA + j * MXU_N)):
            ka_ref[:, (2 * j + k) * LANES:(2 * j + k + 1) * LANES] = rope(t).astype(bf)
        for k, t in enumerate(proj(OFF_VA + j * MXU_N)):
            va_ref[:, (2 * j + k) * LANES:(2 * j + k + 1) * LANES] = t.astype(bf)
    lo = lane < HEAD_DIM
    for j in range(WB // MXU_N):
        for k, t in enumerate(proj(OFF_QB + j * MXU_N)):
            pair = 2 * j + k
            t = rope(t)
            g = (2 * pair) // NSA_HPG
            tr = pltpu.roll(t, HEAD_DIM, 1)
            in_g = lo if g == 0 else jnp.logical_not(lo)
            even = jnp.where(in_g, t if g == 0 else tr, 0.0)
            odd = jnp.where(in_g, tr if g == 0 else t, 0.0)
            qbx_ref[:, (2 * pair) * LANES:(2 * pair + 1) * LANES] = even.astype(bf)
            qbx_ref[:, (2 * pair + 1) * LANES:(2 * pair + 2) * LANES] = odd.astype(bf)
    kc, vc = proj(OFF_KC)
    kc_ref[...] = rope(kc)
    vc_ref[...] = vc
    ks, kw = proj(OFF_KS)
    tpos = i * tm + lax.broadcasted_iota(jnp.int32, (tm, LANES), 0)
    blk = lax.broadcasted_iota(jnp.int32, (tm, LANES), 1)
    ksa_ref[:, 0:LANES] = jnp.where((tpos // SLC_BLOCK) == blk, 1.0, 0.0).astype(bf)
    ksa_ref[:, LANES:2 * LANES] = rope(ks).astype(bf)
    kw_ref[...] = rope(kw).astype(bf)
    tr_out = lax.dot_general(wt_ref[...], u_ref[...], _NT, preferred_element_type=jnp.float32)
    vst_ref[...] = tr_out[TR_VS:TR_VS + WG].astype(bf)
    vwt_ref[...] = tr_out[TR_VW:TR_VW + WG].astype(bf)
    gate_ref[...] = jax.nn.sigmoid(tr_out[TR_GL:TR_GL + LANES])


def _rope_tables(S):
    inv = 1.0 / (ROPE_THETA ** (jnp.arange(0, ROT_DIM, 2, dtype=jnp.float32) / ROT_DIM))
    ang = jnp.arange(S, dtype=jnp.float32)[:, None] * inv[None, :]
    cos, sin = jnp.cos(ang), jnp.sin(ang)
    half = ROT_DIM // 2
    d = np.arange(LANES) % HEAD_DIM
    idx = jnp.asarray(d % half)
    rot = jnp.asarray(d < ROT_DIM)
    lo = jnp.asarray(d < half)
    hi = jnp.asarray((d >= half) & (d < ROT_DIM))
    c_t = jnp.where(rot[None], cos[:, idx], 1.0)
    s_lo = jnp.where(lo[None], -sin[:, idx], 0.0)
    s_hi = jnp.where(hi[None], sin[:, idx], 0.0)
    return c_t, s_lo, s_hi


def _prep_w_in(w):
    q_scale = HEAD_DIM ** -0.5
    w_nat = jnp.concatenate([w[:, 0:WA] * q_scale, w[:, WA:3 * WA], w[:, 3 * WA:3 * WA + WB] * q_scale,
                             w[:, 3 * WA + WB:SRC_VS], w[:, SRC_KW:SRC_VW]], axis=1)
    w_tr = jnp.concatenate([w[:, SRC_VS:SRC_KW], w[:, SRC_VW:SRC_GL],
                            jnp.pad(w[:, SRC_GL:], ((0, 0), (0, LANES - N_GATES)))], axis=1)
    return w_nat.astype(jnp.bfloat16), w_tr.T.astype(jnp.bfloat16)


def _inproj(x, mod6, w_nat, w_tr, tables, tm):
    B, S, D = x.shape
    cos, slo, shi = tables
    bs = lambda w: pl.BlockSpec((None, tm, w), lambda b, i: (b, i, 0))
    bst = pl.BlockSpec((None, LANES, tm), lambda b, i: (b, 0, i))
    tab = pl.BlockSpec((tm, LANES), lambda b, i: (i, 0))
    sd = lambda w, dt: jax.ShapeDtypeStruct((B, S, w), dt)
    sdt = lambda dt: jax.ShapeDtypeStruct((B, LANES, S), dt)
    bf, f32 = jnp.bfloat16, jnp.float32
    return pl.pallas_call(
        functools.partial(_inproj_kernel, tm=tm),
        grid=(B, S // tm),
        in_specs=[bs(D),
                  pl.BlockSpec((None, 6, D), lambda b, i: (b, 0, 0)),
                  pl.BlockSpec((D, NAT_WIDTH), lambda b, i: (0, 0)),
                  pl.BlockSpec((TR_WIDTH, D), lambda b, i: (0, 0)),
                  tab, tab, tab],
        out_specs=[bs(WA), bs(WA), bs(WA), bs(2 * WB), bs(WG), bs(WG), bs(2 * LANES), bs(WG), bst, bst, bst],
        out_shape=[sd(WA, bf), sd(WA, bf), sd(WA, bf), sd(2 * WB, bf), sd(WG, f32), sd(WG, f32),
                   sd(2 * LANES, bf), sd(WG, bf), sdt(bf), sdt(bf), sdt(f32)],
        scratch_shapes=[pltpu.VMEM((tm, D), jnp.bfloat16)],
        compiler_params=_cparams(("parallel", "parallel")),
        name="inproj",
    )(x, mod6, w_nat, w_tr, cos, slo, shi)


def _dil_kernel(q_ref, kc_ref, kp_ref, vc_ref, vp_ref, o_ref, lse_ref, *, nblk):
    i = pl.program_id(2)
    blk_rows = BAND_BLOCK
    qi = lax.broadcasted_iota(jnp.int32, (blk_rows, blk_rows), 0)
    kj = lax.broadcasted_iota(jnp.int32, (blk_rows, blk_rows), 1)
    bias_cur = jnp.where(kj <= qi, 0.0, NEG)
    bias_prev = jnp.where(kj >= qi, 0.0, NEG)
    bias_prev0 = jnp.where(i > 0, bias_prev, NEG)
    lane = lax.broadcasted_iota(jnp.int32, (1, LANES), 1)
    lo = lane < HEAD_DIM
    for blk in range(nblk):
        rows = slice(blk * blk_rows, (blk + 1) * blk_rows)
        prow = slice((blk - 1) * blk_rows, blk * blk_rows)
        bp = bias_prev0 if blk == 0 else bias_prev
        for hp in range(WA // LANES):
            cols = slice(hp * LANES, (hp + 1) * LANES)
            qt = q_ref[rows, cols]
            k_cur, v_cur = kc_ref[rows, cols], vc_ref[rows, cols]
            if blk == 0:
                k_prev, v_prev = kp_ref[:, cols], vp_ref[:, cols]
            else:
                k_prev, v_prev = kc_ref[prow, cols], vc_ref[prow, cols]
            outs, lses = [], []
            for h in range(2):
                qh = jnp.where(lo if h == 0 else jnp.logical_not(lo), qt, jnp.zeros_like(qt))
                s_c = lax.dot_general(qh, k_cur, _NT, preferred_element_type=jnp.float32) + bias_cur
                s_p = lax.dot_general(qh, k_prev, _NT, preferred_element_type=jnp.float32) + bp
                m = jnp.maximum(jnp.max(s_c, -1, keepdims=True), jnp.max(s_p, -1, keepdims=True))
                p_c = jnp.exp(s_c - m)
                p_p = jnp.exp(s_p - m)
                den = jnp.sum(p_c, -1, keepdims=True) + jnp.sum(p_p, -1, keepdims=True)
                o = (jnp.dot(p_c.astype(jnp.bfloat16), v_cur, preferred_element_type=jnp.float32)
                     + jnp.dot(p_p.astype(jnp.bfloat16), v_prev, preferred_element_type=jnp.float32))
                outs.append(o / den)
                lses.append(m + jnp.log(den))
            o_ref[rows, cols] = jnp.where(lo, outs[0], outs[1])
            lse_ref[rows, cols] = jnp.where(lo, lses[0], lses[1])


def _dilated(qa, ka, va, dil):
    B, S, W = qa.shape
    L = S // dil
    rows = min(L, 4 * BAND_BLOCK)
    nblk = rows // BAND_BLOCK
    view = lambda t: t.reshape(B, L, dil * W)
    cur = pl.BlockSpec((None, rows, W), lambda b, r, i: (b, i, r))
    prev = pl.BlockSpec((None, BAND_BLOCK, W), lambda b, r, i: (b, jnp.maximum(i * nblk - 1, 0), r))
    o, lse = pl.pallas_call(
        functools.partial(_dil_kernel, nblk=nblk),
        grid=(B, dil, L // rows),
        in_specs=[cur, cur, prev, cur, prev],
        out_specs=[cur, cur],
        out_shape=[jax.ShapeDtypeStruct((B, L, dil * W), jnp.float32)] * 2,
        compiler_params=_cparams(("parallel", "parallel", "arbitrary")),
        name=f"dilated{dil}",
    )(view(qa), view(ka), view(ka), view(va), view(va))
    return o.reshape(B, S, W), lse.reshape(B, S, W)


def _cmp_kernel(ck_ref, cv_ref, pe_ref, wk1_ref, wk2_ref, wv1_ref, wv2t_ref, ok_ref, ovt_ref):
    half = pe_ref.shape[1] // 2
    n = ck_ref.shape[0]

    def hidden(c_ref, w1_ref):
        c = c_ref[...]
        a = jnp.dot((c + pe_ref[:, :half]).astype(jnp.bfloat16), w1_ref[:half, :],
                    preferred_element_type=jnp.float32)
        b = jnp.dot((c + pe_ref[:, half:]).astype(jnp.bfloat16), w1_ref[half:, :],
                    preferred_element_type=jnp.float32)
        return _gelu_tanh(a + pltpu.roll(b, n - 1, 0)).astype(jnp.bfloat16)

    ok_ref[...] = jnp.dot(hidden(ck_ref, wk1_ref), wk2_ref[...],
                          preferred_element_type=jnp.float32).astype(ok_ref.dtype)
    ovt_ref[...] = lax.dot_general(wv2t_ref[...], hidden(cv_ref, wv1_ref), _NT,
                                   preferred_element_type=jnp.float32).astype(ovt_ref.dtype)


def _compress(kc, vc, pe, w_ck1, w_ck2, w_cv1, w_cv2):
    B, S, _ = kc.shape
    n = S // CMP_STRIDE
    cw = CMP_STRIDE * HEAD_DIM

    def chunks(t):
        t = t.reshape(B, n, CMP_STRIDE, NSA_GROUPS, HEAD_DIM).transpose(0, 3, 1, 2, 4)
        return t.reshape(B, NSA_GROUPS, n, cw)

    bf = jnp.bfloat16
    cspec = pl.BlockSpec((None, None, n, cw), lambda b, g: (b, g, 0, 0))
    full = lambda a: pl.BlockSpec(a.shape, lambda b, g: (0,) * a.ndim)
    args = (chunks(kc), chunks(vc), pe.reshape(1, CMP_LEN * HEAD_DIM),
            w_ck1.astype(bf), jnp.concatenate([w_ck2, w_ck2], axis=-1).astype(bf),
            w_cv1.astype(bf), w_cv2.T.astype(bf))
    return pl.pallas_call(
        _cmp_kernel,
        grid=(B, NSA_GROUPS),
        in_specs=[cspec, cspec] + [full(a) for a in args[2:]],
        out_specs=[pl.BlockSpec((None, None, n, LANES), lambda b, g: (b, g, 0, 0)),
                   pl.BlockSpec((None, None, HEAD_DIM, n), lambda b, g: (b, g, 0, 0))],
        out_shape=[jax.ShapeDtypeStruct((B, NSA_GROUPS, n, LANES), bf),
                   jax.ShapeDtypeStruct((B, NSA_GROUPS, HEAD_DIM, n), bf)],
        compiler_params=_cparams(("parallel", "parallel")),
        name="compress",
    )(*args)


def _nsa_cmp_kernel(q_ref, kcc_ref, vcct_ref, ovt_ref, o_ref, sel_ref, *, tq, nsel, topk):
    qi = pl.program_id(2)
    t0 = qi * tq
    nc = kcc_ref.shape[0]
    tpos = t0 + lax.broadcasted_iota(jnp.int32, (nc, tq), 1)
    cend = lax.broadcasted_iota(jnp.int32, (nc, tq), 0) * CMP_STRIDE + (CMP_LEN - 1)
    cbias = jnp.where(cend <= tpos, 0.0, NEG)
    any_vis = (t0 + lax.broadcasted_iota(jnp.int32, (1, tq), 1)) >= CMP_LEN - 1
    kcc, vcct = kcc_ref[...], vcct_ref[...]
    psum = jnp.zeros((nc, tq), jnp.float32)
    for h in range(NSA_HPG):
        qh = q_ref[:, h * LANES:(h + 1) * LANES]
        s = lax.dot_general(kcc, qh, _NT, preferred_element_type=jnp.float32) + cbias
        m = jnp.max(s, 0, keepdims=True)
        p = jnp.exp(s - m)
        den = jnp.sum(p, 0, keepdims=True)
        p = p * jnp.where(any_vis, 1.0 / den, 0.0)
        o_ref[h * HEAD_DIM:(h + 1) * HEAD_DIM, :] = jnp.dot(vcct, p.astype(jnp.bfloat16),
                                                            preferred_element_type=jnp.float32)
        psum = psum + p
    imp = jnp.dot(ovt_ref[...], psum.astype(jnp.bfloat16), preferred_element_type=jnp.float32)
    blk = lax.broadcasted_iota(jnp.int32, (LANES, tq), 0)
    tq_pos = t0 + lax.broadcasted_iota(jnp.int32, (LANES, tq), 1)
    cur = tq_pos // SLC_BLOCK
    forced = (blk == cur) | (blk == cur - 1) | (blk == 0)
    valid = blk <= cur
    score = jnp.where(valid & jnp.logical_not(forced) & (blk < nsel), imp, -jnp.inf)
    blk_f = blk.astype(jnp.float32)
    sel = jnp.where(forced, 1.0, 0.0)
    for _ in range(topk - 3):
        mx = jnp.max(score, 0, keepdims=True)
        first = jnp.min(jnp.where(score == mx, blk_f, float(2 * LANES)), 0, keepdims=True)
        hit = blk_f == first
        sel = jnp.where(hit, 1.0, sel)
        score = jnp.where(hit, -jnp.inf, score)
    sel = jnp.where(valid, sel, 0.0)
    sel_ref[...] = ((sel.T - 1.0) * (-NEG)).astype(jnp.bfloat16)


def _overlap_t(S):
    nc = S // CMP_STRIDE
    ns = S // SLC_BLOCK
    cs = np.arange(nc) * CMP_STRIDE
    ss = np.arange(LANES) * SLC_BLOCK
    ov = np.clip(np.minimum(cs[None, :] + CMP_LEN, ss[:, None] + SLC_BLOCK)
                 - np.maximum(cs[None, :], ss[:, None]), 0, None).astype(np.float32) / CMP_LEN
    ov[ns:, :] = 0.0
    ov[:, nc - CMP_LEN // CMP_STRIDE + 1:] = 0.0
    return jnp.asarray(ov, jnp.bfloat16)


def _nsa_cmp(qbx, kcc, vcct, tq):
    B, S, _ = qbx.shape
    nc = kcc.shape[2]
    ns = S // SLC_BLOCK
    gw = NSA_HPG * LANES
    ovt = _overlap_t(S)
    return pl.pallas_call(
        functools.partial(_nsa_cmp_kernel, tq=tq, nsel=ns, topk=min(SLC_TOPK, ns)),
        grid=(B, NSA_GROUPS, S // tq),
        in_specs=[pl.BlockSpec((None, tq, gw), lambda b, g, i: (b, i, g)),
                  pl.BlockSpec((None, None, nc, LANES), lambda b, g, i: (b, g, 0, 0)),
                  pl.BlockSpec((None, None, HEAD_DIM, nc), lambda b, g, i: (b, g, 0, 0)),
                  pl.BlockSpec((LANES, nc), lambda b, g, i: (0, 0))],
        out_specs=[pl.BlockSpec((None, NSA_HPG * HEAD_DIM, tq), lambda b, g, i: (b, g, i)),
                   pl.BlockSpec((None, None, tq, LANES), lambda b, g, i: (b, g, i, 0))],
        out_shape=[jax.ShapeDtypeStruct((B, WB, S), jnp.float32),
                   jax.ShapeDtypeStruct((B, NSA_GROUPS, S, LANES), jnp.bfloat16)],
        compiler_params=_cparams(("parallel", "parallel", "parallel")),
        name="nsa_cmp",
    )(qbx, kcc, vcct, ovt)


def _nsa_main_kernel(q_ref, sel_ref, ksa_ref, vst_ref, kw_ref, vwt_ref, ocmp_ref, gate_ref, o_ref,
                     qa_scr, m_scr, l_scr, acc_scr, sa_scr, sb_scr, *, tq, tk):
    g = pl.program_id(1)
    qi = pl.program_id(2)
    t0 = qi * tq
    cols = NSA_HPG * tq
    selb = sel_ref[...]
    for h in range(NSA_HPG):
        qa_scr[h * tq:(h + 1) * tq, 0:LANES] = selb
        qa_scr[h * tq:(h + 1) * tq, LANES:2 * LANES] = q_ref[:, h * LANES:(h + 1) * LANES]
    m_scr[...] = jnp.full((1, cols), -1e37, jnp.float32)
    l_scr[...] = jnp.zeros((1, cols), jnp.float32)
    acc_scr[...] = jnp.zeros((HEAD_DIM, cols), jnp.float32)

    def scores(tile, dst):
        start = pl.multiple_of(tile * tk, tk)
        dst[...] = lax.dot_general(ksa_ref[pl.ds(start, tk), :], qa_scr[...], _NT,
                                   preferred_element_type=jnp.float32)

    def consume(tile, src):
        start = pl.multiple_of(tile * tk, tk)
        vt = vst_ref[:, pl.ds(start, tk)]
        m_prev = m_scr[...]
        m_new = jnp.maximum(m_prev, jnp.max(src[...], 0, keepdims=True))
        alpha = jnp.exp(m_prev - m_new)
        p = jnp.exp(src[...] - m_new)
        l_scr[...] = alpha * l_scr[...] + jnp.sum(p, 0, keepdims=True)
        acc_scr[...] = alpha * acc_scr[...] + jnp.dot(vt, p.astype(jnp.bfloat16),
                                                      preferred_element_type=jnp.float32)
        m_scr[...] = m_new

    last = t0 // tk
    scores(0, sa_scr)

    def pair_body(jj, carry):
        scores(2 * jj + 1, sb_scr)
        consume(2 * jj, sa_scr)
        scores(2 * jj + 2, sa_scr)
        consume(2 * jj + 1, sb_scr)
        return carry

    lax.fori_loop(0, last // 2, pair_body, 0)

    kk = lax.broadcasted_iota(jnp.int32, (tq, tq), 0)
    qq = lax.broadcasted_iota(jnp.int32, (tq, tq), 1)
    cb = jnp.where(kk <= qq, 0.0, NEG)
    cb = jnp.concatenate([cb] * NSA_HPG, axis=1)
    diag = pl.multiple_of(t0 - last * tk, tq)

    def finish(src):
        src[pl.ds(diag, tq), :] = src[pl.ds(diag, tq), :] + cb
        consume(last, src)

    @pl.when(last % 2 == 1)
    def _():
        scores(last, sb_scr)
        consume(last - 1, sa_scr)
        finish(sb_scr)

    @pl.when(last % 2 == 0)
    def _():
        finish(sa_scr)

    o_slc = acc_scr[...] * (1.0 / l_scr[...])

    wk = WIN + tq
    start = pl.multiple_of(jnp.maximum(t0 - WIN, 0), tq)
    kwin = kw_ref[pl.ds(start, wk), :]
    vwt = vwt_ref[:, pl.ds(start, wk)]
    sw = lax.dot_general(kwin, qa_scr[:, LANES:2 * LANES], _NT, preferred_element_type=jnp.float32)
    dist = (lax.broadcasted_iota(jnp.int32, (wk, tq), 1) + t0) - (lax.broadcasted_iota(jnp.int32, (wk, tq), 0) + start)
    wb = jnp.where((dist >= 0) & (dist < WIN), 0.0, NEG)
    sw = sw + jnp.concatenate([wb] * NSA_HPG, axis=1)
    mw = jnp.max(sw, 0, keepdims=True)
    pw = jnp.exp(sw - mw)
    denw = jnp.sum(pw, 0, keepdims=True)
    o_win = jnp.dot(vwt, pw.astype(jnp.bfloat16), preferred_element_type=jnp.float32) * (1.0 / denw)

    base = g * (NSA_HPG * 3)
    heads = []
    for h in range(NSA_HPG):
        c = slice(h * tq, (h + 1) * tq)
        g_cmp = gate_ref[pl.ds(base + 3 * h, 1), :]
        g_slc = gate_ref[pl.ds(base + 3 * h + 1, 1), :]
        g_win = gate_ref[pl.ds(base + 3 * h + 2, 1), :]
        heads.append(g_cmp * ocmp_ref[h * HEAD_DIM:(h + 1) * HEAD_DIM, :] + g_slc * o_slc[:, c] + g_win * o_win[:, c])
    o_ref[...] = jnp.concatenate(heads, axis=0).T.astype(o_ref.dtype)


def _nsa_main(qbx, selb, ksa, vst, kw, vwt, o_cmp_t, gates_t, tq, tk):
    B, S, _ = qbx.shape
    gw = NSA_HPG * LANES
    ow = NSA_HPG * HEAD_DIM
    cols = NSA_HPG * tq
    seq = lambda w: pl.BlockSpec((None, S, w), lambda b, g, i: (b, 0, 0))
    seqt = pl.BlockSpec((None, HEAD_DIM, S), lambda b, g, i: (b, g, 0))
    return pl.pallas_call(
        functools.partial(_nsa_main_kernel, tq=tq, tk=tk),
        grid=(B, NSA_GROUPS, S // tq),
        in_specs=[pl.BlockSpec((None, tq, gw), lambda b, g, i: (b, i, g)),
                  pl.BlockSpec((None, None, tq, LANES), lambda b, g, i: (b, g, i, 0)),
                  seq(2 * LANES), seqt, seq(WG), seqt,
                  pl.BlockSpec((None, ow, tq), lambda b, g, i: (b, g, i)),
                  pl.BlockSpec((None, LANES, tq), lambda b, g, i: (b, 0, i))],
        out_specs=pl.BlockSpec((None, tq, ow), lambda b, g, i: (b, i, g)),
        out_shape=jax.ShapeDtypeStruct((B, S, WB), jnp.bfloat16),
        scratch_shapes=[pltpu.VMEM((cols, 2 * LANES), jnp.bfloat16),
                        pltpu.VMEM((1, cols), jnp.float32),
                        pltpu.VMEM((1, cols), jnp.float32),
                        pltpu.VMEM((HEAD_DIM, cols), jnp.float32),
                        pltpu.VMEM((tk, cols), jnp.float32),
                        pltpu.VMEM((tk, cols), jnp.float32)],
        compiler_params=_cparams(("parallel", "parallel", "arbitrary")),
        name="nsa_main",
    )(qbx, selb, ksa, vst, kw, vwt, o_cmp_t, gates_t)


def _outproj_kernel(x_ref, o1_ref, o2_ref, o3_ref, l1_ref, l2_ref, l3_ref, ob_ref, mod_ref, wo_ref, g_ref, b_ref,
                    y_ref):
    l1, l2, l3 = l1_ref[...], l2_ref[...], l3_ref[...]
    m = jnp.maximum(jnp.maximum(l1, l2), l3)
    e1, e2, e3 = jnp.exp(l1 - m), jnp.exp(l2 - m), jnp.exp(l3 - m)
    o_a = (e1 * o1_ref[...] + e2 * o2_ref[...] + e3 * o3_ref[...]) / (e1 + e2 + e3)
    y = (jnp.dot(o_a.astype(jnp.bfloat16), wo_ref[:WA, :], preferred_element_type=jnp.float32)
         + jnp.dot(ob_ref[...], wo_ref[WA:, :], preferred_element_type=jnp.float32))
    z = ALPHA * x_ref[...] + mod_ref[2:3, :] * y
    y_ref[...] = _ln(z) * g_ref[...] + b_ref[...]


def _outproj(x, o_pat, lse_pat, o_b, mod6, wo_bf16, ln_g, ln_b, tm):
    B, S, D = x.shape
    bs = lambda w: pl.BlockSpec((None, tm, w), lambda b, i: (b, i, 0))
    vec = pl.BlockSpec((1, D), lambda b, i: (0, 0))
    return pl.pallas_call(
        _outproj_kernel,
        grid=(B, S // tm),
        in_specs=[bs(D)] + [bs(WA)] * 6 + [bs(WB),
                  pl.BlockSpec((None, 6, D), lambda b, i: (b, 0, 0)),
                  pl.BlockSpec((WA + WB, D), lambda b, i: (0, 0)), vec, vec],
        out_specs=bs(D),
        out_shape=jax.ShapeDtypeStruct((B, S, D), jnp.float32),
        compiler_params=_cparams(("parallel", "parallel")),
        name="outproj",
    )(x, *o_pat, *lse_pat, o_b, mod6, wo_bf16, ln_g.reshape(1, D), ln_b.reshape(1, D))


def _ffn_kernel(x_ref, mod_ref, wup_ref, cw_ref, cb_ref, wdn_ref, g_ref, b_ref, y_ref,
                u_scr, buf_scr, carry_scr, acc_scr, *, tm, fc):
    i = pl.program_id(1)
    F = wdn_ref.shape[0]
    pad = 8

    @pl.when(i == 0)
    def _():
        carry_scr[...] = jnp.zeros_like(carry_scr)

    x = x_ref[...]
    u_scr[...] = (_ln(x) * (1.0 + mod_ref[4:5, :]) + mod_ref[3:4, :]).astype(jnp.bfloat16)
    for c in range(F // fc):
        cols = slice(c * fc, (c + 1) * fc)
        a_gate = jnp.dot(u_scr[...], wup_ref[:, c * fc:(c + 1) * fc], preferred_element_type=jnp.float32)
        a_val = jnp.dot(u_scr[...], wup_ref[:, F + c * fc:F + (c + 1) * fc], preferred_element_type=jnp.float32)
        buf_scr[0:pad, :] = carry_scr[:, cols]
        buf_scr[pad:pad + tm, :] = a_gate
        carry_scr[:, cols] = a_gate[tm - pad:tm, :]
        conv = (cw_ref[0:1, cols] * buf_scr[pad - 2:pad - 2 + tm, :]
                + cw_ref[1:2, cols] * buf_scr[pad - 1:pad - 1 + tm, :]
                + cw_ref[2:3, cols] * a_gate + cb_ref[:, cols])
        h = (_gelu_tanh(conv) * a_val).astype(jnp.bfloat16)
        d = jnp.dot(h, wdn_ref[cols, :], preferred_element_type=jnp.float32)
        if c == 0:
            acc_scr[...] = d
        else:
            acc_scr[...] += d
    z = ALPHA * x + mod_ref[5:6, :] * acc_scr[...]
    y_ref[...] = _ln(z) * g_ref[...] + b_ref[...]


def _ffn(x, mod6, wup_bf16, conv_w, conv_b, wdn_bf16, ln_g, ln_b, tm, fc):
    B, S, D = x.shape
    F = wdn_bf16.shape[0]
    bs = pl.BlockSpec((None, tm, D), lambda b, i: (b, i, 0))
    full = lambda a: pl.BlockSpec(a.shape, lambda b, i: (0,) * a.ndim)
    args = (wup_bf16, conv_w, conv_b.reshape(1, F), wdn_bf16, ln_g.reshape(1, D), ln_b.reshape(1, D))
    return pl.pallas_call(
        functools.partial(_ffn_kernel, tm=tm, fc=fc),
        grid=(B, S // tm),
        in_specs=[bs, pl.BlockSpec((None, 6, D), lambda b, i: (b, 0, 0))] + [full(a) for a in args],
        out_specs=bs,
        out_shape=jax.ShapeDtypeStruct((B, S, D), jnp.float32),
        scratch_shapes=[pltpu.VMEM((tm, D), jnp.bfloat16),
                        pltpu.VMEM((tm + 8, fc), jnp.float32),
                        pltpu.VMEM((8, F), jnp.float32),
                        pltpu.VMEM((tm, D), jnp.float32)],
        compiler_params=_cparams(("arbitrary", "arbitrary")),
        name="ffn",
    )(x, mod6, *args)


def kernel(x, c, w_ada, b_ada, w_in, pe_cmp, w_ck1, w_ck2, w_cv1, w_cv2, w_o, ln1_g, ln1_b, w_up, conv_w, conv_b,
           w_down, ln2_g, ln2_b):
    B, S, D = x.shape
    tables = _rope_tables(S)
    tm = min(512, S)
    tq = 128
    tk = min(512, S)
    for l in range(DEPTH):
        mod6 = _ada(c, w_ada[l], b_ada[l]).reshape(B, 6, D)
        w_nat, w_tr = _prep_w_in(w_in[l])
        (qa, ka, va, qbx, kc, vc, ksa, kw, vst, vwt, gates_t) = _inproj(x, mod6, w_nat, w_tr, tables, tm)
        pats = [_dilated(qa, ka, va, dil) for _, dil in DIL_PATTERNS]
        kcc, vcct = _compress(kc, vc, pe_cmp[l], w_ck1[l], w_ck2[l], w_cv1[l], w_cv2[l])
        o_cmp_t, selb = _nsa_cmp(qbx, kcc, vcct, 2 * tq)
        o_b = _nsa_main(qbx, selb, ksa, vst, kw, vwt, o_cmp_t, gates_t, tq, tk)
        x = _outproj(x, [p[0] for p in pats], [p[1] for p in pats], o_b, mod6, w_o[l].astype(jnp.bfloat16),
                     ln1_g[l], ln1_b[l], tm)
        x = _ffn(x, mod6, w_up[l].astype(jnp.bfloat16), conv_w[l], conv_b[l], w_down[l].astype(jnp.bfloat16),
                 ln2_g[l], ln2_b[l], tm, 256)
    return x
```

```python
import functools
import math

import numpy as np
import jax
import jax.numpy as jnp
from jax import lax
from jax.experimental import pallas as pl
from jax.experimental.pallas import tpu as pltpu

HEAD_DIM = 64
N_HEADS_A = 8
N_HEADS_B = 8
DIL_PATTERNS = ((128, 1), (512, 4), (2048, 16))
BAND_BLOCK = 128
ROT_DIM = HEAD_DIM // 4
ROPE_THETA = 500000.0
NSA_GROUPS = 2
NSA_HPG = N_HEADS_B // NSA_GROUPS
CMP_LEN = 32
CMP_STRIDE = 16
CMP_HIDDEN = 4 * HEAD_DIM
SLC_BLOCK = 64
SLC_TOPK = 16
WIN = 512
D_FF = 2816
CONV_W = 3
DEPTH = 1
ALPHA = (2 * DEPTH) ** 0.25
LN_EPS = 1e-5
NEG = -1e30
LOG2E = math.log2(math.e)

LANES = 128
MXU_N = 256
DEN_ROWS = 16
WA = N_HEADS_A * HEAD_DIM
WB = N_HEADS_B * HEAD_DIM
WG = NSA_GROUPS * HEAD_DIM
N_GATES = N_HEADS_B * 3
SRC_VS = 3 * WA + WB + 3 * WG
SRC_KW = SRC_VS + WG
SRC_VW = SRC_KW + WG
SRC_GL = SRC_VW + WG
OFF_QA, OFF_KA, OFF_VA, OFF_QB = 0, WA, 2 * WA, 3 * WA
OFF_KC = OFF_QB + WB
OFF_VC = OFF_KC + WG
OFF_KS = OFF_VC + WG
OFF_KW = OFF_KS + WG
NAT_WIDTH = OFF_KW + WG
TR_VS, TR_VW, TR_GL = 0, WG, 2 * WG
TR_WIDTH = 3 * WG

VMEM_LIMIT = 56 * 1024 * 1024

_NT = (((1,), (1,)), ((), ()))


def _cparams(sem):
    return pltpu.CompilerParams(dimension_semantics=sem, vmem_limit_bytes=VMEM_LIMIT)


def _gelu_tanh(x):
    return 0.5 * x * (1.0 + jnp.tanh(math.sqrt(2.0 / math.pi) * (x + 0.044715 * (x * x * x))))


def _ln(x):
    mu = jnp.mean(x, -1, keepdims=True)
    xc = x - mu
    var = jnp.mean(xc * xc, -1, keepdims=True)
    return xc * lax.rsqrt(var + LN_EPS)


def _ada_kernel(c_ref, w_ref, b_ref, o_ref):
    c = c_ref[...]
    a = c * jax.nn.sigmoid(c)
    o_ref[...] = jnp.dot(a, w_ref[...], preferred_element_type=jnp.float32) + b_ref[...]


def _ada(c, w, b):
    B, D = c.shape
    N = w.shape[1]
    tn = D
    return pl.pallas_call(
        _ada_kernel,
        grid=(N // tn,),
        in_specs=[pl.BlockSpec((B, D), lambda j: (0, 0)),
                  pl.BlockSpec((D, tn), lambda j: (0, j)),
                  pl.BlockSpec((1, tn), lambda j: (0, j))],
        out_specs=pl.BlockSpec((B, tn), lambda j: (0, j)),
        out_shape=jax.ShapeDtypeStruct((B, N), jnp.float32),
        compiler_params=_cparams(("arbitrary",)),
        name="adaln",
    )(c, w, b.reshape(1, N))


def _rope_tile(t, cos, s_lo, s_hi):
    half = ROT_DIM // 2
    return t * cos + pltpu.roll(t, half, 1) * s_hi + pltpu.roll(t, LANES - half, 1) * s_lo


def _inproj_kernel(x_ref, mod_ref, w_ref, wt_ref, cos_ref, slo_ref, shi_ref,
                   qa_ref, ka_ref, va_ref, qbx_ref, kc_ref, vc_ref, ksa_ref, kw_ref, vst_ref, vwt_ref, gate_ref,
                   q4_ref, k4_ref, v4_ref, q16_ref, k16_ref, v16_ref,
                   u_ref, ra_ref, rb_ref, *, tm):
    i = pl.program_id(1)
    x = x_ref[...]
    u = _ln(x) * (1.0 + mod_ref[1:2, :]) + mod_ref[0:1, :]
    u_ref[...] = u.astype(jnp.bfloat16)
    cos, slo, shi = cos_ref[...], slo_ref[...], shi_ref[...]
    lane = lax.broadcasted_iota(jnp.int32, (1, LANES), 1)
    bf = jnp.bfloat16

    def proj(off):
        r = jnp.dot(u_ref[...], w_ref[:, off:off + MXU_N], preferred_element_type=jnp.float32)
        return r[:, :LANES], r[:, LANES:]

    def rope(t):
        return _rope_tile(t, cos, slo, shi)

    def emit(t, cols, nat_ref, r4_ref, r16_ref):
        nat_ref[:, cols] = t.astype(bf)
        ra_ref[...] = t
        n4 = tm // 4
        for r in range(4):
            part = ra_ref[pl.ds(r, n4, stride=4), :]
            r4_ref[r, :, cols] = part.astype(bf)
            rb_ref[r * n4:(r + 1) * n4, :] = part
        for r in range(4):
            for m in range(4):
                part = rb_ref[pl.ds(r * n4 + m, n4 // 4, stride=4), :]
                r16_ref[r + 4 * m, :, cols] = part.astype(bf)

    for j in range(WA // MXU_N):
        for k, t in enumerate(proj(OFF_QA + j * MXU_N)):
            emit(rope(t) * LOG2E, slice((2 * j + k) * LANES, (2 * j + k + 1) * LANES), qa_ref, q4_ref, q16_ref)
        for k, t in enumerate(proj(OFF_KA + j * MXU_N)):
            emit(rope(t), slice((2 * j + k) * LANES, (2 * j + k + 1) * LANES), ka_ref, k4_ref, k16_ref)
        for k, t in enumerate(proj(OFF_VA + j * MXU_N)):
            emit(t, slice((2 * j + k) * LANES, (2 * j + k + 1) * LANES), va_ref, v4_ref, v16_ref)
    lo = lane < HEAD_DIM
    for j in range(WB // MXU_N):
        for k, t in enumerate(proj(OFF_QB + j * MXU_N)):
            pair = 2 * j + k
            t = rope(t) * LOG2E
            g = (2 * pair) // NSA_HPG
            tr = pltpu.roll(t, HEAD_DIM, 1)
            in_g = lo if g == 0 else jnp.logical_not(lo)
            even = jnp.where(in_g, t if g == 0 else tr, 0.0)
            odd = jnp.where(in_g, tr if g == 0 else t, 0.0)
            qbx_ref[:, (2 * pair) * LANES:(2 * pair + 1) * LANES] = even.astype(bf)
            qbx_ref[:, (2 * pair + 1) * LANES:(2 * pair + 2) * LANES] = odd.astype(bf)
    kc, vc = proj(OFF_KC)
    kc_ref[...] = rope(kc)
    vc_ref[...] = vc
    ks, kw = proj(OFF_KS)
    tpos = i * tm + lax.broadcasted_iota(jnp.int32, (tm, LANES), 0)
    blk = lax.broadcasted_iota(jnp.int32, (tm, LANES), 1)
    ksa_ref[:, 0:LANES] = jnp.where((tpos // SLC_BLOCK) == blk, 1.0, 0.0).astype(bf)
    ksa_ref[:, LANES:2 * LANES] = rope(ks).astype(bf)
    kw_ref[...] = rope(kw).astype(bf)
    tr_out = lax.dot_general(wt_ref[...], u_ref[...], _NT, preferred_element_type=jnp.float32)
    vst_ref[...] = tr_out[TR_VS:TR_VS + WG].astype(bf)
    vwt_ref[...] = tr_out[TR_VW:TR_VW + WG].astype(bf)
    gate_ref[...] = jax.nn.sigmoid(tr_out[TR_GL:TR_GL + LANES])


def _rope_tables(S):
    inv = 1.0 / (ROPE_THETA ** (jnp.arange(0, ROT_DIM, 2, dtype=jnp.float32) / ROT_DIM))
    ang = jnp.arange(S, dtype=jnp.float32)[:, None] * inv[None, :]
    cos, sin = jnp.cos(ang), jnp.sin(ang)
    half = ROT_DIM // 2
    d = np.arange(LANES) % HEAD_DIM
    idx = jnp.asarray(d % half)
    rot = jnp.asarray(d < ROT_DIM)
    lo = jnp.asarray(d < half)
    hi = jnp.asarray((d >= half) & (d < ROT_DIM))
    c_t = jnp.where(rot[None], cos[:, idx], 1.0)
    s_lo = jnp.where(lo[None], -sin[:, idx], 0.0)
    s_hi = jnp.where(hi[None], sin[:, idx], 0.0)
    return c_t, s_lo, s_hi


def _prep_w_in(w):
    q_scale = HEAD_DIM ** -0.5
    w_nat = jnp.concatenate([w[:, 0:WA] * q_scale, w[:, WA:3 * WA], w[:, 3 * WA:3 * WA + WB] * q_scale,
                             w[:, 3 * WA + WB:SRC_VS], w[:, SRC_KW:SRC_VW]], axis=1)
    w_tr = jnp.concatenate([w[:, SRC_VS:SRC_KW], w[:, SRC_VW:SRC_GL],
                            jnp.pad(w[:, SRC_GL:], ((0, 0), (0, LANES - N_GATES)))], axis=1)
    return w_nat.astype(jnp.bfloat16), w_tr.T.astype(jnp.bfloat16)


def _inproj(x, mod6, w_nat, w_tr, tables, tm):
    B, S, D = x.shape
    cos, slo, shi = tables
    bs = lambda w: pl.BlockSpec((None, tm, w), lambda b, i: (b, i, 0))
    bst = pl.BlockSpec((None, LANES, tm), lambda b, i: (b, 0, i))
    tab = pl.BlockSpec((tm, LANES), lambda b, i: (i, 0))
    sd = lambda w, dt: jax.ShapeDtypeStruct((B, S, w), dt)
    sdt = lambda dt: jax.ShapeDtypeStruct((B, LANES, S), dt)
    bf, f32 = jnp.bfloat16, jnp.float32
    res = lambda d: pl.BlockSpec((None, d, tm // d, WA), lambda b, i: (b, 0, i, 0))
    sdr = lambda d: jax.ShapeDtypeStruct((B, d, S // d, WA), bf)
    return pl.pallas_call(
        functools.partial(_inproj_kernel, tm=tm),
        grid=(B, S // tm),
        in_specs=[bs(D),
                  pl.BlockSpec((None, 6, D), lambda b, i: (b, 0, 0)),
                  pl.BlockSpec((D, NAT_WIDTH), lambda b, i: (0, 0)),
                  pl.BlockSpec((TR_WIDTH, D), lambda b, i: (0, 0)),
                  tab, tab, tab],
        out_specs=[bs(WA), bs(WA), bs(WA), bs(2 * WB), bs(WG), bs(WG), bs(2 * LANES), bs(WG), bst, bst, bst]
                  + [res(4)] * 3 + [res(16)] * 3,
        out_shape=[sd(WA, bf), sd(WA, bf), sd(WA, bf), sd(2 * WB, bf), sd(WG, f32), sd(WG, f32),
                   sd(2 * LANES, bf), sd(WG, bf), sdt(bf), sdt(bf), sdt(f32)]
                  + [sdr(4)] * 3 + [sdr(16)] * 3,
        scratch_shapes=[pltpu.VMEM((tm, D), jnp.bfloat16),
                        pltpu.VMEM((tm, LANES), jnp.float32),
                        pltpu.VMEM((tm, LANES), jnp.float32)],
        compiler_params=_cparams(("parallel", "parallel")),
        name="inproj",
    )(x, mod6, w_nat, w_tr, cos, slo, shi)


def _dil_kernel(q_ref, kc_ref, kp_ref, vc_ref, vp_ref, o_ref, lse_ref, *, nblk):
    i = pl.program_id(2)
    blk_rows = BAND_BLOCK
    qi = lax.broadcasted_iota(jnp.int32, (blk_rows, blk_rows), 0)
    kj = lax.broadcasted_iota(jnp.int32, (blk_rows, blk_rows), 1)
    bias_cur = jnp.where(kj <= qi, 0.0, NEG)
    bias_prev = jnp.where(kj >= qi, 0.0, NEG)
    bias_prev0 = jnp.where(i > 0, bias_prev, NEG)
    lane = lax.broadcasted_iota(jnp.int32, (1, LANES), 1)
    lo = lane < HEAD_DIM
    n_pair = WA // LANES
    for blk in range(nblk):
        rows = slice(blk * blk_rows, (blk + 1) * blk_rows)
        prow = slice((blk - 1) * blk_rows, blk * blk_rows)
        bp = bias_prev0 if blk == 0 else bias_prev
        vals, s_cur, s_prev = [], [], []
        for hp in range(n_pair):
            cols = slice(hp * LANES, (hp + 1) * LANES)
            qt = q_ref[rows, cols]
            k_cur = kc_ref[rows, cols]
            k_prev = kp_ref[:, cols] if blk == 0 else kc_ref[prow, cols]
            vals.append((vc_ref[rows, cols], vp_ref[:, cols] if blk == 0 else vc_ref[prow, cols]))
            for h in range(2):
                qh = jnp.where(lo if h == 0 else jnp.logical_not(lo), qt, jnp.zeros_like(qt))
                s_cur.append(lax.dot_general(qh, k_cur, _NT, preferred_element_type=jnp.float32) + bias_cur)
                s_prev.append(lax.dot_general(qh, k_prev, _NT, preferred_element_type=jnp.float32) + bp)
        ms = [jnp.max(jnp.maximum(a, b), -1, keepdims=True) for a, b in zip(s_cur, s_prev)]
        p_cur = [jnp.exp2(a - m) for a, m in zip(s_cur, ms)]
        p_prev = [jnp.exp2(a - m) for a, m in zip(s_prev, ms)]
        dens = [jnp.sum(a + b, -1, keepdims=True) for a, b in zip(p_cur, p_prev)]
        for hp in range(n_pair):
            cols = slice(hp * LANES, (hp + 1) * LANES)
            v_cur, v_prev = vals[hp]
            outs, lses = [], []
            for h in range(2):
                n = 2 * hp + h
                o = (jnp.dot(p_cur[n].astype(jnp.bfloat16), v_cur, preferred_element_type=jnp.float32)
                     + jnp.dot(p_prev[n].astype(jnp.bfloat16), v_prev, preferred_element_type=jnp.float32))
                outs.append(o * (1.0 / dens[n]))
                lses.append(ms[n] + jnp.log2(dens[n]))
            o_ref[rows, cols] = jnp.where(lo, outs[0], outs[1])
            lse_ref[rows, cols] = jnp.where(lo, lses[0], lses[1])


def _dilated(q, k, v, name):
    B, dil, L, W = q.shape
    rows = min(L, 4 * BAND_BLOCK)
    nblk = rows // BAND_BLOCK
    cur = pl.BlockSpec((None, None, rows, W), lambda b, r, i: (b, r, i, 0))
    prev = pl.BlockSpec((None, None, BAND_BLOCK, W), lambda b, r, i: (b, r, jnp.maximum(i * nblk - 1, 0), 0))
    return pl.pallas_call(
        functools.partial(_dil_kernel, nblk=nblk),
        grid=(B, dil, L // rows),
        in_specs=[cur, cur, prev, cur, prev],
        out_specs=[cur, cur],
        out_shape=[jax.ShapeDtypeStruct((B, dil, L, W), jnp.float32)] * 2,
        compiler_params=_cparams(("parallel", "parallel", "arbitrary")),
        name=name,
    )(q, k, k, v, v)


def _cmp_kernel(ck_ref, cv_ref, pe_ref, wk1_ref, wk2_ref, wv1_ref, wv2t_ref, ok_ref, ovt_ref):
    half = pe_ref.shape[1] // 2
    n = ck_ref.shape[0]

    def hidden(c_ref, w1_ref):
        c = c_ref[...]
        a = jnp.dot((c + pe_ref[:, :half]).astype(jnp.bfloat16), w1_ref[:half, :],
                    preferred_element_type=jnp.float32)
        b = jnp.dot((c + pe_ref[:, half:]).astype(jnp.bfloat16), w1_ref[half:, :],
                    preferred_element_type=jnp.float32)
        return _gelu_tanh(a + pltpu.roll(b, n - 1, 0)).astype(jnp.bfloat16)

    ok_ref[...] = jnp.dot(hidden(ck_ref, wk1_ref), wk2_ref[...],
                          preferred_element_type=jnp.float32).astype(ok_ref.dtype)
    ovt_ref[...] = lax.dot_general(wv2t_ref[...], hidden(cv_ref, wv1_ref), _NT,
                                   preferred_element_type=jnp.float32).astype(ovt_ref.dtype)


def _compress(kc, vc, pe, w_ck1, w_ck2, w_cv1, w_cv2):
    B, S, _ = kc.shape
    n = S // CMP_STRIDE
    cw = CMP_STRIDE * HEAD_DIM

    def chunks(t):
        t = t.reshape(B, n, CMP_STRIDE, NSA_GROUPS, HEAD_DIM).transpose(0, 3, 1, 2, 4)
        return t.reshape(B, NSA_GROUPS, n, cw)

    bf = jnp.bfloat16
    cspec = pl.BlockSpec((None, None, n, cw), lambda b, g: (b, g, 0, 0))
    full = lambda a: pl.BlockSpec(a.shape, lambda b, g: (0,) * a.ndim)
    args = (chunks(kc), chunks(vc), pe.reshape(1, CMP_LEN * HEAD_DIM),
            w_ck1.astype(bf), jnp.concatenate([w_ck2, w_ck2], axis=-1).astype(bf),
            w_cv1.astype(bf), w_cv2.T.astype(bf))
    return pl.pallas_call(
        _cmp_kernel,
        grid=(B, NSA_GROUPS),
        in_specs=[cspec, cspec] + [full(a) for a in args[2:]],
        out_specs=[pl.BlockSpec((None, None, n, LANES), lambda b, g: (b, g, 0, 0)),
                   pl.BlockSpec((None, None, HEAD_DIM, n), lambda b, g: (b, g, 0, 0))],
        out_shape=[jax.ShapeDtypeStruct((B, NSA_GROUPS, n, LANES), bf),
                   jax.ShapeDtypeStruct((B, NSA_GROUPS, HEAD_DIM, n), bf)],
        compiler_params=_cparams(("parallel", "parallel")),
        name="compress",
    )(*args)


def _nsa_cmp_kernel(q_ref, kcc_ref, vcct_ref, ovt_ref, o_ref, sel_ref, *, tq, nsel, topk):
    qi = pl.program_id(2)
    t0 = qi * tq
    nc = kcc_ref.shape[0]
    tpos = t0 + lax.broadcasted_iota(jnp.int32, (nc, tq), 1)
    cend = lax.broadcasted_iota(jnp.int32, (nc, tq), 0) * CMP_STRIDE + (CMP_LEN - 1)
    cbias = jnp.where(cend <= tpos, 0.0, NEG)
    any_vis = (t0 + lax.broadcasted_iota(jnp.int32, (1, tq), 1)) >= CMP_LEN - 1
    kcc, vcct = kcc_ref[...], vcct_ref[...]
    ss = [lax.dot_general(kcc, q_ref[:, h * LANES:(h + 1) * LANES], _NT, preferred_element_type=jnp.float32) + cbias
          for h in range(NSA_HPG)]
    ms = [jnp.max(s, 0, keepdims=True) for s in ss]
    ps = [jnp.exp2(s - m) for s, m in zip(ss, ms)]
    ps = [p * jnp.where(any_vis, 1.0 / jnp.sum(p, 0, keepdims=True), 0.0) for p in ps]
    for h in range(NSA_HPG):
        o_ref[h * HEAD_DIM:(h + 1) * HEAD_DIM, :] = jnp.dot(vcct, ps[h].astype(jnp.bfloat16),
                                                            preferred_element_type=jnp.float32)
    psum = ps[0]
    for p in ps[1:]:
        psum = psum + p
    imp = jnp.dot(ovt_ref[...], psum.astype(jnp.bfloat16), preferred_element_type=jnp.float32)
    blk = lax.broadcasted_iota(jnp.int32, (LANES, tq), 0)
    tq_pos = t0 + lax.broadcasted_iota(jnp.int32, (LANES, tq), 1)
    cur = tq_pos // SLC_BLOCK
    forced = (blk == cur) | (blk == cur - 1) | (blk == 0)
    valid = blk <= cur
    score = jnp.where(valid & jnp.logical_not(forced) & (blk < nsel), imp, -jnp.inf)
    blk_f = blk.astype(jnp.float32)
    sel = jnp.where(forced, 1.0, 0.0)
    for _ in range(topk - 3):
        mx = jnp.max(score, 0, keepdims=True)
        first = jnp.min(jnp.where(score == mx, blk_f, float(2 * LANES)), 0, keepdims=True)
        hit = blk_f == first
        sel = jnp.where(hit, 1.0, sel)
        score = jnp.where(hit, -jnp.inf, score)
    sel = jnp.where(valid, sel, 0.0)
    sel_ref[...] = ((sel.T - 1.0) * (-NEG)).astype(jnp.bfloat16)


def _overlap_t(S):
    nc = S // CMP_STRIDE
    ns = S // SLC_BLOCK
    cs = np.arange(nc) * CMP_STRIDE
    ss = np.arange(LANES) * SLC_BLOCK
    ov = np.clip(np.minimum(cs[None, :] + CMP_LEN, ss[:, None] + SLC_BLOCK)
                 - np.maximum(cs[None, :], ss[:, None]), 0, None).astype(np.float32) / CMP_LEN
    ov[ns:, :] = 0.0
    ov[:, nc - CMP_LEN // CMP_STRIDE + 1:] = 0.0
    return jnp.asarray(ov, jnp.bfloat16)


def _nsa_cmp(qbx, kcc, vcct, tq):
    B, S, _ = qbx.shape
    nc = kcc.shape[2]
    ns = S // SLC_BLOCK
    gw = NSA_HPG * LANES
    ovt = _overlap_t(S)
    return pl.pallas_call(
        functools.partial(_nsa_cmp_kernel, tq=tq, nsel=ns, topk=min(SLC_TOPK, ns)),
        grid=(B, NSA_GROUPS, S // tq),
        in_specs=[pl.BlockSpec((None, tq, gw), lambda b, g, i: (b, i, g)),
                  pl.BlockSpec((None, None, nc, LANES), lambda b, g, i: (b, g, 0, 0)),
                  pl.BlockSpec((None, None, HEAD_DIM, nc), lambda b, g, i: (b, g, 0, 0)),
                  pl.BlockSpec((LANES, nc), lambda b, g, i: (0, 0))],
        out_specs=[pl.BlockSpec((None, NSA_HPG * HEAD_DIM, tq), lambda b, g, i: (b, g, i)),
                   pl.BlockSpec((None, None, tq, LANES), lambda b, g, i: (b, g, i, 0))],
        out_shape=[jax.ShapeDtypeStruct((B, WB, S), jnp.float32),
                   jax.ShapeDtypeStruct((B, NSA_GROUPS, S, LANES), jnp.bfloat16)],
        compiler_params=_cparams(("parallel", "parallel", "parallel")),
        name="nsa_cmp",
    )(qbx, kcc, vcct, ovt)


def _nsa_main_kernel(q_ref, sel_ref, ksa_ref, vst_ref, kw_ref, vwt_ref, ocmp_ref, gate_ref, o_ref,
                     qa_scr, m_scr, acc_scr, win_scr, sa_scr, sb_scr, *, tq, tk):
    g = pl.program_id(1)
    qi = pl.program_id(2)
    t0 = qi * tq
    cols = NSA_HPG * tq
    bf = jnp.bfloat16
    selb = sel_ref[...]
    for h in range(NSA_HPG):
        qa_scr[h * tq:(h + 1) * tq, 0:LANES] = selb
        qa_scr[h * tq:(h + 1) * tq, LANES:2 * LANES] = q_ref[:, h * LANES:(h + 1) * LANES]
    m_scr[...] = jnp.full((1, cols), -1e37, jnp.float32)
    acc_scr[...] = jnp.zeros(acc_scr.shape, jnp.float32)

    def with_ones(vt):
        return jnp.concatenate([vt, jnp.ones((DEN_ROWS, vt.shape[1]), bf)], axis=0)

    def scores(tile, dst):
        start = pl.multiple_of(tile * tk, tk)
        dst[...] = lax.dot_general(ksa_ref[pl.ds(start, tk), :], qa_scr[...], _NT,
                                   preferred_element_type=jnp.float32)

    def consume(tile, src):
        start = pl.multiple_of(tile * tk, tk)
        vt = with_ones(vst_ref[:, pl.ds(start, tk)])
        m_prev = m_scr[...]
        m_new = jnp.maximum(m_prev, jnp.max(src[...], 0, keepdims=True))
        alpha = jnp.exp2(m_prev - m_new)
        p = jnp.exp2(src[...] - m_new)
        acc_scr[...] = alpha * acc_scr[...] + jnp.dot(vt, p.astype(bf), preferred_element_type=jnp.float32)
        m_scr[...] = m_new

    last = t0 // tk
    scores(0, sa_scr)

    wk = WIN + tq
    start = pl.multiple_of(jnp.maximum(t0 - WIN, 0), tq)
    kwin = kw_ref[pl.ds(start, wk), :]
    sw = lax.dot_general(kwin, qa_scr[:, LANES:2 * LANES], _NT, preferred_element_type=jnp.float32)
    dist = (lax.broadcasted_iota(jnp.int32, (wk, tq), 1) + t0) - (lax.broadcasted_iota(jnp.int32, (wk, tq), 0) + start)
    wb = jnp.where((dist >= 0) & (dist < WIN), 0.0, NEG)
    sw = sw + jnp.concatenate([wb] * NSA_HPG, axis=1)
    pw = jnp.exp2(sw - jnp.max(sw, 0, keepdims=True))
    ow = jnp.dot(with_ones(vwt_ref[:, pl.ds(start, wk)]), pw.astype(bf), preferred_element_type=jnp.float32)
    win_scr[...] = ow[:HEAD_DIM] * (1.0 / ow[HEAD_DIM:HEAD_DIM + 1])

    def pair_body(jj, carry):
        scores(2 * jj + 1, sb_scr)
        consume(2 * jj, sa_scr)
        scores(2 * jj + 2, sa_scr)
        consume(2 * jj + 1, sb_scr)
        return carry

    lax.fori_loop(0, last // 2, pair_body, 0)

    kk = lax.broadcasted_iota(jnp.int32, (tq, tq), 0)
    qq = lax.broadcasted_iota(jnp.int32, (tq, tq), 1)
    cb = jnp.where(kk <= qq, 0.0, NEG)
    cb = jnp.concatenate([cb] * NSA_HPG, axis=1)
    diag = pl.multiple_of(t0 - last * tk, tq)

    def finish(src):
        src[pl.ds(diag, tq), :] = src[pl.ds(diag, tq), :] + cb
        consume(last, src)

    @pl.when(last % 2 == 1)
    def _():
        scores(last, sb_scr)
        consume(last - 1, sa_scr)
        finish(sb_scr)

    @pl.when(last % 2 == 0)
    def _():
        finish(sa_scr)

    o_slc = acc_scr[0:HEAD_DIM, :] * (1.0 / acc_scr[HEAD_DIM:HEAD_DIM + 1, :])
    o_win = win_scr[...]

    base = g * (NSA_HPG * 3)
    heads = []
    for h in range(NSA_HPG):
        c = slice(h * tq, (h + 1) * tq)
        g_cmp = gate_ref[pl.ds(base + 3 * h, 1), :]
        g_slc = gate_ref[pl.ds(base + 3 * h + 1, 1), :]
        g_win = gate_ref[pl.ds(base + 3 * h + 2, 1), :]
        heads.append(g_cmp * ocmp_ref[h * HEAD_DIM:(h + 1) * HEAD_DIM, :] + g_slc * o_slc[:, c] + g_win * o_win[:, c])
    o_ref[...] = jnp.concatenate(heads, axis=0).T.astype(o_ref.dtype)


def _nsa_main(qbx, selb, ksa, vst, kw, vwt, o_cmp_t, gates_t, tq, tk):
    B, S, _ = qbx.shape
    gw = NSA_HPG * LANES
    ow = NSA_HPG * HEAD_DIM
    cols = NSA_HPG * tq
    seq = lambda w: pl.BlockSpec((None, S, w), lambda b, g, i: (b, 0, 0))
    seqt = pl.BlockSpec((None, HEAD_DIM, S), lambda b, g, i: (b, g, 0))
    return pl.pallas_call(
        functools.partial(_nsa_main_kernel, tq=tq, tk=tk),
        grid=(B, NSA_GROUPS, S // tq),
        in_specs=[pl.BlockSpec((None, tq, gw), lambda b, g, i: (b, i, g)),
                  pl.BlockSpec((None, None, tq, LANES), lambda b, g, i: (b, g, i, 0)),
                  seq(2 * LANES), seqt, seq(WG), seqt,
                  pl.BlockSpec((None, ow, tq), lambda b, g, i: (b, g, i)),
                  pl.BlockSpec((None, LANES, tq), lambda b, g, i: (b, 0, i))],
        out_specs=pl.BlockSpec((None, tq, ow), lambda b, g, i: (b, i, g)),
        out_shape=jax.ShapeDtypeStruct((B, S, WB), jnp.bfloat16),
        scratch_shapes=[pltpu.VMEM((cols, 2 * LANES), jnp.bfloat16),
                        pltpu.VMEM((1, cols), jnp.float32),
                        pltpu.VMEM((HEAD_DIM + DEN_ROWS, cols), jnp.float32),
                        pltpu.VMEM((HEAD_DIM, cols), jnp.float32),
                        pltpu.VMEM((tk, cols), jnp.float32),
                        pltpu.VMEM((tk, cols), jnp.float32)],
        compiler_params=_cparams(("parallel", "parallel", "arbitrary")),
        name="nsa_main",
    )(qbx, selb, ksa, vst, kw, vwt, o_cmp_t, gates_t)


def _outproj_kernel(x_ref, o1_ref, l1_ref, o4_ref, l4_ref, o16_ref, l16_ref, ob_ref, mod_ref, wo_ref, g_ref, b_ref,
                    y_ref, oa_scr, s0_scr, s1_scr, s2_scr, s3_scr, *, tm):
    def token_order(src, cols, scr):
        dil = src.shape[0]
        for r in range(dil):
            scr[pl.ds(r, tm // dil, stride=dil), :] = src[r, :, cols]
        return scr[...]

    for c in range(WA // LANES):
        cols = slice(c * LANES, (c + 1) * LANES)
        o1, l1 = o1_ref[:, cols], l1_ref[:, cols]
        o2, l2 = token_order(o4_ref, cols, s0_scr), token_order(l4_ref, cols, s1_scr)
        o3, l3 = token_order(o16_ref, cols, s2_scr), token_order(l16_ref, cols, s3_scr)
        m = jnp.maximum(jnp.maximum(l1, l2), l3)
        e1, e2, e3 = jnp.exp2(l1 - m), jnp.exp2(l2 - m), jnp.exp2(l3 - m)
        o_a = (e1 * o1 + e2 * o2 + e3 * o3) / (e1 + e2 + e3)
        oa_scr[:, cols] = o_a.astype(jnp.bfloat16)
    y = (jnp.dot(oa_scr[...], wo_ref[:WA, :], preferred_element_type=jnp.float32)
         + jnp.dot(ob_ref[...], wo_ref[WA:, :], preferred_element_type=jnp.float32))
    z = ALPHA * x_ref[...] + mod_ref[2:3, :] * y
    y_ref[...] = _ln(z) * g_ref[...] + b_ref[...]


def _outproj(x, pats, o_b, mod6, wo_bf16, ln_g, ln_b, tm):
    B, S, D = x.shape
    bs = lambda w: pl.BlockSpec((None, tm, w), lambda b, i: (b, i, 0))
    res = lambda d: pl.BlockSpec((None, d, tm // d, WA), lambda b, i: (b, 0, i, 0))
    vec = pl.BlockSpec((1, D), lambda b, i: (0, 0))
    (o1, l1), (o4, l4), (o16, l16) = pats
    scr = pltpu.VMEM((tm, LANES), jnp.float32)
    return pl.pallas_call(
        functools.partial(_outproj_kernel, tm=tm),
        grid=(B, S // tm),
        in_specs=[bs(D), bs(WA), bs(WA), res(4), res(4), res(16), res(16), bs(WB),
                  pl.BlockSpec((None, 6, D), lambda b, i: (b, 0, 0)),
                  pl.BlockSpec((WA + WB, D), lambda b, i: (0, 0)), vec, vec],
        out_specs=bs(D),
        out_shape=jax.ShapeDtypeStruct((B, S, D), jnp.float32),
        scratch_shapes=[pltpu.VMEM((tm, WA), jnp.bfloat16), scr, scr, scr, scr],
        compiler_params=_cparams(("parallel", "parallel")),
        name="outproj",
    )(x, o1.reshape(B, S, WA), l1.reshape(B, S, WA), o4, l4, o16, l16, o_b, mod6, wo_bf16,
      ln_g.reshape(1, D), ln_b.reshape(1, D))


def _ffn_kernel(x_ref, mod_ref, wup_ref, cw_ref, cb_ref, wdn_ref, g_ref, b_ref, y_ref,
                u_scr, buf_scr, carry_scr, acc_scr, *, tm, fc):
    i = pl.program_id(1)
    F = wdn_ref.shape[0]
    pad = 8

    @pl.when(i == 0)
    def _():
        carry_scr[...] = jnp.zeros_like(carry_scr)

    x = x_ref[...]
    u_scr[...] = (_ln(x) * (1.0 + mod_ref[4:5, :]) + mod_ref[3:4, :]).astype(jnp.bfloat16)
    def up(c):
        return (jnp.dot(u_scr[...], wup_ref[:, c * fc:(c + 1) * fc], preferred_element_type=jnp.float32),
                jnp.dot(u_scr[...], wup_ref[:, F + c * fc:F + (c + 1) * fc], preferred_element_type=jnp.float32))

    n_chunk = F // fc
    nxt = up(0)
    for c in range(n_chunk):
        cols = slice(c * fc, (c + 1) * fc)
        a_gate, a_val = nxt
        if c + 1 < n_chunk:
            nxt = up(c + 1)
        buf_scr[0:pad, :] = carry_scr[:, cols]
        buf_scr[pad:pad + tm, :] = a_gate
        carry_scr[:, cols] = a_gate[tm - pad:tm, :]
        conv = (cw_ref[0:1, cols] * buf_scr[pad - 2:pad - 2 + tm, :]
                + cw_ref[1:2, cols] * buf_scr[pad - 1:pad - 1 + tm, :]
                + cw_ref[2:3, cols] * a_gate + cb_ref[:, cols])
        h = (_gelu_tanh(conv) * a_val).astype(jnp.bfloat16)
        d = jnp.dot(h, wdn_ref[cols, :], preferred_element_type=jnp.float32)
        if c == 0:
            acc_scr[...] = d
        else:
            acc_scr[...] += d
    z = ALPHA * x + mod_ref[5:6, :] * acc_scr[...]
    y_ref[...] = _ln(z) * g_ref[...] + b_ref[...]


def _ffn(x, mod6, wup_bf16, conv_w, conv_b, wdn_bf16, ln_g, ln_b, tm, fc):
    B, S, D = x.shape
    F = wdn_bf16.shape[0]
    bs = pl.BlockSpec((None, tm, D), lambda b, i: (b, i, 0))
    full = lambda a: pl.BlockSpec(a.shape, lambda b, i: (0,) * a.ndim)
    args = (wup_bf16, conv_w, conv_b.reshape(1, F), wdn_bf16, ln_g.reshape(1, D), ln_b.reshape(1, D))
    return pl.pallas_call(
        functools.partial(_ffn_kernel, tm=tm, fc=fc),
        grid=(B, S // tm),
        in_specs=[bs, pl.BlockSpec((None, 6, D), lambda b, i: (b, 0, 0))] + [full(a) for a in args],
        out_specs=bs,
        out_shape=jax.ShapeDtypeStruct((B, S, D), jnp.float32),
        scratch_shapes=[pltpu.VMEM((tm, D), jnp.bfloat16),
                        pltpu.VMEM((tm + 8, fc), jnp.float32),
                        pltpu.VMEM((8, F), jnp.float32),
                        pltpu.VMEM((tm, D), jnp.float32)],
        compiler_params=_cparams(("arbitrary", "arbitrary")),
        name="ffn",
    )(x, mod6, *args)


def kernel(x, c, w_ada, b_ada, w_in, pe_cmp, w_ck1, w_ck2, w_cv1, w_cv2, w_o, ln1_g, ln1_b, w_up, conv_w, conv_b,
           w_down, ln2_g, ln2_b):
    B, S, D = x.shape
    tables = _rope_tables(S)
    tm = min(512, S)
    tq = 128
    tk = min(512, S)
    for l in range(DEPTH):
        mod6 = _ada(c, w_ada[l], b_ada[l]).reshape(B, 6, D)
        w_nat, w_tr = _prep_w_in(w_in[l])
        (qa, ka, va, qbx, kc, vc, ksa, kw, vst, vwt, gates_t,
         q4, k4, v4, q16, k16, v16) = _inproj(x, mod6, w_nat, w_tr, tables, tm)
        pats = [_dilated(qa[:, None], ka[:, None], va[:, None], "dilated1"),
                _dilated(q4, k4, v4, "dilated4"), _dilated(q16, k16, v16, "dilated16")]
        kcc, vcct = _compress(kc, vc, pe_cmp[l], w_ck1[l], w_ck2[l], w_cv1[l], w_cv2[l])
        o_cmp_t, selb = _nsa_cmp(qbx, kcc, vcct, 2 * tq)
        o_b = _nsa_main(qbx, selb, ksa, vst, kw, vwt, o_cmp_t, gates_t, 2 * tq, tk)
        x = _outproj(x, pats, o_b, mod6, w_o[l].astype(jnp.bfloat16), ln1_g[l], ln1_b[l], tm)
        x = _ffn(x, mod6, w_up[l].astype(jnp.bfloat16), conv_w[l], conv_b[l], w_down[l].astype(jnp.bfloat16),
                 ln2_g[l], ln2_b[l], tm // 2, 256)
    return x
```

```python
import functools
import math

import numpy as np
import jax
import jax.numpy as jnp
from jax import lax
from jax.experimental import pallas as pl
from jax.experimental.pallas import tpu as pltpu

HEAD_DIM = 64
N_HEADS_A = 8
N_HEADS_B = 8
DIL_PATTERNS = ((128, 1), (512, 4), (2048, 16))
BAND_BLOCK = 128
ROT_DIM = HEAD_DIM // 4
ROPE_THETA = 500000.0
NSA_GROUPS = 2
NSA_HPG = N_HEADS_B // NSA_GROUPS
CMP_LEN = 32
CMP_STRIDE = 16
CMP_HIDDEN = 4 * HEAD_DIM
SLC_BLOCK = 64
SLC_TOPK = 16
WIN = 512
D_FF = 2816
CONV_W = 3
DEPTH = 1
ALPHA = (2 * DEPTH) ** 0.25
LN_EPS = 1e-5
NEG = -1e30
LOG2E = math.log2(math.e)

LANES = 128
MXU_N = 256
DEN_ROWS = 16
WA = N_HEADS_A * HEAD_DIM
WB = N_HEADS_B * HEAD_DIM
WG = NSA_GROUPS * HEAD_DIM
N_GATES = N_HEADS_B * 3
SRC_VS = 3 * WA + WB + 3 * WG
SRC_KW = SRC_VS + WG
SRC_VW = SRC_KW + WG
SRC_GL = SRC_VW + WG
OFF_QA, OFF_KA, OFF_VA, OFF_QB = 0, WA, 2 * WA, 3 * WA
OFF_KC = OFF_QB + WB
OFF_VC = OFF_KC + WG
OFF_KS = OFF_VC + WG
OFF_KW = OFF_KS + WG
NAT_WIDTH = OFF_KW + WG
TR_VS, TR_VW, TR_GL = 0, WG, 2 * WG
TR_WIDTH = 3 * WG

VMEM_LIMIT = 56 * 1024 * 1024

_NT = (((1,), (1,)), ((), ()))


def _cparams(sem):
    return pltpu.CompilerParams(dimension_semantics=sem, vmem_limit_bytes=VMEM_LIMIT)


def _gelu_tanh(x):
    return 0.5 * x * (1.0 + jnp.tanh(math.sqrt(2.0 / math.pi) * (x + 0.044715 * (x * x * x))))


def _ln(x):
    mu = jnp.mean(x, -1, keepdims=True)
    xc = x - mu
    var = jnp.mean(xc * xc, -1, keepdims=True)
    return xc * lax.rsqrt(var + LN_EPS)


def _ada_kernel(c_ref, w_ref, b_ref, o_ref):
    c = c_ref[...]
    a = c * jax.nn.sigmoid(c)
    o_ref[...] = jnp.dot(a, w_ref[...], preferred_element_type=jnp.float32) + b_ref[...]


def _ada(c, w, b):
    B, D = c.shape
    N = w.shape[1]
    tn = D
    return pl.pallas_call(
        _ada_kernel,
        grid=(N // tn,),
        in_specs=[pl.BlockSpec((B, D), lambda j: (0, 0)),
                  pl.BlockSpec((D, tn), lambda j: (0, j)),
                  pl.BlockSpec((1, tn), lambda j: (0, j))],
        out_specs=pl.BlockSpec((B, tn), lambda j: (0, j)),
        out_shape=jax.ShapeDtypeStruct((B, N), jnp.float32),
        compiler_params=_cparams(("arbitrary",)),
        name="adaln",
    )(c, w, b.reshape(1, N))


def _rope_tile(t, cos, s_lo, s_hi):
    half = ROT_DIM // 2
    return t * cos + pltpu.roll(t, half, 1) * s_hi + pltpu.roll(t, LANES - half, 1) * s_lo


def _inproj_kernel(x_ref, mod_ref, w_ref, wt_ref, cos_ref, slo_ref, shi_ref,
                   qa_ref, ka_ref, va_ref, qbx_ref, kc_ref, vc_ref, ksa_ref, kw_ref, vst_ref, vwt_ref, gate_ref,
                   q4_ref, k4_ref, v4_ref, q16_ref, k16_ref, v16_ref,
                   u_ref, ra_ref, rb_ref, *, tm):
    i = pl.program_id(1)
    x = x_ref[...]
    u = _ln(x) * (1.0 + mod_ref[1:2, :]) + mod_ref[0:1, :]
    u_ref[...] = u.astype(jnp.bfloat16)
    cos, slo, shi = cos_ref[...], slo_ref[...], shi_ref[...]
    lane = lax.broadcasted_iota(jnp.int32, (1, LANES), 1)
    bf = jnp.bfloat16

    def proj(off):
        r = jnp.dot(u_ref[...], w_ref[:, off:off + MXU_N], preferred_element_type=jnp.float32)
        return r[:, :LANES], r[:, LANES:]

    def rope(t):
        return _rope_tile(t, cos, slo, shi)

    def emit(t, cols, nat_ref, r4_ref, r16_ref):
        nat_ref[:, cols] = t.astype(bf)
        ra_ref[...] = t
        n4 = tm // 4
        for r in range(4):
            part = ra_ref[pl.ds(r, n4, stride=4), :]
            r4_ref[r, :, cols] = part.astype(bf)
            rb_ref[r * n4:(r + 1) * n4, :] = part
        for r in range(4):
            for m in range(4):
                part = rb_ref[pl.ds(r * n4 + m, n4 // 4, stride=4), :]
                r16_ref[r + 4 * m, :, cols] = part.astype(bf)

    for j in range(WA // MXU_N):
        for k, t in enumerate(proj(OFF_QA + j * MXU_N)):
            emit(rope(t) * LOG2E, slice((2 * j + k) * LANES, (2 * j + k + 1) * LANES), qa_ref, q4_ref, q16_ref)
        for k, t in enumerate(proj(OFF_KA + j * MXU_N)):
            emit(rope(t), slice((2 * j + k) * LANES, (2 * j + k + 1) * LANES), ka_ref, k4_ref, k16_ref)
        for k, t in enumerate(proj(OFF_VA + j * MXU_N)):
            emit(t, slice((2 * j + k) * LANES, (2 * j + k + 1) * LANES), va_ref, v4_ref, v16_ref)
    lo = lane < HEAD_DIM
    for j in range(WB // MXU_N):
        for k, t in enumerate(proj(OFF_QB + j * MXU_N)):
            pair = 2 * j + k
            t = rope(t) * LOG2E
            g = (2 * pair) // NSA_HPG
            tr = pltpu.roll(t, HEAD_DIM, 1)
            in_g = lo if g == 0 else jnp.logical_not(lo)
            even = jnp.where(in_g, t if g == 0 else tr, 0.0)
            odd = jnp.where(in_g, tr if g == 0 else t, 0.0)
            qbx_ref[:, (2 * pair) * LANES:(2 * pair + 1) * LANES] = even.astype(bf)
            qbx_ref[:, (2 * pair + 1) * LANES:(2 * pair + 2) * LANES] = odd.astype(bf)
    kc, vc = proj(OFF_KC)
    kc_ref[...] = rope(kc)
    vc_ref[...] = vc
    ks, kw = proj(OFF_KS)
    tpos = i * tm + lax.broadcasted_iota(jnp.int32, (tm, LANES), 0)
    blk = lax.broadcasted_iota(jnp.int32, (tm, LANES), 1)
    ksa_ref[:, 0:LANES] = jnp.where((tpos // SLC_BLOCK) == blk, 1.0, 0.0).astype(bf)
    ksa_ref[:, LANES:2 * LANES] = rope(ks).astype(bf)
    kw_ref[...] = rope(kw).astype(bf)
    tr_out = lax.dot_general(wt_ref[...], u_ref[...], _NT, preferred_element_type=jnp.float32)
    vst_ref[...] = tr_out[TR_VS:TR_VS + WG].astype(bf)
    vwt_ref[...] = tr_out[TR_VW:TR_VW + WG].astype(bf)
    gate_ref[...] = jax.nn.sigmoid(tr_out[TR_GL:TR_GL + LANES])


def _rope_tables(S):
    inv = 1.0 / (ROPE_THETA ** (jnp.arange(0, ROT_DIM, 2, dtype=jnp.float32) / ROT_DIM))
    ang = jnp.arange(S, dtype=jnp.float32)[:, None] * inv[None, :]
    cos, sin = jnp.cos(ang), jnp.sin(ang)
    half = ROT_DIM // 2
    d = np.arange(LANES) % HEAD_DIM
    idx = jnp.asarray(d % half)
    rot = jnp.asarray(d < ROT_DIM)
    lo = jnp.asarray(d < half)
    hi = jnp.asarray((d >= half) & (d < ROT_DIM))
    c_t = jnp.where(rot[None], cos[:, idx], 1.0)
    s_lo = jnp.where(lo[None], -sin[:, idx], 0.0)
    s_hi = jnp.where(hi[None], sin[:, idx], 0.0)
    return c_t, s_lo, s_hi


def _prep_w_in(w):
    q_scale = HEAD_DIM ** -0.5
    w_nat = jnp.concatenate([w[:, 0:WA] * q_scale, w[:, WA:3 * WA], w[:, 3 * WA:3 * WA + WB] * q_scale,
                             w[:, 3 * WA + WB:SRC_VS], w[:, SRC_KW:SRC_VW]], axis=1)
    w_tr = jnp.concatenate([w[:, SRC_VS:SRC_KW], w[:, SRC_VW:SRC_GL],
                            jnp.pad(w[:, SRC_GL:], ((0, 0), (0, LANES - N_GATES)))], axis=1)
    return w_nat.astype(jnp.bfloat16), w_tr.T.astype(jnp.bfloat16)


def _inproj(x, mod6, w_nat, w_tr, tables, tm):
    B, S, D = x.shape
    cos, slo, shi = tables
    bs = lambda w: pl.BlockSpec((None, tm, w), lambda b, i: (b, i, 0))
    bst = pl.BlockSpec((None, LANES, tm), lambda b, i: (b, 0, i))
    tab = pl.BlockSpec((tm, LANES), lambda b, i: (i, 0))
    sd = lambda w, dt: jax.ShapeDtypeStruct((B, S, w), dt)
    sdt = lambda dt: jax.ShapeDtypeStruct((B, LANES, S), dt)
    bf, f32 = jnp.bfloat16, jnp.float32
    res = lambda d: pl.BlockSpec((None, d, tm // d, WA), lambda b, i: (b, 0, i, 0))
    res1 = pl.BlockSpec((None, None, tm, WA), lambda b, i: (b, 0, i, 0))
    sdr = lambda d: jax.ShapeDtypeStruct((B, d, S // d, WA), bf)
    return pl.pallas_call(
        functools.partial(_inproj_kernel, tm=tm),
        grid=(B, S // tm),
        in_specs=[bs(D),
                  pl.BlockSpec((None, 6, D), lambda b, i: (b, 0, 0)),
                  pl.BlockSpec((D, NAT_WIDTH), lambda b, i: (0, 0)),
                  pl.BlockSpec((TR_WIDTH, D), lambda b, i: (0, 0)),
                  tab, tab, tab],
        out_specs=[res1, res1, res1, bs(2 * WB), bs(WG), bs(WG), bs(2 * LANES), bs(WG), bst, bst, bst]
                  + [res(4)] * 3 + [res(16)] * 3,
        out_shape=[sdr(1), sdr(1), sdr(1), sd(2 * WB, bf), sd(WG, f32), sd(WG, f32),
                   sd(2 * LANES, bf), sd(WG, bf), sdt(bf), sdt(bf), sdt(f32)]
                  + [sdr(4)] * 3 + [sdr(16)] * 3,
        scratch_shapes=[pltpu.VMEM((tm, D), jnp.bfloat16),
                        pltpu.VMEM((tm, LANES), jnp.float32),
                        pltpu.VMEM((tm, LANES), jnp.float32)],
        compiler_params=_cparams(("parallel", "parallel")),
        name="inproj",
    )(x, mod6, w_nat, w_tr, cos, slo, shi)


def _dil_kernel(q_ref, kc_ref, kp_ref, vc_ref, vp_ref, o_ref, lse_ref, *, nblk):
    i = pl.program_id(2)
    blk_rows = BAND_BLOCK
    qi = lax.broadcasted_iota(jnp.int32, (blk_rows, 2 * blk_rows), 0)
    kj = lax.broadcasted_iota(jnp.int32, (blk_rows, 2 * blk_rows), 1)
    in_prev = kj < blk_rows
    bias = jnp.where(in_prev, jnp.where(kj >= qi, 0.0, NEG), jnp.where(kj - blk_rows <= qi, 0.0, NEG))
    bias0 = jnp.where(in_prev, jnp.where(i == 0, NEG, bias), bias)
    lane = lax.broadcasted_iota(jnp.int32, (1, LANES), 1)
    lo = lane < HEAD_DIM
    n_pair = WA // LANES
    for blk in range(nblk):
        rows = slice(blk * blk_rows, (blk + 1) * blk_rows)
        both = slice((blk - 1) * blk_rows, (blk + 1) * blk_rows)
        b = bias0 if blk == 0 else bias
        vals, ss = [], []
        for hp in range(n_pair):
            cols = slice(hp * LANES, (hp + 1) * LANES)
            qt = q_ref[rows, cols]
            if blk == 0:
                k2 = jnp.concatenate([kp_ref[:, cols], kc_ref[rows, cols]], axis=0)
                vals.append(jnp.concatenate([vp_ref[:, cols], vc_ref[rows, cols]], axis=0))
            else:
                k2 = kc_ref[both, cols]
                vals.append(vc_ref[both, cols])
            for h in range(2):
                qh = jnp.where(lo if h == 0 else jnp.logical_not(lo), qt, jnp.zeros_like(qt))
                ss.append(lax.dot_general(qh, k2, _NT, preferred_element_type=jnp.float32) + b)
        ms = [jnp.max(s, -1, keepdims=True) for s in ss]
        ps = [jnp.exp2(s - m) for s, m in zip(ss, ms)]
        dens = [jnp.sum(p, -1, keepdims=True) for p in ps]
        for hp in range(n_pair):
            cols = slice(hp * LANES, (hp + 1) * LANES)
            outs, lses = [], []
            for h in range(2):
                n = 2 * hp + h
                o = jnp.dot(ps[n].astype(jnp.bfloat16), vals[hp], preferred_element_type=jnp.float32)
                outs.append(o * (1.0 / dens[n]))
                lses.append(ms[n] + jnp.log2(dens[n]))
            o_ref[rows, cols] = jnp.where(lo, outs[0], outs[1])
            lse_ref[rows, cols] = jnp.where(lo, lses[0], lses[1])


def _dilated(q, k, v, name):
    B, dil, L, W = q.shape
    rows = min(L, 4 * BAND_BLOCK)
    nblk = rows // BAND_BLOCK
    cur = pl.BlockSpec((None, None, rows, W), lambda b, r, i: (b, r, i, 0))
    prev = pl.BlockSpec((None, None, BAND_BLOCK, W), lambda b, r, i: (b, r, jnp.maximum(i * nblk - 1, 0), 0))
    return pl.pallas_call(
        functools.partial(_dil_kernel, nblk=nblk),
        grid=(B, dil, L // rows),
        in_specs=[cur, cur, prev, cur, prev],
        out_specs=[cur, cur],
        out_shape=[jax.ShapeDtypeStruct((B, dil, L, W), jnp.float32)] * 2,
        compiler_params=_cparams(("parallel", "parallel", "arbitrary")),
        name=name,
    )(q, k, k, v, v)


def _cmp_kernel(ck_ref, cv_ref, pe_ref, wk1_ref, wk2_ref, wv1_ref, wv2t_ref, ok_ref, ovt_ref):
    half = pe_ref.shape[1] // 2
    n = ck_ref.shape[0]

    def hidden(c_ref, w1_ref):
        c = c_ref[...]
        a = jnp.dot((c + pe_ref[:, :half]).astype(jnp.bfloat16), w1_ref[:half, :],
                    preferred_element_type=jnp.float32)
        b = jnp.dot((c + pe_ref[:, half:]).astype(jnp.bfloat16), w1_ref[half:, :],
                    preferred_element_type=jnp.float32)
        return _gelu_tanh(a + pltpu.roll(b, n - 1, 0)).astype(jnp.bfloat16)

    ok_ref[...] = jnp.dot(hidden(ck_ref, wk1_ref), wk2_ref[...],
                          preferred_element_type=jnp.float32).astype(ok_ref.dtype)
    ovt_ref[...] = lax.dot_general(wv2t_ref[...], hidden(cv_ref, wv1_ref), _NT,
                                   preferred_element_type=jnp.float32).astype(ovt_ref.dtype)


def _compress(kc, vc, pe, w_ck1, w_ck2, w_cv1, w_cv2):
    B, S, _ = kc.shape
    n = S // CMP_STRIDE
    cw = CMP_STRIDE * HEAD_DIM

    def chunks(t):
        t = t.reshape(B, n, CMP_STRIDE, NSA_GROUPS, HEAD_DIM).transpose(0, 3, 1, 2, 4)
        return t.reshape(B, NSA_GROUPS, n, cw)

    bf = jnp.bfloat16
    cspec = pl.BlockSpec((None, None, n, cw), lambda b, g: (b, g, 0, 0))
    full = lambda a: pl.BlockSpec(a.shape, lambda b, g: (0,) * a.ndim)
    args = (chunks(kc), chunks(vc), pe.reshape(1, CMP_LEN * HEAD_DIM),
            w_ck1.astype(bf), jnp.concatenate([w_ck2, w_ck2], axis=-1).astype(bf),
            w_cv1.astype(bf), w_cv2.T.astype(bf))
    return pl.pallas_call(
        _cmp_kernel,
        grid=(B, NSA_GROUPS),
        in_specs=[cspec, cspec] + [full(a) for a in args[2:]],
        out_specs=[pl.BlockSpec((None, None, n, LANES), lambda b, g: (b, g, 0, 0)),
                   pl.BlockSpec((None, None, HEAD_DIM, n), lambda b, g: (b, g, 0, 0))],
        out_shape=[jax.ShapeDtypeStruct((B, NSA_GROUPS, n, LANES), bf),
                   jax.ShapeDtypeStruct((B, NSA_GROUPS, HEAD_DIM, n), bf)],
        compiler_params=_cparams(("parallel", "parallel")),
        name="compress",
    )(*args)


def _nsa_cmp_kernel(q_ref, kcc_ref, vcct_ref, ovt_ref, o_ref, sel_ref, *, tq, nsel, topk):
    qi = pl.program_id(2)
    t0 = qi * tq
    nc = kcc_ref.shape[0]
    tpos = t0 + lax.broadcasted_iota(jnp.int32, (nc, tq), 1)
    cend = lax.broadcasted_iota(jnp.int32, (nc, tq), 0) * CMP_STRIDE + (CMP_LEN - 1)
    cbias = jnp.where(cend <= tpos, 0.0, NEG)
    any_vis = (t0 + lax.broadcasted_iota(jnp.int32, (1, tq), 1)) >= CMP_LEN - 1
    kcc, vcct = kcc_ref[...], vcct_ref[...]
    ss = [lax.dot_general(kcc, q_ref[:, h * LANES:(h + 1) * LANES], _NT, preferred_element_type=jnp.float32) + cbias
          for h in range(NSA_HPG)]
    ms = [jnp.max(s, 0, keepdims=True) for s in ss]
    ps = [jnp.exp2(s - m).astype(jnp.bfloat16) for s, m in zip(ss, ms)]
    vaug = jnp.concatenate([vcct, jnp.ones((DEN_ROWS, nc), jnp.bfloat16)], axis=0)
    imp = jnp.zeros((LANES, tq), jnp.float32)
    for h in range(NSA_HPG):
        oa = jnp.dot(vaug, ps[h], preferred_element_type=jnp.float32)
        inv = jnp.where(any_vis, 1.0 / oa[HEAD_DIM:HEAD_DIM + 1], 0.0)
        o_ref[h * HEAD_DIM:(h + 1) * HEAD_DIM, :] = oa[:HEAD_DIM] * inv
        imp = imp + jnp.dot(ovt_ref[...], ps[h], preferred_element_type=jnp.float32) * inv
    blk = lax.broadcasted_iota(jnp.int32, (LANES, tq), 0)
    tq_pos = t0 + lax.broadcasted_iota(jnp.int32, (LANES, tq), 1)
    cur = tq_pos // SLC_BLOCK
    forced = (blk == cur) | (blk == cur - 1) | (blk == 0)
    valid = blk <= cur
    score = jnp.where(valid & jnp.logical_not(forced) & (blk < nsel), imp, -jnp.inf)
    blk_f = blk.astype(jnp.float32)
    sel = jnp.where(forced, 1.0, 0.0)
    for _ in range(topk - 3):
        mx = jnp.max(score, 0, keepdims=True)
        first = jnp.min(jnp.where(score == mx, blk_f, float(2 * LANES)), 0, keepdims=True)
        hit = blk_f == first
        sel = jnp.where(hit, 1.0, sel)
        score = jnp.where(hit, -jnp.inf, score)
    sel = jnp.where(valid, sel, 0.0)
    sel_ref[...] = ((sel.T - 1.0) * (-NEG)).astype(jnp.bfloat16)


def _overlap_t(S):
    nc = S // CMP_STRIDE
    ns = S // SLC_BLOCK
    cs = np.arange(nc) * CMP_STRIDE
    ss = np.arange(LANES) * SLC_BLOCK
    ov = np.clip(np.minimum(cs[None, :] + CMP_LEN, ss[:, None] + SLC_BLOCK)
                 - np.maximum(cs[None, :], ss[:, None]), 0, None).astype(np.float32) / CMP_LEN
    ov[ns:, :] = 0.0
    ov[:, nc - CMP_LEN // CMP_STRIDE + 1:] = 0.0
    return jnp.asarray(ov, jnp.bfloat16)


def _nsa_cmp(qbx, kcc, vcct, tq):
    B, S, _ = qbx.shape
    nc = kcc.shape[2]
    ns = S // SLC_BLOCK
    gw = NSA_HPG * LANES
    ovt = _overlap_t(S)
    return pl.pallas_call(
        functools.partial(_nsa_cmp_kernel, tq=tq, nsel=ns, topk=min(SLC_TOPK, ns)),
        grid=(B, NSA_GROUPS, S // tq),
        in_specs=[pl.BlockSpec((None, tq, gw), lambda b, g, i: (b, i, g)),
                  pl.BlockSpec((None, None, nc, LANES), lambda b, g, i: (b, g, 0, 0)),
                  pl.BlockSpec((None, None, HEAD_DIM, nc), lambda b, g, i: (b, g, 0, 0)),
                  pl.BlockSpec((LANES, nc), lambda b, g, i: (0, 0))],
        out_specs=[pl.BlockSpec((None, NSA_HPG * HEAD_DIM, tq), lambda b, g, i: (b, g, i)),
                   pl.BlockSpec((None, None, tq, LANES), lambda b, g, i: (b, g, i, 0))],
        out_shape=[jax.ShapeDtypeStruct((B, WB, S), jnp.float32),
                   jax.ShapeDtypeStruct((B, NSA_GROUPS, S, LANES), jnp.bfloat16)],
        compiler_params=_cparams(("parallel", "parallel", "parallel")),
        name="nsa_cmp",
    )(qbx, kcc, vcct, ovt)


def _nsa_main_kernel(q_ref, sel_ref, ksa_ref, vst_ref, kw_ref, vwt_ref, ocmp_ref, gate_ref, o_ref,
                     qa_scr, m_scr, acc_scr, win_scr, sa_scr, sb_scr, *, tq, tk):
    g = pl.program_id(1)
    qi = pl.program_id(2)
    t0 = qi * tq
    cols = NSA_HPG * tq
    bf = jnp.bfloat16
    selb = sel_ref[...]
    for h in range(NSA_HPG):
        qa_scr[h * tq:(h + 1) * tq, 0:LANES] = selb
        qa_scr[h * tq:(h + 1) * tq, LANES:2 * LANES] = q_ref[:, h * LANES:(h + 1) * LANES]
    m_scr[...] = jnp.full((1, cols), -1e37, jnp.float32)
    acc_scr[...] = jnp.zeros(acc_scr.shape, jnp.float32)

    def with_ones(vt):
        return jnp.concatenate([vt, jnp.ones((DEN_ROWS, vt.shape[1]), bf)], axis=0)

    def scores(tile, dst):
        start = pl.multiple_of(tile * tk, tk)
        dst[...] = lax.dot_general(ksa_ref[pl.ds(start, tk), :], qa_scr[...], _NT,
                                   preferred_element_type=jnp.float32)

    def consume(tile, src):
        start = pl.multiple_of(tile * tk, tk)
        vt = with_ones(vst_ref[:, pl.ds(start, tk)])
        m_prev = m_scr[...]
        m_new = jnp.maximum(m_prev, jnp.max(src[...], 0, keepdims=True))
        alpha = jnp.exp2(m_prev - m_new)
        p = jnp.exp2(src[...] - m_new)
        acc_scr[...] = alpha * acc_scr[...] + jnp.dot(vt, p.astype(bf), preferred_element_type=jnp.float32)
        m_scr[...] = m_new

    last = t0 // tk
    scores(0, sa_scr)

    def window():
        wk = WIN + tq
        start = pl.multiple_of(jnp.maximum(t0 - WIN, 0), tq)
        kwin = kw_ref[pl.ds(start, wk), :]
        sw = lax.dot_general(kwin, qa_scr[:, LANES:2 * LANES], _NT, preferred_element_type=jnp.float32)
        dist = ((lax.broadcasted_iota(jnp.int32, (wk, tq), 1) + t0)
                - (lax.broadcasted_iota(jnp.int32, (wk, tq), 0) + start))
        wb = jnp.where((dist >= 0) & (dist < WIN), 0.0, NEG)
        sw = sw + jnp.concatenate([wb] * NSA_HPG, axis=1)
        pw = jnp.exp2(sw - jnp.max(sw, 0, keepdims=True))
        ow = jnp.dot(with_ones(vwt_ref[:, pl.ds(start, wk)]), pw.astype(bf), preferred_element_type=jnp.float32)
        win_scr[...] = ow[:HEAD_DIM] * (1.0 / ow[HEAD_DIM:HEAD_DIM + 1])

    def pair_body(jj, carry):
        scores(2 * jj + 1, sb_scr)
        consume(2 * jj, sa_scr)
        scores(2 * jj + 2, sa_scr)
        consume(2 * jj + 1, sb_scr)
        return carry

    lax.fori_loop(0, last // 2, pair_body, 0)

    kk = lax.broadcasted_iota(jnp.int32, (tq, tq), 0)
    qq = lax.broadcasted_iota(jnp.int32, (tq, tq), 1)
    cb = jnp.where(kk <= qq, 0.0, NEG)
    cb = jnp.concatenate([cb] * NSA_HPG, axis=1)
    diag = pl.multiple_of(t0 - last * tk, tq)

    def finish(src):
        src[pl.ds(diag, tq), :] = src[pl.ds(diag, tq), :] + cb
        window()
        consume(last, src)

    @pl.when(last % 2 == 1)
    def _():
        scores(last, sb_scr)
        consume(last - 1, sa_scr)
        finish(sb_scr)

    @pl.when(last % 2 == 0)
    def _():
        finish(sa_scr)

    o_slc = acc_scr[0:HEAD_DIM, :] * (1.0 / acc_scr[HEAD_DIM:HEAD_DIM + 1, :])
    o_win = win_scr[...]

    base = g * (NSA_HPG * 3)
    heads = []
    for h in range(NSA_HPG):
        c = slice(h * tq, (h + 1) * tq)
        g_cmp = gate_ref[pl.ds(base + 3 * h, 1), :]
        g_slc = gate_ref[pl.ds(base + 3 * h + 1, 1), :]
        g_win = gate_ref[pl.ds(base + 3 * h + 2, 1), :]
        heads.append(g_cmp * ocmp_ref[h * HEAD_DIM:(h + 1) * HEAD_DIM, :] + g_slc * o_slc[:, c] + g_win * o_win[:, c])
    o_ref[...] = jnp.concatenate(heads, axis=0).T.astype(o_ref.dtype)


def _nsa_main(qbx, selb, ksa, vst, kw, vwt, o_cmp_t, gates_t, tq, tk):
    B, S, _ = qbx.shape
    gw = NSA_HPG * LANES
    ow = NSA_HPG * HEAD_DIM
    cols = NSA_HPG * tq
    seq = lambda w: pl.BlockSpec((None, S, w), lambda b, g, i: (b, 0, 0))
    seqt = pl.BlockSpec((None, HEAD_DIM, S), lambda b, g, i: (b, g, 0))
    return pl.pallas_call(
        functools.partial(_nsa_main_kernel, tq=tq, tk=tk),
        grid=(B, NSA_GROUPS, S // tq),
        in_specs=[pl.BlockSpec((None, tq, gw), lambda b, g, i: (b, i, g)),
                  pl.BlockSpec((None, None, tq, LANES), lambda b, g, i: (b, g, i, 0)),
                  seq(2 * LANES), seqt, seq(WG), seqt,
                  pl.BlockSpec((None, ow, tq), lambda b, g, i: (b, g, i)),
                  pl.BlockSpec((None, LANES, tq), lambda b, g, i: (b, 0, i))],
        out_specs=pl.BlockSpec((None, tq, ow), lambda b, g, i: (b, i, g)),
        out_shape=jax.ShapeDtypeStruct((B, S, WB), jnp.bfloat16),
        scratch_shapes=[pltpu.VMEM((cols, 2 * LANES), jnp.bfloat16),
                        pltpu.VMEM((1, cols), jnp.float32),
                        pltpu.VMEM((HEAD_DIM + DEN_ROWS, cols), jnp.float32),
                        pltpu.VMEM((HEAD_DIM, cols), jnp.float32),
                        pltpu.VMEM((tk, cols), jnp.float32),
                        pltpu.VMEM((tk, cols), jnp.float32)],
        compiler_params=_cparams(("parallel", "parallel", "arbitrary")),
        name="nsa_main",
    )(qbx, selb, ksa, vst, kw, vwt, o_cmp_t, gates_t)


def _outproj_kernel(x_ref, o1_ref, l1_ref, o4_ref, l4_ref, o16_ref, l16_ref, ob_ref, mod_ref, wo_ref, g_ref, b_ref,
                    y_ref, oa_scr, s0_scr, s1_scr, s2_scr, s3_scr, *, tm):
    def token_order(src, cols, scr):
        dil = src.shape[0]
        for r in range(dil):
            scr[pl.ds(r, tm // dil, stride=dil), :] = src[r, :, cols]
        return scr[...]

    for c in range(WA // LANES):
        cols = slice(c * LANES, (c + 1) * LANES)
        o1, l1 = o1_ref[:, cols], l1_ref[:, cols]
        o2, l2 = token_order(o4_ref, cols, s0_scr), token_order(l4_ref, cols, s1_scr)
        o3, l3 = token_order(o16_ref, cols, s2_scr), token_order(l16_ref, cols, s3_scr)
        m = jnp.maximum(jnp.maximum(l1, l2), l3)
        e1, e2, e3 = jnp.exp2(l1 - m), jnp.exp2(l2 - m), jnp.exp2(l3 - m)
        o_a = (e1 * o1 + e2 * o2 + e3 * o3) / (e1 + e2 + e3)
        oa_scr[:, cols] = o_a.astype(jnp.bfloat16)
    y = (jnp.dot(oa_scr[...], wo_ref[:WA, :], preferred_element_type=jnp.float32)
         + jnp.dot(ob_ref[...], wo_ref[WA:, :], preferred_element_type=jnp.float32))
    z = ALPHA * x_ref[...] + mod_ref[2:3, :] * y
    y_ref[...] = _ln(z) * g_ref[...] + b_ref[...]


def _outproj(x, pats, o_b, mod6, wo_bf16, ln_g, ln_b, tm):
    B, S, D = x.shape
    bs = lambda w: pl.BlockSpec((None, tm, w), lambda b, i: (b, i, 0))
    res = lambda d: pl.BlockSpec((None, d, tm // d, WA), lambda b, i: (b, 0, i, 0))
    res1 = pl.BlockSpec((None, None, tm, WA), lambda b, i: (b, 0, i, 0))
    vec = pl.BlockSpec((1, D), lambda b, i: (0, 0))
    (o1, l1), (o4, l4), (o16, l16) = pats
    scr = pltpu.VMEM((tm, LANES), jnp.float32)
    return pl.pallas_call(
        functools.partial(_outproj_kernel, tm=tm),
        grid=(B, S // tm),
        in_specs=[bs(D), res1, res1, res(4), res(4), res(16), res(16), bs(WB),
                  pl.BlockSpec((None, 6, D), lambda b, i: (b, 0, 0)),
                  pl.BlockSpec((WA + WB, D), lambda b, i: (0, 0)), vec, vec],
        out_specs=bs(D),
        out_shape=jax.ShapeDtypeStruct((B, S, D), jnp.float32),
        scratch_shapes=[pltpu.VMEM((tm, WA), jnp.bfloat16), scr, scr, scr, scr],
        compiler_params=_cparams(("parallel", "parallel")),
        name="outproj",
    )(x, o1, l1, o4, l4, o16, l16, o_b, mod6, wo_bf16,
      ln_g.reshape(1, D), ln_b.reshape(1, D))


def _ffn_kernel(x_ref, mod_ref, wup_ref, cw_ref, cb_ref, wdn_ref, g_ref, b_ref, y_ref,
                u_scr, buf_scr, carry_scr, acc_scr, *, tm, fc):
    i = pl.program_id(1)
    F = wdn_ref.shape[0]
    pad = 8

    @pl.when(i == 0)
    def _():
        carry_scr[...] = jnp.zeros_like(carry_scr)

    x = x_ref[...]
    u_scr[...] = (_ln(x) * (1.0 + mod_ref[4:5, :]) + mod_ref[3:4, :]).astype(jnp.bfloat16)
    def up(c):
        return (jnp.dot(u_scr[...], wup_ref[:, c * fc:(c + 1) * fc], preferred_element_type=jnp.float32),
                jnp.dot(u_scr[...], wup_ref[:, F + c * fc:F + (c + 1) * fc], preferred_element_type=jnp.float32))

    n_chunk = F // fc
    nxt = up(0)
    for c in range(n_chunk):
        cols = slice(c * fc, (c + 1) * fc)
        a_gate, a_val = nxt
        if c + 1 < n_chunk:
            nxt = up(c + 1)
        buf_scr[0:pad, :] = carry_scr[:, cols]
        buf_scr[pad:pad + tm, :] = a_gate
        carry_scr[:, cols] = a_gate[tm - pad:tm, :]
        conv = (cw_ref[0:1, cols] * buf_scr[pad - 2:pad - 2 + tm, :]
                + cw_ref[1:2, cols] * buf_scr[pad - 1:pad - 1 + tm, :]
                + cw_ref[2:3, cols] * a_gate + cb_ref[:, cols])
        h = (_gelu_tanh(conv) * a_val).astype(jnp.bfloat16)
        d = jnp.dot(h, wdn_ref[cols, :], preferred_element_type=jnp.float32)
        if c == 0:
            acc_scr[...] = d
        else:
            acc_scr[...] += d
    z = ALPHA * x + mod_ref[5:6, :] * acc_scr[...]
    y_ref[...] = _ln(z) * g_ref[...] + b_ref[...]


def _ffn(x, mod6, wup_bf16, conv_w, conv_b, wdn_bf16, ln_g, ln_b, tm, fc):
    B, S, D = x.shape
    F = wdn_bf16.shape[0]
    bs = pl.BlockSpec((None, tm, D), lambda b, i: (b, i, 0))
    full = lambda a: pl.BlockSpec(a.shape, lambda b, i: (0,) * a.ndim)
    args = (wup_bf16, conv_w, conv_b.reshape(1, F), wdn_bf16, ln_g.reshape(1, D), ln_b.reshape(1, D))
    return pl.pallas_call(
        functools.partial(_ffn_kernel, tm=tm, fc=fc),
        grid=(B, S // tm),
        in_specs=[bs, pl.BlockSpec((None, 6, D), lambda b, i: (b, 0, 0))] + [full(a) for a in args],
        out_specs=bs,
        out_shape=jax.ShapeDtypeStruct((B, S, D), jnp.float32),
        scratch_shapes=[pltpu.VMEM((tm, D), jnp.bfloat16),
                        pltpu.VMEM((tm + 8, fc), jnp.float32),
                        pltpu.VMEM((8, F), jnp.float32),
                        pltpu.VMEM((tm, D), jnp.float32)],
        compiler_params=_cparams(("arbitrary", "arbitrary")),
        name="ffn",
    )(x, mod6, *args)


def kernel(x, c, w_ada, b_ada, w_in, pe_cmp, w_ck1, w_ck2, w_cv1, w_cv2, w_o, ln1_g, ln1_b, w_up, conv_w, conv_b,
           w_down, ln2_g, ln2_b):
    B, S, D = x.shape
    tables = _rope_tables(S)
    tm = min(512, S)
    tq = 128
    tk = min(512, S)
    for l in range(DEPTH):
        mod6 = _ada(c, w_ada[l], b_ada[l]).reshape(B, 6, D)
        w_nat, w_tr = _prep_w_in(w_in[l])
        (qa, ka, va, qbx, kc, vc, ksa, kw, vst, vwt, gates_t,
         q4, k4, v4, q16, k16, v16) = _inproj(x, mod6, w_nat, w_tr, tables, tm)
        pats = [_dilated(qa, ka, va, "dilated1"),
                _dilated(q4, k4, v4, "dilated4"), _dilated(q16, k16, v16, "dilated16")]
        kcc, vcct = _compress(kc, vc, pe_cmp[l], w_ck1[l], w_ck2[l], w_cv1[l], w_cv2[l])
        o_cmp_t, selb = _nsa_cmp(qbx, kcc, vcct, 2 * tq)
        o_b = _nsa_main(qbx, selb, ksa, vst, kw, vwt, o_cmp_t, gates_t, 2 * tq, tk)
        x = _outproj(x, pats, o_b, mod6, w_o[l].astype(jnp.bfloat16), ln1_g[l], ln1_b[l], tm)
        x = _ffn(x, mod6, w_up[l].astype(jnp.bfloat16), conv_w[l], conv_b[l], w_down[l].astype(jnp.bfloat16),
                 ln2_g[l], ln2_b[l], tm // 2, 256)
    return x
```

```python
import functools
import math

import numpy as np
import jax
import jax.numpy as jnp
from jax import lax
from jax.experimental import pallas as pl
from jax.experimental.pallas import tpu as pltpu

HEAD_DIM = 64
N_HEADS_A = 8
N_HEADS_B = 8
DIL_PATTERNS = ((128, 1), (512, 4), (2048, 16))
BAND_BLOCK = 128
ROT_DIM = HEAD_DIM // 4
ROPE_THETA = 500000.0
NSA_GROUPS = 2
NSA_HPG = N_HEADS_B // NSA_GROUPS
CMP_LEN = 32
CMP_STRIDE = 16
CMP_HIDDEN = 4 * HEAD_DIM
SLC_BLOCK = 64
SLC_TOPK = 16
WIN = 512
D_FF = 2816
CONV_W = 3
DEPTH = 1
ALPHA = (2 * DEPTH) ** 0.25
LN_EPS = 1e-5
NEG = -1e30
LOG2E = math.log2(math.e)

LANES = 128
MXU_N = 256
DEN_ROWS = 16
LSE_LANES = LANES // N_HEADS_A
WA = N_HEADS_A * HEAD_DIM
WB = N_HEADS_B * HEAD_DIM
WG = NSA_GROUPS * HEAD_DIM
N_GATES = N_HEADS_B * 3
SRC_VS = 3 * WA + WB + 3 * WG
SRC_KW = SRC_VS + WG
SRC_VW = SRC_KW + WG
SRC_GL = SRC_VW + WG
OFF_QA, OFF_KA, OFF_VA, OFF_QB = 0, WA, 2 * WA, 3 * WA
OFF_KC = OFF_QB + WB
OFF_VC = OFF_KC + WG
OFF_KS = OFF_VC + WG
OFF_KW = OFF_KS + WG
NAT_WIDTH = OFF_KW + WG
TR_VS, TR_VW, TR_GL = 0, WG, 2 * WG
TR_WIDTH = 3 * WG

VMEM_LIMIT = 56 * 1024 * 1024

_NT = (((1,), (1,)), ((), ()))


def _cparams(sem):
    return pltpu.CompilerParams(dimension_semantics=sem, vmem_limit_bytes=VMEM_LIMIT)


def _gelu_tanh(x):
    return 0.5 * x * (1.0 + jnp.tanh(math.sqrt(2.0 / math.pi) * (x + 0.044715 * (x * x * x))))


def _ln(x):
    mu = jnp.mean(x, -1, keepdims=True)
    xc = x - mu
    var = jnp.mean(xc * xc, -1, keepdims=True)
    return xc * lax.rsqrt(var + LN_EPS)


def _ada_kernel(c_ref, w_ref, b_ref, o_ref):
    c = c_ref[...]
    a = c * jax.nn.sigmoid(c)
    o_ref[...] = jnp.dot(a, w_ref[...], preferred_element_type=jnp.float32) + b_ref[...]


def _ada(c, w, b):
    B, D = c.shape
    N = w.shape[1]
    tn = D
    return pl.pallas_call(
        _ada_kernel,
        grid=(N // tn,),
        in_specs=[pl.BlockSpec((B, D), lambda j: (0, 0)),
                  pl.BlockSpec((D, tn), lambda j: (0, j)),
                  pl.BlockSpec((1, tn), lambda j: (0, j))],
        out_specs=pl.BlockSpec((B, tn), lambda j: (0, j)),
        out_shape=jax.ShapeDtypeStruct((B, N), jnp.float32),
        compiler_params=_cparams(("arbitrary",)),
        name="adaln",
    )(c, w, b.reshape(1, N))


def _rope_tile(t, cos, s_lo, s_hi):
    half = ROT_DIM // 2
    return t * cos + pltpu.roll(t, half, 1) * s_hi + pltpu.roll(t, LANES - half, 1) * s_lo


def _inproj_kernel(x_ref, mod_ref, w_ref, wt_ref, cos_ref, slo_ref, shi_ref,
                   qa_ref, ka_ref, va_ref, qbx_ref, kc_ref, vc_ref, ksa_ref, kw_ref, vst_ref, vwt_ref, gate_ref,
                   q4_ref, k4_ref, v4_ref, q16_ref, k16_ref, v16_ref,
                   u_ref, ra_ref, rb_ref, *, tm):
    i = pl.program_id(1)
    x = x_ref[...]
    u = _ln(x) * (1.0 + mod_ref[1:2, :]) + mod_ref[0:1, :]
    u_ref[...] = u.astype(jnp.bfloat16)
    cos, slo, shi = cos_ref[...], slo_ref[...], shi_ref[...]
    lane = lax.broadcasted_iota(jnp.int32, (1, LANES), 1)
    bf = jnp.bfloat16

    def proj(off):
        r = jnp.dot(u_ref[...], w_ref[:, off:off + MXU_N], preferred_element_type=jnp.float32)
        return r[:, :LANES], r[:, LANES:]

    def rope(t):
        return _rope_tile(t, cos, slo, shi)

    def emit(t, cols, nat_ref, r4_ref, r16_ref):
        nat_ref[:, cols] = t.astype(bf)
        ra_ref[...] = t
        n4 = tm // 4
        for r in range(4):
            part = ra_ref[pl.ds(r, n4, stride=4), :]
            r4_ref[r, :, cols] = part.astype(bf)
            rb_ref[r * n4:(r + 1) * n4, :] = part
        for r in range(4):
            for m in range(4):
                part = rb_ref[pl.ds(r * n4 + m, n4 // 4, stride=4), :]
                r16_ref[r + 4 * m, :, cols] = part.astype(bf)

    for j in range(WA // MXU_N):
        for k, t in enumerate(proj(OFF_QA + j * MXU_N)):
            emit(rope(t) * LOG2E, slice((2 * j + k) * LANES, (2 * j + k + 1) * LANES), qa_ref, q4_ref, q16_ref)
        for k, t in enumerate(proj(OFF_KA + j * MXU_N)):
            emit(rope(t), slice((2 * j + k) * LANES, (2 * j + k + 1) * LANES), ka_ref, k4_ref, k16_ref)
        for k, t in enumerate(proj(OFF_VA + j * MXU_N)):
            emit(t, slice((2 * j + k) * LANES, (2 * j + k + 1) * LANES), va_ref, v4_ref, v16_ref)
    lo = lane < HEAD_DIM
    for j in range(WB // MXU_N):
        for k, t in enumerate(proj(OFF_QB + j * MXU_N)):
            pair = 2 * j + k
            t = rope(t) * LOG2E
            g = (2 * pair) // NSA_HPG
            tr = pltpu.roll(t, HEAD_DIM, 1)
            in_g = lo if g == 0 else jnp.logical_not(lo)
            even = jnp.where(in_g, t if g == 0 else tr, 0.0)
            odd = jnp.where(in_g, tr if g == 0 else t, 0.0)
            qbx_ref[:, (2 * pair) * LANES:(2 * pair + 1) * LANES] = even.astype(bf)
            qbx_ref[:, (2 * pair + 1) * LANES:(2 * pair + 2) * LANES] = odd.astype(bf)
    kc, vc = proj(OFF_KC)
    kc_ref[...] = rope(kc)
    vc_ref[...] = vc
    ks, kw = proj(OFF_KS)
    tpos = i * tm + lax.broadcasted_iota(jnp.int32, (tm, LANES), 0)
    blk = lax.broadcasted_iota(jnp.int32, (tm, LANES), 1)
    ksa_ref[:, 0:LANES] = jnp.where((tpos // SLC_BLOCK) == blk, 1.0, 0.0).astype(bf)
    ksa_ref[:, LANES:2 * LANES] = rope(ks).astype(bf)
    kw_ref[...] = rope(kw).astype(bf)
    tr_out = lax.dot_general(wt_ref[...], u_ref[...], _NT, preferred_element_type=jnp.float32)
    vst_ref[...] = tr_out[TR_VS:TR_VS + WG].astype(bf)
    vwt_ref[...] = tr_out[TR_VW:TR_VW + WG].astype(bf)
    gate_ref[...] = jax.nn.sigmoid(tr_out[TR_GL:TR_GL + LANES])


def _rope_tables(S):
    inv = 1.0 / (ROPE_THETA ** (jnp.arange(0, ROT_DIM, 2, dtype=jnp.float32) / ROT_DIM))
    ang = jnp.arange(S, dtype=jnp.float32)[:, None] * inv[None, :]
    cos, sin = jnp.cos(ang), jnp.sin(ang)
    half = ROT_DIM // 2
    d = np.arange(LANES) % HEAD_DIM
    idx = jnp.asarray(d % half)
    rot = jnp.asarray(d < ROT_DIM)
    lo = jnp.asarray(d < half)
    hi = jnp.asarray((d >= half) & (d < ROT_DIM))
    c_t = jnp.where(rot[None], cos[:, idx], 1.0)
    s_lo = jnp.where(lo[None], -sin[:, idx], 0.0)
    s_hi = jnp.where(hi[None], sin[:, idx], 0.0)
    return c_t, s_lo, s_hi


def _prep_w_in(w):
    q_scale = HEAD_DIM ** -0.5
    w_nat = jnp.concatenate([w[:, 0:WA] * q_scale, w[:, WA:3 * WA], w[:, 3 * WA:3 * WA + WB] * q_scale,
                             w[:, 3 * WA + WB:SRC_VS], w[:, SRC_KW:SRC_VW]], axis=1)
    w_tr = jnp.concatenate([w[:, SRC_VS:SRC_KW], w[:, SRC_VW:SRC_GL],
                            jnp.pad(w[:, SRC_GL:], ((0, 0), (0, LANES - N_GATES)))], axis=1)
    return w_nat.astype(jnp.bfloat16), w_tr.T.astype(jnp.bfloat16)


def _inproj(x, mod6, w_nat, w_tr, tables, tm):
    B, S, D = x.shape
    cos, slo, shi = tables
    bs = lambda w: pl.BlockSpec((None, tm, w), lambda b, i: (b, i, 0))
    bst = pl.BlockSpec((None, LANES, tm), lambda b, i: (b, 0, i))
    tab = pl.BlockSpec((tm, LANES), lambda b, i: (i, 0))
    sd = lambda w, dt: jax.ShapeDtypeStruct((B, S, w), dt)
    sdt = lambda dt: jax.ShapeDtypeStruct((B, LANES, S), dt)
    bf, f32 = jnp.bfloat16, jnp.float32
    res = lambda d: pl.BlockSpec((None, d, tm // d, WA), lambda b, i: (b, 0, i, 0))
    res1 = pl.BlockSpec((None, None, tm, WA), lambda b, i: (b, 0, i, 0))
    sdr = lambda d: jax.ShapeDtypeStruct((B, d, S // d, WA), bf)
    return pl.pallas_call(
        functools.partial(_inproj_kernel, tm=tm),
        grid=(B, S // tm),
        in_specs=[bs(D),
                  pl.BlockSpec((None, 6, D), lambda b, i: (b, 0, 0)),
                  pl.BlockSpec((D, NAT_WIDTH), lambda b, i: (0, 0)),
                  pl.BlockSpec((TR_WIDTH, D), lambda b, i: (0, 0)),
                  tab, tab, tab],
        out_specs=[res1, res1, res1, bs(2 * WB), bs(WG), bs(WG), bs(2 * LANES), bs(WG), bst, bst, bst]
                  + [res(4)] * 3 + [res(16)] * 3,
        out_shape=[sdr(1), sdr(1), sdr(1), sd(2 * WB, bf), sd(WG, f32), sd(WG, f32),
                   sd(2 * LANES, bf), sd(WG, bf), sdt(bf), sdt(bf), sdt(f32)]
                  + [sdr(4)] * 3 + [sdr(16)] * 3,
        scratch_shapes=[pltpu.VMEM((tm, D), jnp.bfloat16),
                        pltpu.VMEM((tm, LANES), jnp.float32),
                        pltpu.VMEM((tm, LANES), jnp.float32)],
        compiler_params=_cparams(("parallel", "parallel")),
        name="inproj",
    )(x, mod6, w_nat, w_tr, cos, slo, shi)


def _dil_kernel(q_ref, kc_ref, kp_ref, vc_ref, vp_ref, o_ref, lse_ref, *, nblk):
    i = pl.program_id(2)
    blk_rows = BAND_BLOCK
    qi = lax.broadcasted_iota(jnp.int32, (blk_rows, 2 * blk_rows), 0)
    kj = lax.broadcasted_iota(jnp.int32, (blk_rows, 2 * blk_rows), 1)
    in_prev = kj < blk_rows
    bias = jnp.where(in_prev, jnp.where(kj >= qi, 0.0, NEG), jnp.where(kj - blk_rows <= qi, 0.0, NEG))
    bias0 = jnp.where(in_prev, jnp.where(i == 0, NEG, bias), bias)
    lane = lax.broadcasted_iota(jnp.int32, (1, LANES), 1)
    lo = lane < HEAD_DIM
    n_pair = WA // LANES
    for blk in range(nblk):
        rows = slice(blk * blk_rows, (blk + 1) * blk_rows)
        both = slice((blk - 1) * blk_rows, (blk + 1) * blk_rows)
        b = bias0 if blk == 0 else bias
        vals, ss = [], []
        for hp in range(n_pair):
            cols = slice(hp * LANES, (hp + 1) * LANES)
            qt = q_ref[rows, cols]
            if blk == 0:
                k2 = jnp.concatenate([kp_ref[:, cols], kc_ref[rows, cols]], axis=0)
                vals.append(jnp.concatenate([vp_ref[:, cols], vc_ref[rows, cols]], axis=0))
            else:
                k2 = kc_ref[both, cols]
                vals.append(vc_ref[both, cols])
            for h in range(2):
                qh = jnp.where(lo if h == 0 else jnp.logical_not(lo), qt, jnp.zeros_like(qt))
                ss.append(lax.dot_general(qh, k2, _NT, preferred_element_type=jnp.float32) + b)
        ms = [jnp.max(s, -1, keepdims=True) for s in ss]
        ps = [jnp.exp2(s - m) for s, m in zip(ss, ms)]
        dens = [jnp.sum(p, -1, keepdims=True) for p in ps]
        lse = jnp.zeros((blk_rows, LANES), jnp.float32)
        for hp in range(n_pair):
            cols = slice(hp * LANES, (hp + 1) * LANES)
            outs = []
            for h in range(2):
                n = 2 * hp + h
                o = jnp.dot(ps[n].astype(jnp.bfloat16), vals[hp], preferred_element_type=jnp.float32)
                outs.append(o * (1.0 / dens[n]))
                lse = jnp.where(lane // LSE_LANES == n, ms[n] + jnp.log2(dens[n]), lse)
            o_ref[rows, cols] = jnp.where(lo, outs[0], outs[1])
        lse_ref[rows, :] = lse


def _dilated(q, k, v, name):
    B, dil, L, W = q.shape
    rows = min(L, 4 * BAND_BLOCK)
    nblk = rows // BAND_BLOCK
    cur = pl.BlockSpec((None, None, rows, W), lambda b, r, i: (b, r, i, 0))
    cur_lse = pl.BlockSpec((None, None, rows, LANES), lambda b, r, i: (b, r, i, 0))
    prev = pl.BlockSpec((None, None, BAND_BLOCK, W), lambda b, r, i: (b, r, jnp.maximum(i * nblk - 1, 0), 0))
    return pl.pallas_call(
        functools.partial(_dil_kernel, nblk=nblk),
        grid=(B, dil, L // rows),
        in_specs=[cur, cur, prev, cur, prev],
        out_specs=[cur, cur_lse],
        out_shape=[jax.ShapeDtypeStruct((B, dil, L, W), jnp.float32),
                   jax.ShapeDtypeStruct((B, dil, L, LANES), jnp.float32)],
        compiler_params=_cparams(("parallel", "parallel", "arbitrary")),
        name=name,
    )(q, k, k, v, v)


def _cmp_kernel(ck_ref, cv_ref, pe_ref, wk1_ref, wk2_ref, wv1_ref, wv2t_ref, ok_ref, ovt_ref):
    n = ck_ref.shape[0] // CMP_STRIDE
    bf = jnp.bfloat16

    def hidden(c_ref, w1_ref):
        a = jnp.zeros((n, CMP_HIDDEN), jnp.float32)
        b = jnp.zeros((n, CMP_HIDDEN), jnp.float32)
        for j in range(CMP_STRIDE):
            t = c_ref[pl.ds(j, n, stride=CMP_STRIDE), :]
            a = a + jnp.dot((t + pe_ref[j:j + 1, :]).astype(bf), w1_ref[j], preferred_element_type=jnp.float32)
            b = b + jnp.dot((t + pe_ref[CMP_STRIDE + j:CMP_STRIDE + j + 1, :]).astype(bf), w1_ref[CMP_STRIDE + j],
                            preferred_element_type=jnp.float32)
        return _gelu_tanh(a + pltpu.roll(b, n - 1, 0)).astype(bf)

    ok_ref[...] = jnp.dot(hidden(ck_ref, wk1_ref), wk2_ref[...],
                          preferred_element_type=jnp.float32).astype(ok_ref.dtype)
    ovt_ref[...] = lax.dot_general(wv2t_ref[...], hidden(cv_ref, wv1_ref), _NT,
                                   preferred_element_type=jnp.float32).astype(ovt_ref.dtype)


def _compress(kc, vc, pe, w_ck1, w_ck2, w_cv1, w_cv2):
    B, S, _ = kc.shape
    n = S // CMP_STRIDE
    bf = jnp.bfloat16

    def slabs(w1):
        w = w1.reshape(CMP_LEN, HEAD_DIM, CMP_HIDDEN).astype(bf)
        z = jnp.zeros_like(w)
        return jnp.stack([jnp.concatenate([w, z], axis=1), jnp.concatenate([z, w], axis=1)], axis=0)

    seq = pl.BlockSpec((None, S, WG), lambda b, g: (b, 0, 0))
    full = lambda a: pl.BlockSpec(a.shape, lambda b, g: (0,) * a.ndim)
    slab = pl.BlockSpec((None, CMP_LEN, WG, CMP_HIDDEN), lambda b, g: (g, 0, 0, 0))
    pe2 = jnp.concatenate([pe, pe], axis=-1)
    args = (kc, vc, pe2, slabs(w_ck1), jnp.concatenate([w_ck2, w_ck2], axis=-1).astype(bf),
            slabs(w_cv1), w_cv2.T.astype(bf))
    return pl.pallas_call(
        _cmp_kernel,
        grid=(B, NSA_GROUPS),
        in_specs=[seq, seq, full(pe2), slab, full(args[4]), slab, full(args[6])],
        out_specs=[pl.BlockSpec((None, None, n, LANES), lambda b, g: (b, g, 0, 0)),
                   pl.BlockSpec((None, None, HEAD_DIM, n), lambda b, g: (b, g, 0, 0))],
        out_shape=[jax.ShapeDtypeStruct((B, NSA_GROUPS, n, LANES), bf),
                   jax.ShapeDtypeStruct((B, NSA_GROUPS, HEAD_DIM, n), bf)],
        compiler_params=_cparams(("parallel", "parallel")),
        name="compress",
    )(*args)


def _nsa_cmp_kernel(q_ref, kcc_ref, vcct_ref, ovt_ref, o_ref, sel_ref, *, tq, nsel, topk):
    qi = pl.program_id(2)
    t0 = qi * tq
    nc = kcc_ref.shape[0]
    tpos = t0 + lax.broadcasted_iota(jnp.int32, (nc, tq), 1)
    cend = lax.broadcasted_iota(jnp.int32, (nc, tq), 0) * CMP_STRIDE + (CMP_LEN - 1)
    cbias = jnp.where(cend <= tpos, 0.0, NEG)
    any_vis = (t0 + lax.broadcasted_iota(jnp.int32, (1, tq), 1)) >= CMP_LEN - 1
    kcc, vcct = kcc_ref[...], vcct_ref[...]
    ss = [lax.dot_general(kcc, q_ref[:, h * LANES:(h + 1) * LANES], _NT, preferred_element_type=jnp.float32) + cbias
          for h in range(NSA_HPG)]
    ms = [jnp.max(s, 0, keepdims=True) for s in ss]
    ps = [jnp.exp2(s - m).astype(jnp.bfloat16) for s, m in zip(ss, ms)]
    vaug = jnp.concatenate([vcct, jnp.ones((DEN_ROWS, nc), jnp.bfloat16)], axis=0)
    imp = jnp.zeros((LANES, tq), jnp.float32)
    for h in range(NSA_HPG):
        oa = jnp.dot(vaug, ps[h], preferred_element_type=jnp.float32)
        inv = jnp.where(any_vis, 1.0 / oa[HEAD_DIM:HEAD_DIM + 1], 0.0)
        o_ref[h * HEAD_DIM:(h + 1) * HEAD_DIM, :] = oa[:HEAD_DIM] * inv
        imp = imp + jnp.dot(ovt_ref[...], ps[h], preferred_element_type=jnp.float32) * inv
    blk = lax.broadcasted_iota(jnp.int32, (LANES, tq), 0)
    tq_pos = t0 + lax.broadcasted_iota(jnp.int32, (LANES, tq), 1)
    cur = tq_pos // SLC_BLOCK
    forced = (blk == cur) | (blk == cur - 1) | (blk == 0)
    valid = blk <= cur
    score = jnp.where(valid & jnp.logical_not(forced) & (blk < nsel), imp, -jnp.inf)
    blk_f = blk.astype(jnp.float32)
    sel = jnp.where(forced, 1.0, 0.0)
    for _ in range(topk - 3):
        mx = jnp.max(score, 0, keepdims=True)
        first = jnp.min(jnp.where(score == mx, blk_f, float(2 * LANES)), 0, keepdims=True)
        hit = blk_f == first
        sel = jnp.where(hit, 1.0, sel)
        score = jnp.where(hit, -jnp.inf, score)
    sel = jnp.where(valid, sel, 0.0)
    sel_ref[...] = ((sel.T - 1.0) * (-NEG)).astype(jnp.bfloat16)


def _overlap_t(S):
    nc = S // CMP_STRIDE
    ns = S // SLC_BLOCK
    cs = np.arange(nc) * CMP_STRIDE
    ss = np.arange(LANES) * SLC_BLOCK
    ov = np.clip(np.minimum(cs[None, :] + CMP_LEN, ss[:, None] + SLC_BLOCK)
                 - np.maximum(cs[None, :], ss[:, None]), 0, None).astype(np.float32) / CMP_LEN
    ov[ns:, :] = 0.0
    ov[:, nc - CMP_LEN // CMP_STRIDE + 1:] = 0.0
    return jnp.asarray(ov, jnp.bfloat16)


def _nsa_cmp(qbx, kcc, vcct, tq):
    B, S, _ = qbx.shape
    nc = kcc.shape[2]
    ns = S // SLC_BLOCK
    gw = NSA_HPG * LANES
    ovt = _overlap_t(S)
    return pl.pallas_call(
        functools.partial(_nsa_cmp_kernel, tq=tq, nsel=ns, topk=min(SLC_TOPK, ns)),
        grid=(B, NSA_GROUPS, S // tq),
        in_specs=[pl.BlockSpec((None, tq, gw), lambda b, g, i: (b, i, g)),
                  pl.BlockSpec((None, None, nc, LANES), lambda b, g, i: (b, g, 0, 0)),
                  pl.BlockSpec((None, None, HEAD_DIM, nc), lambda b, g, i: (b, g, 0, 0)),
                  pl.BlockSpec((LANES, nc), lambda b, g, i: (0, 0))],
        out_specs=[pl.BlockSpec((None, NSA_HPG * HEAD_DIM, tq), lambda b, g, i: (b, g, i)),
                   pl.BlockSpec((None, None, tq, LANES), lambda b, g, i: (b, g, i, 0))],
        out_shape=[jax.ShapeDtypeStruct((B, WB, S), jnp.float32),
                   jax.ShapeDtypeStruct((B, NSA_GROUPS, S, LANES), jnp.bfloat16)],
        compiler_params=_cparams(("parallel", "parallel", "parallel")),
        name="nsa_cmp",
    )(qbx, kcc, vcct, ovt)


def _nsa_main_kernel(q_ref, sel_ref, ksa_ref, vst_ref, kw_ref, vwt_ref, ocmp_ref, gate_ref, o_ref,
                     qa_scr, m_scr, acc_scr, win_scr, sa_scr, sb_scr, *, tq, tk):
    g = pl.program_id(1)
    qi = pl.program_id(2)
    t0 = qi * tq
    cols = NSA_HPG * tq
    bf = jnp.bfloat16
    selb = sel_ref[...]
    for h in range(NSA_HPG):
        qa_scr[h * tq:(h + 1) * tq, 0:LANES] = selb
        qa_scr[h * tq:(h + 1) * tq, LANES:2 * LANES] = q_ref[:, h * LANES:(h + 1) * LANES]
    m_scr[...] = jnp.full((1, cols), -1e37, jnp.float32)
    acc_scr[...] = jnp.zeros(acc_scr.shape, jnp.float32)

    def with_ones(vt):
        return jnp.concatenate([vt, jnp.ones((DEN_ROWS, vt.shape[1]), bf)], axis=0)

    def scores(tile, dst):
        start = pl.multiple_of(tile * tk, tk)
        dst[...] = lax.dot_general(ksa_ref[pl.ds(start, tk), :], qa_scr[...], _NT,
                                   preferred_element_type=jnp.float32)

    def consume(tile, src):
        start = pl.multiple_of(tile * tk, tk)
        vt = with_ones(vst_ref[:, pl.ds(start, tk)])
        m_prev = m_scr[...]
        m_new = jnp.maximum(m_prev, jnp.max(src[...], 0, keepdims=True))
        alpha = jnp.exp2(m_prev - m_new)
        p = jnp.exp2(src[...] - m_new)
        acc_scr[...] = alpha * acc_scr[...] + jnp.dot(vt, p.astype(bf), preferred_element_type=jnp.float32)
        m_scr[...] = m_new

    last = t0 // tk
    scores(0, sa_scr)

    def window():
        wk = WIN + tq
        start = pl.multiple_of(jnp.maximum(t0 - WIN, 0), tq)
        kwin = kw_ref[pl.ds(start, wk), :]
        sw = lax.dot_general(kwin, qa_scr[:, LANES:2 * LANES], _NT, preferred_element_type=jnp.float32)
        dist = ((lax.broadcasted_iota(jnp.int32, (wk, tq), 1) + t0)
                - (lax.broadcasted_iota(jnp.int32, (wk, tq), 0) + start))
        wb = jnp.where((dist >= 0) & (dist < WIN), 0.0, NEG)
        sw = sw + jnp.concatenate([wb] * NSA_HPG, axis=1)
        pw = jnp.exp2(sw - jnp.max(sw, 0, keepdims=True))
        ow = jnp.dot(with_ones(vwt_ref[:, pl.ds(start, wk)]), pw.astype(bf), preferred_element_type=jnp.float32)
        win_scr[...] = ow[:HEAD_DIM] * (1.0 / ow[HEAD_DIM:HEAD_DIM + 1])

    def pair_body(jj, carry):
        scores(2 * jj + 1, sb_scr)
        consume(2 * jj, sa_scr)
        scores(2 * jj + 2, sa_scr)
        consume(2 * jj + 1, sb_scr)
        return carry

    lax.fori_loop(0, last // 2, pair_body, 0)

    kk = lax.broadcasted_iota(jnp.int32, (tq, tq), 0)
    qq = lax.broadcasted_iota(jnp.int32, (tq, tq), 1)
    cb = jnp.where(kk <= qq, 0.0, NEG)
    cb = jnp.concatenate([cb] * NSA_HPG, axis=1)
    diag = pl.multiple_of(t0 - last * tk, tq)

    def finish(src):
        src[pl.ds(diag, tq), :] = src[pl.ds(diag, tq), :] + cb
        window()
        consume(last, src)

    @pl.when(last % 2 == 1)
    def _():
        scores(last, sb_scr)
        consume(last - 1, sa_scr)
        finish(sb_scr)

    @pl.when(last % 2 == 0)
    def _():
        finish(sa_scr)

    o_slc = acc_scr[0:HEAD_DIM, :] * (1.0 / acc_scr[HEAD_DIM:HEAD_DIM + 1, :])
    o_win = win_scr[...]

    base = g * (NSA_HPG * 3)
    heads = []
    for h in range(NSA_HPG):
        c = slice(h * tq, (h + 1) * tq)
        g_cmp = gate_ref[pl.ds(base + 3 * h, 1), :]
        g_slc = gate_ref[pl.ds(base + 3 * h + 1, 1), :]
        g_win = gate_ref[pl.ds(base + 3 * h + 2, 1), :]
        heads.append(g_cmp * ocmp_ref[h * HEAD_DIM:(h + 1) * HEAD_DIM, :] + g_slc * o_slc[:, c] + g_win * o_win[:, c])
    o_ref[...] = jnp.concatenate(heads, axis=0).T.astype(o_ref.dtype)


def _nsa_main(qbx, selb, ksa, vst, kw, vwt, o_cmp_t, gates_t, tq, tk):
    B, S, _ = qbx.shape
    gw = NSA_HPG * LANES
    ow = NSA_HPG * HEAD_DIM
    cols = NSA_HPG * tq
    seq = lambda w: pl.BlockSpec((None, S, w), lambda b, g, i: (b, 0, 0))
    seqt = pl.BlockSpec((None, HEAD_DIM, S), lambda b, g, i: (b, g, 0))
    return pl.pallas_call(
        functools.partial(_nsa_main_kernel, tq=tq, tk=tk),
        grid=(B, NSA_GROUPS, S // tq),
        in_specs=[pl.BlockSpec((None, tq, gw), lambda b, g, i: (b, i, g)),
                  pl.BlockSpec((None, None, tq, LANES), lambda b, g, i: (b, g, i, 0)),
                  seq(2 * LANES), seqt, seq(WG), seqt,
                  pl.BlockSpec((None, ow, tq), lambda b, g, i: (b, g, i)),
                  pl.BlockSpec((None, LANES, tq), lambda b, g, i: (b, 0, i))],
        out_specs=pl.BlockSpec((None, tq, ow), lambda b, g, i: (b, i, g)),
        out_shape=jax.ShapeDtypeStruct((B, S, WB), jnp.bfloat16),
        scratch_shapes=[pltpu.VMEM((cols, 2 * LANES), jnp.bfloat16),
                        pltpu.VMEM((1, cols), jnp.float32),
                        pltpu.VMEM((HEAD_DIM + DEN_ROWS, cols), jnp.float32),
                        pltpu.VMEM((HEAD_DIM, cols), jnp.float32),
                        pltpu.VMEM((tk, cols), jnp.float32),
                        pltpu.VMEM((tk, cols), jnp.float32)],
        compiler_params=_cparams(("parallel", "parallel", "arbitrary")),
        name="nsa_main",
    )(qbx, selb, ksa, vst, kw, vwt, o_cmp_t, gates_t)


def _outproj_kernel(x_ref, o1_ref, l1_ref, o4_ref, l4_ref, o16_ref, l16_ref, ob_ref, mod_ref, wo_ref, g_ref, b_ref,
                    ex_ref, y_ref, oa_scr, s0_scr, s1_scr, s2_scr, s3_scr, *, tm):
    def token_order(src, cols, scr):
        dil = src.shape[0]
        for r in range(dil):
            scr[pl.ds(r, tm // dil, stride=dil), :] = src[r, :, cols]
        return scr[...]

    l1 = l1_ref[...]
    l2 = token_order(l4_ref, slice(None), s0_scr)
    l3 = token_order(l16_ref, slice(None), s1_scr)
    m = jnp.maximum(jnp.maximum(l1, l2), l3)
    es = [jnp.exp2(l1 - m), jnp.exp2(l2 - m), jnp.exp2(l3 - m)]
    inv = 1.0 / (es[0] + es[1] + es[2])

    def widen(w):
        hi = w.astype(jnp.bfloat16)
        lo = (w - hi.astype(jnp.float32)).astype(jnp.bfloat16)
        return jnp.dot(jnp.concatenate([hi, lo], axis=1), ex_ref[...], preferred_element_type=jnp.float32)

    w1, w2, w3 = [widen(e * inv) for e in es]
    for c in range(WA // LANES):
        cols = slice(c * LANES, (c + 1) * LANES)
        o2 = token_order(o4_ref, cols, s2_scr)
        o3 = token_order(o16_ref, cols, s3_scr)
        o_a = w1[:, cols] * o1_ref[:, cols] + w2[:, cols] * o2 + w3[:, cols] * o3
        oa_scr[:, cols] = o_a.astype(jnp.bfloat16)
    y = (jnp.dot(oa_scr[...], wo_ref[:WA, :], preferred_element_type=jnp.float32)
         + jnp.dot(ob_ref[...], wo_ref[WA:, :], preferred_element_type=jnp.float32))
    z = ALPHA * x_ref[...] + mod_ref[2:3, :] * y
    y_ref[...] = _ln(z) * g_ref[...] + b_ref[...]


def _outproj(x, pats, o_b, mod6, wo_bf16, ln_g, ln_b, tm):
    B, S, D = x.shape
    bs = lambda w: pl.BlockSpec((None, tm, w), lambda b, i: (b, i, 0))
    res = lambda d, w: pl.BlockSpec((None, d, tm // d, w), lambda b, i: (b, 0, i, 0))
    res1 = lambda w: pl.BlockSpec((None, None, tm, w), lambda b, i: (b, 0, i, 0))
    vec = pl.BlockSpec((1, D), lambda b, i: (0, 0))
    (o1, l1), (o4, l4), (o16, l16) = pats
    scr = pltpu.VMEM((tm, LANES), jnp.float32)
    expand = np.zeros((LANES, WA), np.float32)
    expand[(np.arange(WA) // HEAD_DIM) * LSE_LANES, np.arange(WA)] = 1.0
    expand = jnp.asarray(np.concatenate([expand, expand], axis=0), jnp.bfloat16)
    return pl.pallas_call(
        functools.partial(_outproj_kernel, tm=tm),
        grid=(B, S // tm),
        in_specs=[bs(D), res1(WA), res1(LANES), res(4, WA), res(4, LANES), res(16, WA), res(16, LANES), bs(WB),
                  pl.BlockSpec((None, 6, D), lambda b, i: (b, 0, 0)),
                  pl.BlockSpec((WA + WB, D), lambda b, i: (0, 0)), vec, vec,
                  pl.BlockSpec((2 * LANES, WA), lambda b, i: (0, 0))],
        out_specs=bs(D),
        out_shape=jax.ShapeDtypeStruct((B, S, D), jnp.float32),
        scratch_shapes=[pltpu.VMEM((tm, WA), jnp.bfloat16), scr, scr, scr, scr],
        compiler_params=_cparams(("parallel", "parallel")),
        name="outproj",
    )(x, o1, l1, o4, l4, o16, l16, o_b, mod6, wo_bf16,
      ln_g.reshape(1, D), ln_b.reshape(1, D), expand)


def _ffn_kernel(x_ref, mod_ref, wup_ref, cw_ref, cb_ref, wdn_ref, g_ref, b_ref, y_ref,
                u_scr, buf_scr, carry_scr, acc_scr, *, tm, fc):
    i = pl.program_id(1)
    F = wdn_ref.shape[0]
    pad = 8

    @pl.when(i == 0)
    def _():
        carry_scr[...] = jnp.zeros_like(carry_scr)

    x = x_ref[...]
    u_scr[...] = (_ln(x) * (1.0 + mod_ref[4:5, :]) + mod_ref[3:4, :]).astype(jnp.bfloat16)
    def up(c):
        return (jnp.dot(u_scr[...], wup_ref[:, c * fc:(c + 1) * fc], preferred_element_type=jnp.float32),
                jnp.dot(u_scr[...], wup_ref[:, F + c * fc:F + (c + 1) * fc], preferred_element_type=jnp.float32))

    n_chunk = F // fc
    nxt = up(0)
    for c in range(n_chunk):
        cols = slice(c * fc, (c + 1) * fc)
        a_gate, a_val = nxt
        if c + 1 < n_chunk:
            nxt = up(c + 1)
        buf_scr[0:pad, :] = carry_scr[:, cols]
        buf_scr[pad:pad + tm, :] = a_gate
        carry_scr[:, cols] = a_gate[tm - pad:tm, :]
        conv = (cw_ref[0:1, cols] * buf_scr[pad - 2:pad - 2 + tm, :]
                + cw_ref[1:2, cols] * buf_scr[pad - 1:pad - 1 + tm, :]
                + cw_ref[2:3, cols] * a_gate + cb_ref[:, cols])
        h = (_gelu_tanh(conv) * a_val).astype(jnp.bfloat16)
        d = jnp.dot(h, wdn_ref[cols, :], preferred_element_type=jnp.float32)
        if c == 0:
            acc_scr[...] = d
        else:
            acc_scr[...] += d
    z = ALPHA * x + mod_ref[5:6, :] * acc_scr[...]
    y_ref[...] = _ln(z) * g_ref[...] + b_ref[...]


def _ffn(x, mod6, wup_bf16, conv_w, conv_b, wdn_bf16, ln_g, ln_b, tm, fc):
    B, S, D = x.shape
    F = wdn_bf16.shape[0]
    bs = pl.BlockSpec((None, tm, D), lambda b, i: (b, i, 0))
    full = lambda a: pl.BlockSpec(a.shape, lambda b, i: (0,) * a.ndim)
    args = (wup_bf16, conv_w, conv_b.reshape(1, F), wdn_bf16, ln_g.reshape(1, D), ln_b.reshape(1, D))
    return pl.pallas_call(
        functools.partial(_ffn_kernel, tm=tm, fc=fc),
        grid=(B, S // tm),
        in_specs=[bs, pl.BlockSpec((None, 6, D), lambda b, i: (b, 0, 0))] + [full(a) for a in args],
        out_specs=bs,
        out_shape=jax.ShapeDtypeStruct((B, S, D), jnp.float32),
        scratch_shapes=[pltpu.VMEM((tm, D), jnp.bfloat16),
                        pltpu.VMEM((tm + 8, fc), jnp.float32),
                        pltpu.VMEM((8, F), jnp.float32),
                        pltpu.VMEM((tm, D), jnp.float32)],
        compiler_params=_cparams(("arbitrary", "arbitrary")),
        name="ffn",
    )(x, mod6, *args)


def kernel(x, c, w_ada, b_ada, w_in, pe_cmp, w_ck1, w_ck2, w_cv1, w_cv2, w_o, ln1_g, ln1_b, w_up, conv_w, conv_b,
           w_down, ln2_g, ln2_b):
    B, S, D = x.shape
    tables = _rope_tables(S)
    tm = min(512, S)
    tq = 128
    tk = min(512, S)
    for l in range(DEPTH):
        mod6 = _ada(c, w_ada[l], b_ada[l]).reshape(B, 6, D)
        w_nat, w_tr = _prep_w_in(w_in[l])
        (qa, ka, va, qbx, kc, vc, ksa, kw, vst, vwt, gates_t,
         q4, k4, v4, q16, k16, v16) = _inproj(x, mod6, w_nat, w_tr, tables, tm)
        pats = [_dilated(qa, ka, va, "dilated1"),
                _dilated(q4, k4, v4, "dilated4"), _dilated(q16, k16, v16, "dilated16")]
        kcc, vcct = _compress(kc, vc, pe_cmp[l], w_ck1[l], w_ck2[l], w_cv1[l], w_cv2[l])
        o_cmp_t, selb = _nsa_cmp(qbx, kcc, vcct, 4 * tq)
        o_b = _nsa_main(qbx, selb, ksa, vst, kw, vwt, o_cmp_t, gates_t, 2 * tq, tk)
        x = _outproj(x, pats, o_b, mod6, w_o[l].astype(jnp.bfloat16), ln1_g[l], ln1_b[l], tm)
        x = _ffn(x, mod6, w_up[l].astype(jnp.bfloat16), conv_w[l], conv_b[l], w_down[l].astype(jnp.bfloat16),
                 ln2_g[l], ln2_b[l], tm // 2, 256)
    return x
```

```python
import functools
import math

import numpy as np
import jax
import jax.numpy as jnp
from jax import lax
from jax.experimental import pallas as pl
from jax.experimental.pallas import tpu as pltpu

HEAD_DIM = 64
N_HEADS_A = 8
N_HEADS_B = 8
DIL_PATTERNS = ((128, 1), (512, 4), (2048, 16))
BAND_BLOCK = 128
ROT_DIM = HEAD_DIM // 4
ROPE_THETA = 500000.0
NSA_GROUPS = 2
NSA_HPG = N_HEADS_B // NSA_GROUPS
CMP_LEN = 32
CMP_STRIDE = 16
CMP_HIDDEN = 4 * HEAD_DIM
SLC_BLOCK = 64
SLC_TOPK = 16
WIN = 512
D_FF = 2816
CONV_W = 3
DEPTH = 1
ALPHA = (2 * DEPTH) ** 0.25
LN_EPS = 1e-5
NEG = -1e30
LOG2E = math.log2(math.e)

LANES = 128
MXU_N = 256
DEN_ROWS = 16
LSE_LANES = LANES // N_HEADS_A
WA = N_HEADS_A * HEAD_DIM
WB = N_HEADS_B * HEAD_DIM
WG = NSA_GROUPS * HEAD_DIM
N_GATES = N_HEADS_B * 3
SRC_VS = 3 * WA + WB + 3 * WG
SRC_KW = SRC_VS + WG
SRC_VW = SRC_KW + WG
SRC_GL = SRC_VW + WG
OFF_QA, OFF_KA, OFF_VA, OFF_QB = 0, WA, 2 * WA, 3 * WA
OFF_KC = OFF_QB + WB
OFF_VC = OFF_KC + WG
OFF_KS = OFF_VC + WG
OFF_KW = OFF_KS + WG
NAT_WIDTH = OFF_KW + WG
TR_VS, TR_VW, TR_GL = 0, WG, 2 * WG
TR_WIDTH = 3 * WG

VMEM_LIMIT = 56 * 1024 * 1024

_NT = (((1,), (1,)), ((), ()))


def _cparams(sem):
    return pltpu.CompilerParams(dimension_semantics=sem, vmem_limit_bytes=VMEM_LIMIT)


def _gelu_tanh(x):
    return 0.5 * x * (1.0 + jnp.tanh(math.sqrt(2.0 / math.pi) * (x + 0.044715 * (x * x * x))))


def _ln(x):
    mu = jnp.mean(x, -1, keepdims=True)
    xc = x - mu
    var = jnp.mean(xc * xc, -1, keepdims=True)
    return xc * lax.rsqrt(var + LN_EPS)


def _ada_kernel(c_ref, w_ref, b_ref, o_ref):
    c = c_ref[...]
    a = c * jax.nn.sigmoid(c)
    o_ref[...] = jnp.dot(a, w_ref[...], preferred_element_type=jnp.float32) + b_ref[...]


def _ada(c, w, b):
    B, D = c.shape
    N = w.shape[1]
    tn = D
    return pl.pallas_call(
        _ada_kernel,
        grid=(N // tn,),
        in_specs=[pl.BlockSpec((B, D), lambda j: (0, 0)),
                  pl.BlockSpec((D, tn), lambda j: (0, j)),
                  pl.BlockSpec((1, tn), lambda j: (0, j))],
        out_specs=pl.BlockSpec((B, tn), lambda j: (0, j)),
        out_shape=jax.ShapeDtypeStruct((B, N), jnp.float32),
        compiler_params=_cparams(("arbitrary",)),
        name="adaln",
    )(c, w, b.reshape(1, N))


def _rope_tile(t, cos, s_lo, s_hi):
    half = ROT_DIM // 2
    return t * cos + pltpu.roll(t, half, 1) * s_hi + pltpu.roll(t, LANES - half, 1) * s_lo


def _inproj_kernel(x_ref, mod_ref, w_ref, wt_ref, cos_ref, slo_ref, shi_ref,
                   qa_ref, ka_ref, va_ref, qbx_ref, kc_ref, vc_ref, ksa_ref, kw_ref, vst_ref, vwt_ref, gate_ref,
                   q4_ref, k4_ref, v4_ref, q16_ref, k16_ref, v16_ref,
                   u_ref, ra_ref, rb_ref, *, tm):
    i = pl.program_id(1)
    x = x_ref[...]
    u = _ln(x) * (1.0 + mod_ref[1:2, :]) + mod_ref[0:1, :]
    u_ref[...] = u.astype(jnp.bfloat16)
    cos, slo, shi = cos_ref[...], slo_ref[...], shi_ref[...]
    lane = lax.broadcasted_iota(jnp.int32, (1, LANES), 1)
    bf = jnp.bfloat16

    def proj(off):
        r = jnp.dot(u_ref[...], w_ref[:, off:off + MXU_N], preferred_element_type=jnp.float32)
        return r[:, :LANES], r[:, LANES:]

    def rope(t):
        return _rope_tile(t, cos, slo, shi)

    def emit(t, cols, nat_ref, r4_ref, r16_ref):
        nat_ref[:, cols] = t.astype(bf)
        ra_ref[...] = t
        n4 = tm // 4
        for r in range(4):
            part = ra_ref[pl.ds(r, n4, stride=4), :]
            r4_ref[r, :, cols] = part.astype(bf)
            rb_ref[r * n4:(r + 1) * n4, :] = part
        for r in range(4):
            for m in range(4):
                part = rb_ref[pl.ds(r * n4 + m, n4 // 4, stride=4), :]
                r16_ref[r + 4 * m, :, cols] = part.astype(bf)

    for j in range(WA // MXU_N):
        for k, t in enumerate(proj(OFF_QA + j * MXU_N)):
            emit(rope(t) * LOG2E, slice((2 * j + k) * LANES, (2 * j + k + 1) * LANES), qa_ref, q4_ref, q16_ref)
        for k, t in enumerate(proj(OFF_KA + j * MXU_N)):
            emit(rope(t), slice((2 * j + k) * LANES, (2 * j + k + 1) * LANES), ka_ref, k4_ref, k16_ref)
        for k, t in enumerate(proj(OFF_VA + j * MXU_N)):
            emit(t, slice((2 * j + k) * LANES, (2 * j + k + 1) * LANES), va_ref, v4_ref, v16_ref)
    lo = lane < HEAD_DIM
    for j in range(WB // MXU_N):
        for k, t in enumerate(proj(OFF_QB + j * MXU_N)):
            pair = 2 * j + k
            t = rope(t) * LOG2E
            g = (2 * pair) // NSA_HPG
            tr = pltpu.roll(t, HEAD_DIM, 1)
            in_g = lo if g == 0 else jnp.logical_not(lo)
            even = jnp.where(in_g, t if g == 0 else tr, 0.0)
            odd = jnp.where(in_g, tr if g == 0 else t, 0.0)
            qbx_ref[:, (2 * pair) * LANES:(2 * pair + 1) * LANES] = even.astype(bf)
            qbx_ref[:, (2 * pair + 1) * LANES:(2 * pair + 2) * LANES] = odd.astype(bf)
    kc, vc = proj(OFF_KC)
    kc_ref[...] = rope(kc)
    vc_ref[...] = vc
    ks, kw = proj(OFF_KS)
    tpos = i * tm + lax.broadcasted_iota(jnp.int32, (tm, LANES), 0)
    blk = lax.broadcasted_iota(jnp.int32, (tm, LANES), 1)
    ksa_ref[:, 0:LANES] = jnp.where((tpos // SLC_BLOCK) == blk, 1.0, 0.0).astype(bf)
    ksa_ref[:, LANES:2 * LANES] = rope(ks).astype(bf)
    kw_ref[...] = rope(kw).astype(bf)
    tr_out = lax.dot_general(wt_ref[...], u_ref[...], _NT, preferred_element_type=jnp.float32)
    vst_ref[...] = tr_out[TR_VS:TR_VS + WG].astype(bf)
    vwt_ref[...] = tr_out[TR_VW:TR_VW + WG].astype(bf)
    gate_ref[...] = jax.nn.sigmoid(tr_out[TR_GL:TR_GL + LANES])


def _rope_tables(S):
    inv = 1.0 / (ROPE_THETA ** (jnp.arange(0, ROT_DIM, 2, dtype=jnp.float32) / ROT_DIM))
    ang = jnp.arange(S, dtype=jnp.float32)[:, None] * inv[None, :]
    cos, sin = jnp.cos(ang), jnp.sin(ang)
    half = ROT_DIM // 2
    d = np.arange(LANES) % HEAD_DIM
    idx = jnp.asarray(d % half)
    rot = jnp.asarray(d < ROT_DIM)
    lo = jnp.asarray(d < half)
    hi = jnp.asarray((d >= half) & (d < ROT_DIM))
    c_t = jnp.where(rot[None], cos[:, idx], 1.0)
    s_lo = jnp.where(lo[None], -sin[:, idx], 0.0)
    s_hi = jnp.where(hi[None], sin[:, idx], 0.0)
    return c_t, s_lo, s_hi


def _prep_w_in(w):
    q_scale = HEAD_DIM ** -0.5
    w_nat = jnp.concatenate([w[:, 0:WA] * q_scale, w[:, WA:3 * WA], w[:, 3 * WA:3 * WA + WB] * q_scale,
                             w[:, 3 * WA + WB:SRC_VS], w[:, SRC_KW:SRC_VW]], axis=1)
    w_tr = jnp.concatenate([w[:, SRC_VS:SRC_KW], w[:, SRC_VW:SRC_GL],
                            jnp.pad(w[:, SRC_GL:], ((0, 0), (0, LANES - N_GATES)))], axis=1)
    return w_nat.astype(jnp.bfloat16), w_tr.T.astype(jnp.bfloat16)


def _inproj(x, mod6, w_nat, w_tr, tables, tm):
    B, S, D = x.shape
    cos, slo, shi = tables
    bs = lambda w: pl.BlockSpec((None, tm, w), lambda b, i: (b, i, 0))
    bst = pl.BlockSpec((None, LANES, tm), lambda b, i: (b, 0, i))
    tab = pl.BlockSpec((tm, LANES), lambda b, i: (i, 0))
    sd = lambda w, dt: jax.ShapeDtypeStruct((B, S, w), dt)
    sdt = lambda dt: jax.ShapeDtypeStruct((B, LANES, S), dt)
    bf, f32 = jnp.bfloat16, jnp.float32
    res = lambda d: pl.BlockSpec((None, d, tm // d, WA), lambda b, i: (b, 0, i, 0))
    res1 = pl.BlockSpec((None, None, tm, WA), lambda b, i: (b, 0, i, 0))
    sdr = lambda d: jax.ShapeDtypeStruct((B, d, S // d, WA), bf)
    return pl.pallas_call(
        functools.partial(_inproj_kernel, tm=tm),
        grid=(B, S // tm),
        in_specs=[bs(D),
                  pl.BlockSpec((None, 6, D), lambda b, i: (b, 0, 0)),
                  pl.BlockSpec((D, NAT_WIDTH), lambda b, i: (0, 0)),
                  pl.BlockSpec((TR_WIDTH, D), lambda b, i: (0, 0)),
                  tab, tab, tab],
        out_specs=[res1, res1, res1, bs(2 * WB), bs(WG), bs(WG), bs(2 * LANES), bs(WG), bst, bst, bst]
                  + [res(4)] * 3 + [res(16)] * 3,
        out_shape=[sdr(1), sdr(1), sdr(1), sd(2 * WB, bf), sd(WG, f32), sd(WG, f32),
                   sd(2 * LANES, bf), sd(WG, bf), sdt(bf), sdt(bf), sdt(f32)]
                  + [sdr(4)] * 3 + [sdr(16)] * 3,
        scratch_shapes=[pltpu.VMEM((tm, D), jnp.bfloat16),
                        pltpu.VMEM((tm, LANES), jnp.float32),
                        pltpu.VMEM((tm, LANES), jnp.float32)],
        compiler_params=_cparams(("parallel", "parallel")),
        name="inproj",
    )(x, mod6, w_nat, w_tr, cos, slo, shi)


def _dil_kernel(q_ref, kc_ref, kp_ref, vc_ref, vp_ref, o_ref, lse_ref, *, nblk):
    i = pl.program_id(2)
    blk_rows = BAND_BLOCK
    qi = lax.broadcasted_iota(jnp.int32, (blk_rows, 2 * blk_rows), 0)
    kj = lax.broadcasted_iota(jnp.int32, (blk_rows, 2 * blk_rows), 1)
    in_prev = kj < blk_rows
    bias = jnp.where(in_prev, jnp.where(kj >= qi, 0.0, NEG), jnp.where(kj - blk_rows <= qi, 0.0, NEG))
    bias0 = jnp.where(in_prev, jnp.where(i == 0, NEG, bias), bias)
    lane = lax.broadcasted_iota(jnp.int32, (1, LANES), 1)
    lo = lane < HEAD_DIM
    n_pair = WA // LANES
    for blk in range(nblk):
        rows = slice(blk * blk_rows, (blk + 1) * blk_rows)
        both = slice((blk - 1) * blk_rows, (blk + 1) * blk_rows)
        b = bias0 if blk == 0 else bias
        vals, ss = [], []
        for hp in range(n_pair):
            cols = slice(hp * LANES, (hp + 1) * LANES)
            qt = q_ref[rows, cols]
            if blk == 0:
                k2 = jnp.concatenate([kp_ref[:, cols], kc_ref[rows, cols]], axis=0)
                vals.append(jnp.concatenate([vp_ref[:, cols], vc_ref[rows, cols]], axis=0))
            else:
                k2 = kc_ref[both, cols]
                vals.append(vc_ref[both, cols])
            for h in range(2):
                qh = jnp.where(lo if h == 0 else jnp.logical_not(lo), qt, jnp.zeros_like(qt))
                ss.append(lax.dot_general(qh, k2, _NT, preferred_element_type=jnp.float32) + b)
        ms = [jnp.max(s, -1, keepdims=True) for s in ss]
        ps = [jnp.exp2(s - m) for s, m in zip(ss, ms)]
        dens = [jnp.sum(p, -1, keepdims=True) for p in ps]
        lse = jnp.zeros((blk_rows, LANES), jnp.float32)
        for hp in range(n_pair):
            cols = slice(hp * LANES, (hp + 1) * LANES)
            outs = []
            for h in range(2):
                n = 2 * hp + h
                o = jnp.dot(ps[n].astype(jnp.bfloat16), vals[hp], preferred_element_type=jnp.float32)
                outs.append(o * (1.0 / dens[n]))
                lse = jnp.where(lane // LSE_LANES == n, ms[n] + jnp.log2(dens[n]), lse)
            o_ref[rows, cols] = jnp.where(lo, outs[0], outs[1])
        lse_ref[rows, :] = lse


def _dilated(q, k, v, name):
    B, dil, L, W = q.shape
    rows = min(L, 4 * BAND_BLOCK)
    nblk = rows // BAND_BLOCK
    cur = pl.BlockSpec((None, None, rows, W), lambda b, r, i: (b, r, i, 0))
    cur_lse = pl.BlockSpec((None, None, rows, LANES), lambda b, r, i: (b, r, i, 0))
    prev = pl.BlockSpec((None, None, BAND_BLOCK, W), lambda b, r, i: (b, r, jnp.maximum(i * nblk - 1, 0), 0))
    return pl.pallas_call(
        functools.partial(_dil_kernel, nblk=nblk),
        grid=(B, dil, L // rows),
        in_specs=[cur, cur, prev, cur, prev],
        out_specs=[cur, cur_lse],
        out_shape=[jax.ShapeDtypeStruct((B, dil, L, W), jnp.float32),
                   jax.ShapeDtypeStruct((B, dil, L, LANES), jnp.float32)],
        compiler_params=_cparams(("parallel", "parallel", "arbitrary")),
        name=name,
    )(q, k, k, v, v)


def _cmp_kernel(ck_ref, cv_ref, pe_ref, wk1_ref, wk2_ref, wv1_ref, wv2t_ref, ok_ref, ovt_ref):
    n = ck_ref.shape[0] // CMP_STRIDE
    bf = jnp.bfloat16

    def hidden(c_ref, w1_ref):
        a = jnp.zeros((n, CMP_HIDDEN), jnp.float32)
        b = jnp.zeros((n, CMP_HIDDEN), jnp.float32)
        for j in range(CMP_STRIDE):
            t = c_ref[pl.ds(j, n, stride=CMP_STRIDE), :]
            a = a + jnp.dot((t + pe_ref[j:j + 1, :]).astype(bf), w1_ref[j], preferred_element_type=jnp.float32)
            b = b + jnp.dot((t + pe_ref[CMP_STRIDE + j:CMP_STRIDE + j + 1, :]).astype(bf), w1_ref[CMP_STRIDE + j],
                            preferred_element_type=jnp.float32)
        return _gelu_tanh(a + pltpu.roll(b, n - 1, 0)).astype(bf)

    ok_ref[...] = jnp.dot(hidden(ck_ref, wk1_ref), wk2_ref[...],
                          preferred_element_type=jnp.float32).astype(ok_ref.dtype)
    ovt_ref[...] = lax.dot_general(wv2t_ref[...], hidden(cv_ref, wv1_ref), _NT,
                                   preferred_element_type=jnp.float32).astype(ovt_ref.dtype)


def _compress(kc, vc, pe, w_ck1, w_ck2, w_cv1, w_cv2):
    B, S, _ = kc.shape
    n = S // CMP_STRIDE
    bf = jnp.bfloat16

    def slabs(w1):
        w = w1.reshape(CMP_LEN, HEAD_DIM, CMP_HIDDEN).astype(bf)
        z = jnp.zeros_like(w)
        return jnp.stack([jnp.concatenate([w, z], axis=1), jnp.concatenate([z, w], axis=1)], axis=0)

    seq = pl.BlockSpec((None, S, WG), lambda b, g: (b, 0, 0))
    full = lambda a: pl.BlockSpec(a.shape, lambda b, g: (0,) * a.ndim)
    slab = pl.BlockSpec((None, CMP_LEN, WG, CMP_HIDDEN), lambda b, g: (g, 0, 0, 0))
    pe2 = jnp.concatenate([pe, pe], axis=-1)
    args = (kc, vc, pe2, slabs(w_ck1), jnp.concatenate([w_ck2, w_ck2], axis=-1).astype(bf),
            slabs(w_cv1), w_cv2.T.astype(bf))
    return pl.pallas_call(
        _cmp_kernel,
        grid=(B, NSA_GROUPS),
        in_specs=[seq, seq, full(pe2), slab, full(args[4]), slab, full(args[6])],
        out_specs=[pl.BlockSpec((None, None, n, LANES), lambda b, g: (b, g, 0, 0)),
                   pl.BlockSpec((None, None, HEAD_DIM, n), lambda b, g: (b, g, 0, 0))],
        out_shape=[jax.ShapeDtypeStruct((B, NSA_GROUPS, n, LANES), bf),
                   jax.ShapeDtypeStruct((B, NSA_GROUPS, HEAD_DIM, n), bf)],
        compiler_params=_cparams(("parallel", "parallel")),
        name="compress",
    )(*args)


def _nsa_cmp_kernel(q_ref, kcc_ref, vcct_ref, ovt_ref, o_ref, sel_ref, *, tq, nsel, topk):
    qi = pl.program_id(2)
    t0 = qi * tq
    nc = kcc_ref.shape[0]
    tpos = t0 + lax.broadcasted_iota(jnp.int32, (nc, tq), 1)
    cend = lax.broadcasted_iota(jnp.int32, (nc, tq), 0) * CMP_STRIDE + (CMP_LEN - 1)
    cbias = jnp.where(cend <= tpos, 0.0, NEG)
    any_vis = (t0 + lax.broadcasted_iota(jnp.int32, (1, tq), 1)) >= CMP_LEN - 1
    kcc, vcct = kcc_ref[...], vcct_ref[...]
    ss = [lax.dot_general(kcc, q_ref[:, h * LANES:(h + 1) * LANES], _NT, preferred_element_type=jnp.float32) + cbias
          for h in range(NSA_HPG)]
    ms = [jnp.max(s, 0, keepdims=True) for s in ss]
    ps = [jnp.exp2(s - m).astype(jnp.bfloat16) for s, m in zip(ss, ms)]
    vaug = jnp.concatenate([vcct, jnp.ones((DEN_ROWS, nc), jnp.bfloat16)], axis=0)
    imp = jnp.zeros((LANES, tq), jnp.float32)
    for h in range(NSA_HPG):
        oa = jnp.dot(vaug, ps[h], preferred_element_type=jnp.float32)
        inv = jnp.where(any_vis, 1.0 / oa[HEAD_DIM:HEAD_DIM + 1], 0.0)
        o_ref[h * HEAD_DIM:(h + 1) * HEAD_DIM, :] = oa[:HEAD_DIM] * inv
        imp = imp + jnp.dot(ovt_ref[...], ps[h], preferred_element_type=jnp.float32) * inv
    blk = lax.broadcasted_iota(jnp.int32, (LANES, tq), 0)
    tq_pos = t0 + lax.broadcasted_iota(jnp.int32, (LANES, tq), 1)
    cur = tq_pos // SLC_BLOCK
    forced = (blk == cur) | (blk == cur - 1) | (blk == 0)
    valid = blk <= cur
    score = jnp.where(valid & jnp.logical_not(forced) & (blk < nsel), imp, -jnp.inf)
    blk_f = blk.astype(jnp.float32)
    sel = jnp.where(forced, 1.0, 0.0)
    for _ in range(topk - 3):
        mx = jnp.max(score, 0, keepdims=True)
        first = jnp.min(jnp.where(score == mx, blk_f, float(2 * LANES)), 0, keepdims=True)
        hit = blk_f == first
        sel = jnp.where(hit, 1.0, sel)
        score = jnp.where(hit, -jnp.inf, score)
    sel = jnp.where(valid, sel, 0.0)
    sel_ref[...] = ((sel.T - 1.0) * (-NEG)).astype(jnp.bfloat16)


def _overlap_t(S):
    nc = S // CMP_STRIDE
    ns = S // SLC_BLOCK
    cs = np.arange(nc) * CMP_STRIDE
    ss = np.arange(LANES) * SLC_BLOCK
    ov = np.clip(np.minimum(cs[None, :] + CMP_LEN, ss[:, None] + SLC_BLOCK)
                 - np.maximum(cs[None, :], ss[:, None]), 0, None).astype(np.float32) / CMP_LEN
    ov[ns:, :] = 0.0
    ov[:, nc - CMP_LEN // CMP_STRIDE + 1:] = 0.0
    return jnp.asarray(ov, jnp.bfloat16)


def _nsa_cmp(qbx, kcc, vcct, tq):
    B, S, _ = qbx.shape
    nc = kcc.shape[2]
    ns = S // SLC_BLOCK
    gw = NSA_HPG * LANES
    ovt = _overlap_t(S)
    return pl.pallas_call(
        functools.partial(_nsa_cmp_kernel, tq=tq, nsel=ns, topk=min(SLC_TOPK, ns)),
        grid=(B, NSA_GROUPS, S // tq),
        in_specs=[pl.BlockSpec((None, tq, gw), lambda b, g, i: (b, i, g)),
                  pl.BlockSpec((None, None, nc, LANES), lambda b, g, i: (b, g, 0, 0)),
                  pl.BlockSpec((None, None, HEAD_DIM, nc), lambda b, g, i: (b, g, 0, 0)),
                  pl.BlockSpec((LANES, nc), lambda b, g, i: (0, 0))],
        out_specs=[pl.BlockSpec((None, NSA_HPG * HEAD_DIM, tq), lambda b, g, i: (b, g, i)),
                   pl.BlockSpec((None, None, tq, LANES), lambda b, g, i: (b, g, i, 0))],
        out_shape=[jax.ShapeDtypeStruct((B, WB, S), jnp.float32),
                   jax.ShapeDtypeStruct((B, NSA_GROUPS, S, LANES), jnp.bfloat16)],
        compiler_params=_cparams(("parallel", "parallel", "parallel")),
        name="nsa_cmp",
    )(qbx, kcc, vcct, ovt)


def _nsa_main_kernel(q_ref, sel_ref, ksa_ref, vst_ref, kw_ref, vwt_ref, ocmp_ref, gate_ref, o_ref,
                     qa_scr, m_scr, acc_scr, win_scr, sa_scr, sb_scr, ma_scr, mb_scr, *, tq, tk):
    g = pl.program_id(1)
    qi = pl.program_id(2)
    t0 = qi * tq
    cols = NSA_HPG * tq
    bf = jnp.bfloat16
    selb = sel_ref[...]
    for h in range(NSA_HPG):
        qa_scr[h * tq:(h + 1) * tq, 0:LANES] = selb
        qa_scr[h * tq:(h + 1) * tq, LANES:2 * LANES] = q_ref[:, h * LANES:(h + 1) * LANES]
    m_scr[...] = jnp.full((1, cols), -1e37, jnp.float32)
    acc_scr[...] = jnp.zeros(acc_scr.shape, jnp.float32)

    def with_ones(vt):
        return jnp.concatenate([vt, jnp.ones((DEN_ROWS, vt.shape[1]), bf)], axis=0)

    def scores(tile, dst, dst_max):
        start = pl.multiple_of(tile * tk, tk)
        s = lax.dot_general(ksa_ref[pl.ds(start, tk), :], qa_scr[...], _NT,
                            preferred_element_type=jnp.float32)
        dst[...] = s
        dst_max[...] = jnp.max(s, 0, keepdims=True)

    def consume(tile, src, src_max):
        start = pl.multiple_of(tile * tk, tk)
        vt = with_ones(vst_ref[:, pl.ds(start, tk)])
        m_prev = m_scr[...]
        m_new = jnp.maximum(m_prev, src_max[...])
        alpha = jnp.exp2(m_prev - m_new)
        p = jnp.exp2(src[...] - m_new)
        acc_scr[...] = alpha * acc_scr[...] + jnp.dot(vt, p.astype(bf), preferred_element_type=jnp.float32)
        m_scr[...] = m_new

    last = t0 // tk
    scores(0, sa_scr, ma_scr)

    def window():
        wk = WIN + tq
        start = pl.multiple_of(jnp.maximum(t0 - WIN, 0), tq)
        kwin = kw_ref[pl.ds(start, wk), :]
        sw = lax.dot_general(kwin, qa_scr[:, LANES:2 * LANES], _NT, preferred_element_type=jnp.float32)
        dist = ((lax.broadcasted_iota(jnp.int32, (wk, tq), 1) + t0)
                - (lax.broadcasted_iota(jnp.int32, (wk, tq), 0) + start))
        wb = jnp.where((dist >= 0) & (dist < WIN), 0.0, NEG)
        sw = sw + jnp.concatenate([wb] * NSA_HPG, axis=1)
        pw = jnp.exp2(sw - jnp.max(sw, 0, keepdims=True))
        ow = jnp.dot(with_ones(vwt_ref[:, pl.ds(start, wk)]), pw.astype(bf), preferred_element_type=jnp.float32)
        win_scr[...] = ow[:HEAD_DIM] * (1.0 / ow[HEAD_DIM:HEAD_DIM + 1])

    def pair_body(jj, carry):
        scores(2 * jj + 1, sb_scr, mb_scr)
        consume(2 * jj, sa_scr, ma_scr)
        scores(2 * jj + 2, sa_scr, ma_scr)
        consume(2 * jj + 1, sb_scr, mb_scr)
        return carry

    lax.fori_loop(0, last // 2, pair_body, 0)

    kk = lax.broadcasted_iota(jnp.int32, (tq, tq), 0)
    qq = lax.broadcasted_iota(jnp.int32, (tq, tq), 1)
    cb = jnp.where(kk <= qq, 0.0, NEG)
    cb = jnp.concatenate([cb] * NSA_HPG, axis=1)
    diag = pl.multiple_of(t0 - last * tk, tq)

    def finish(src):
        wk = WIN + tq
        wstart = pl.multiple_of(jnp.maximum(t0 - WIN, 0), tq)
        sw = lax.dot_general(kw_ref[pl.ds(wstart, wk), :], qa_scr[:, LANES:2 * LANES], _NT,
                             preferred_element_type=jnp.float32)
        dist = ((lax.broadcasted_iota(jnp.int32, (wk, tq), 1) + t0)
                - (lax.broadcasted_iota(jnp.int32, (wk, tq), 0) + wstart))
        wb = jnp.where((dist >= 0) & (dist < WIN), 0.0, NEG)
        sw = sw + jnp.concatenate([wb] * NSA_HPG, axis=1)
        src[pl.ds(diag, tq), :] = src[pl.ds(diag, tq), :] + cb
        start = pl.multiple_of(last * tk, tk)
        m_prev = m_scr[...]
        m_new = jnp.maximum(m_prev, jnp.max(src[...], 0, keepdims=True))
        mw = jnp.max(sw, 0, keepdims=True)
        alpha = jnp.exp2(m_prev - m_new)
        p = jnp.exp2(src[...] - m_new).astype(bf)
        pw = jnp.exp2(sw - mw).astype(bf)
        pv = jnp.dot(with_ones(vst_ref[:, pl.ds(start, tk)]), p, preferred_element_type=jnp.float32)
        ow = jnp.dot(with_ones(vwt_ref[:, pl.ds(wstart, wk)]), pw, preferred_element_type=jnp.float32)
        acc_scr[...] = alpha * acc_scr[...] + pv
        win_scr[...] = ow[:HEAD_DIM] * (1.0 / ow[HEAD_DIM:HEAD_DIM + 1])

    @pl.when(last % 2 == 1)
    def _():
        scores(last, sb_scr, mb_scr)
        consume(last - 1, sa_scr, ma_scr)
        finish(sb_scr)

    @pl.when(last % 2 == 0)
    def _():
        finish(sa_scr)

    o_slc = acc_scr[0:HEAD_DIM, :] * (1.0 / acc_scr[HEAD_DIM:HEAD_DIM + 1, :])
    o_win = win_scr[...]

    base = g * (NSA_HPG * 3)
    heads = []
    for h in range(NSA_HPG):
        c = slice(h * tq, (h + 1) * tq)
        g_cmp = gate_ref[pl.ds(base + 3 * h, 1), :]
        g_slc = gate_ref[pl.ds(base + 3 * h + 1, 1), :]
        g_win = gate_ref[pl.ds(base + 3 * h + 2, 1), :]
        heads.append(g_cmp * ocmp_ref[h * HEAD_DIM:(h + 1) * HEAD_DIM, :] + g_slc * o_slc[:, c] + g_win * o_win[:, c])
    o_ref[...] = jnp.concatenate(heads, axis=0).T.astype(o_ref.dtype)


def _nsa_main(qbx, selb, ksa, vst, kw, vwt, o_cmp_t, gates_t, tq, tk):
    B, S, _ = qbx.shape
    gw = NSA_HPG * LANES
    ow = NSA_HPG * HEAD_DIM
    cols = NSA_HPG * tq
    seq = lambda w: pl.BlockSpec((None, S, w), lambda b, g, i: (b, 0, 0))
    seqt = pl.BlockSpec((None, HEAD_DIM, S), lambda b, g, i: (b, g, 0))
    return pl.pallas_call(
        functools.partial(_nsa_main_kernel, tq=tq, tk=tk),
        grid=(B, NSA_GROUPS, S // tq),
        in_specs=[pl.BlockSpec((None, tq, gw), lambda b, g, i: (b, i, g)),
                  pl.BlockSpec((None, None, tq, LANES), lambda b, g, i: (b, g, i, 0)),
                  seq(2 * LANES), seqt, seq(WG), seqt,
                  pl.BlockSpec((None, ow, tq), lambda b, g, i: (b, g, i)),
                  pl.BlockSpec((None, LANES, tq), lambda b, g, i: (b, 0, i))],
        out_specs=pl.BlockSpec((None, tq, ow), lambda b, g, i: (b, i, g)),
        out_shape=jax.ShapeDtypeStruct((B, S, WB), jnp.bfloat16),
        scratch_shapes=[pltpu.VMEM((cols, 2 * LANES), jnp.bfloat16),
                        pltpu.VMEM((1, cols), jnp.float32),
                        pltpu.VMEM((HEAD_DIM + DEN_ROWS, cols), jnp.float32),
                        pltpu.VMEM((HEAD_DIM, cols), jnp.float32),
                        pltpu.VMEM((tk, cols), jnp.float32),
                        pltpu.VMEM((tk, cols), jnp.float32),
                        pltpu.VMEM((1, cols), jnp.float32),
                        pltpu.VMEM((1, cols), jnp.float32)],
        compiler_params=_cparams(("parallel", "parallel", "arbitrary")),
        name="nsa_main",
    )(qbx, selb, ksa, vst, kw, vwt, o_cmp_t, gates_t)


def _outproj_kernel(x_ref, o1_ref, l1_ref, o4_ref, l4_ref, o16_ref, l16_ref, ob_ref, mod_ref, wo_ref, g_ref, b_ref,
                    ex_ref, y_ref, oa_scr, s0_scr, s1_scr, s2_scr, s3_scr, *, tm):
    def token_order(src, cols, scr):
        dil = src.shape[0]
        for r in range(dil):
            scr[pl.ds(r, tm // dil, stride=dil), :] = src[r, :, cols]
        return scr[...]

    l1 = l1_ref[...]
    l2 = token_order(l4_ref, slice(None), s0_scr)
    l3 = token_order(l16_ref, slice(None), s1_scr)
    m = jnp.maximum(jnp.maximum(l1, l2), l3)
    es = [jnp.exp2(l1 - m), jnp.exp2(l2 - m), jnp.exp2(l3 - m)]
    inv = 1.0 / (es[0] + es[1] + es[2])

    def widen(w):
        hi = w.astype(jnp.bfloat16)
        lo = (w - hi.astype(jnp.float32)).astype(jnp.bfloat16)
        return jnp.dot(jnp.concatenate([hi, lo], axis=1), ex_ref[...], preferred_element_type=jnp.float32)

    w1, w2, w3 = [widen(e * inv) for e in es]
    for c in range(WA // LANES):
        cols = slice(c * LANES, (c + 1) * LANES)
        o2 = token_order(o4_ref, cols, s2_scr)
        o3 = token_order(o16_ref, cols, s3_scr)
        o_a = w1[:, cols] * o1_ref[:, cols] + w2[:, cols] * o2 + w3[:, cols] * o3
        oa_scr[:, cols] = o_a.astype(jnp.bfloat16)
    y = (jnp.dot(oa_scr[...], wo_ref[:WA, :], preferred_element_type=jnp.float32)
         + jnp.dot(ob_ref[...], wo_ref[WA:, :], preferred_element_type=jnp.float32))
    z = ALPHA * x_ref[...] + mod_ref[2:3, :] * y
    y_ref[...] = _ln(z) * g_ref[...] + b_ref[...]


def _outproj(x, pats, o_b, mod6, wo_bf16, ln_g, ln_b, tm):
    B, S, D = x.shape
    bs = lambda w: pl.BlockSpec((None, tm, w), lambda b, i: (b, i, 0))
    res = lambda d, w: pl.BlockSpec((None, d, tm // d, w), lambda b, i: (b, 0, i, 0))
    res1 = lambda w: pl.BlockSpec((None, None, tm, w), lambda b, i: (b, 0, i, 0))
    vec = pl.BlockSpec((1, D), lambda b, i: (0, 0))
    (o1, l1), (o4, l4), (o16, l16) = pats
    scr = pltpu.VMEM((tm, LANES), jnp.float32)
    expand = np.zeros((LANES, WA), np.float32)
    expand[(np.arange(WA) // HEAD_DIM) * LSE_LANES, np.arange(WA)] = 1.0
    expand = jnp.asarray(np.concatenate([expand, expand], axis=0), jnp.bfloat16)
    return pl.pallas_call(
        functools.partial(_outproj_kernel, tm=tm),
        grid=(B, S // tm),
        in_specs=[bs(D), res1(WA), res1(LANES), res(4, WA), res(4, LANES), res(16, WA), res(16, LANES), bs(WB),
                  pl.BlockSpec((None, 6, D), lambda b, i: (b, 0, 0)),
                  pl.BlockSpec((WA + WB, D), lambda b, i: (0, 0)), vec, vec,
                  pl.BlockSpec((2 * LANES, WA), lambda b, i: (0, 0))],
        out_specs=bs(D),
        out_shape=jax.ShapeDtypeStruct((B, S, D), jnp.float32),
        scratch_shapes=[pltpu.VMEM((tm, WA), jnp.bfloat16), scr, scr, scr, scr],
        compiler_params=_cparams(("parallel", "parallel")),
        name="outproj",
    )(x, o1, l1, o4, l4, o16, l16, o_b, mod6, wo_bf16,
      ln_g.reshape(1, D), ln_b.reshape(1, D), expand)


def _ffn_kernel(x_ref, mod_ref, wup_ref, cw_ref, cb_ref, wdn_ref, g_ref, b_ref, y_ref,
                u_scr, buf_scr, carry_scr, acc_scr, *, tm, fc):
    i = pl.program_id(1)
    F = wdn_ref.shape[0]
    pad = 8

    @pl.when(i == 0)
    def _():
        carry_scr[...] = jnp.zeros_like(carry_scr)

    th = tm // 2
    n_chunk = F // fc
    halves = (slice(0, th), slice(th, tm))

    def prologue(rows):
        u_scr[rows, :] = (_ln(x_ref[rows, :]) * (1.0 + mod_ref[4:5, :]) + mod_ref[3:4, :]).astype(jnp.bfloat16)

    def up(rows, c):
        u = u_scr[rows, :]
        return (jnp.dot(u, wup_ref[:, c * fc:(c + 1) * fc], preferred_element_type=jnp.float32),
                jnp.dot(u, wup_ref[:, F + c * fc:F + (c + 1) * fc], preferred_element_type=jnp.float32))

    def chunk(rows, c, a_gate, a_val):
        cols = slice(c * fc, (c + 1) * fc)
        buf_scr[0:pad, :] = carry_scr[:, cols]
        buf_scr[pad:pad + th, :] = a_gate
        carry_scr[:, cols] = a_gate[th - pad:th, :]
        conv = (cw_ref[0:1, cols] * buf_scr[pad - 2:pad - 2 + th, :]
                + cw_ref[1:2, cols] * buf_scr[pad - 1:pad - 1 + th, :]
                + cw_ref[2:3, cols] * a_gate + cb_ref[:, cols])
        h = (_gelu_tanh(conv) * a_val).astype(jnp.bfloat16)
        d = jnp.dot(h, wdn_ref[cols, :], preferred_element_type=jnp.float32)
        if c == 0:
            acc_scr[rows, :] = d
        else:
            acc_scr[rows, :] += d

    def epilogue(rows):
        z = ALPHA * x_ref[rows, :] + mod_ref[5:6, :] * acc_scr[rows, :]
        y_ref[rows, :] = _ln(z) * g_ref[...] + b_ref[...]

    items = [(rows, c) for rows in halves for c in range(n_chunk)]
    prologue(halves[0])
    nxt = up(*items[0])
    for k, (rows, c) in enumerate(items):
        cur = nxt
        if k == 0:
            prologue(halves[1])
        if k + 1 < len(items):
            nxt = up(*items[k + 1])
        if k == n_chunk + 1:
            epilogue(halves[0])
        chunk(rows, c, *cur)
    epilogue(halves[1])


def _ffn(x, mod6, wup_bf16, conv_w, conv_b, wdn_bf16, ln_g, ln_b, tm, fc):
    B, S, D = x.shape
    F = wdn_bf16.shape[0]
    bs = pl.BlockSpec((None, tm, D), lambda b, i: (b, i, 0))
    full = lambda a: pl.BlockSpec(a.shape, lambda b, i: (0,) * a.ndim)
    args = (wup_bf16, conv_w, conv_b.reshape(1, F), wdn_bf16, ln_g.reshape(1, D), ln_b.reshape(1, D))
    return pl.pallas_call(
        functools.partial(_ffn_kernel, tm=tm, fc=fc),
        grid=(B, S // tm),
        in_specs=[bs, pl.BlockSpec((None, 6, D), lambda b, i: (b, 0, 0))] + [full(a) for a in args],
        out_specs=bs,
        out_shape=jax.ShapeDtypeStruct((B, S, D), jnp.float32),
        scratch_shapes=[pltpu.VMEM((tm, D), jnp.bfloat16),
                        pltpu.VMEM((tm // 2 + 8, fc), jnp.float32),
                        pltpu.VMEM((8, F), jnp.float32),
                        pltpu.VMEM((tm, D), jnp.float32)],
        compiler_params=_cparams(("arbitrary", "arbitrary")),
        name="ffn",
    )(x, mod6, *args)


def kernel(x, c, w_ada, b_ada, w_in, pe_cmp, w_ck1, w_ck2, w_cv1, w_cv2, w_o, ln1_g, ln1_b, w_up, conv_w, conv_b,
           w_down, ln2_g, ln2_b):
    B, S, D = x.shape
    tables = _rope_tables(S)
    tm = min(512, S)
    tq = 128
    tk = min(512, S)
    for l in range(DEPTH):
        mod6 = _ada(c, w_ada[l], b_ada[l]).reshape(B, 6, D)
        w_nat, w_tr = _prep_w_in(w_in[l])
        (qa, ka, va, qbx, kc, vc, ksa, kw, vst, vwt, gates_t,
         q4, k4, v4, q16, k16, v16) = _inproj(x, mod6, w_nat, w_tr, tables, tm)
        pats = [_dilated(qa, ka, va, "dilated1"),
                _dilated(q4, k4, v4, "dilated4"), _dilated(q16, k16, v16, "dilated16")]
        kcc, vcct = _compress(kc, vc, pe_cmp[l], w_ck1[l], w_ck2[l], w_cv1[l], w_cv2[l])
        o_cmp_t, selb = _nsa_cmp(qbx, kcc, vcct, 4 * tq)
        o_b = _nsa_main(qbx, selb, ksa, vst, kw, vwt, o_cmp_t, gates_t, 2 * tq, tk)
        x = _outproj(x, pats, o_b, mod6, w_o[l].astype(jnp.bfloat16), ln1_g[l], ln1_b[l], tm)
        x = _ffn(x, mod6, w_up[l].astype(jnp.bfloat16), conv_w[l], conv_b[l], w_down[l].astype(jnp.bfloat16),
                 ln2_g[l], ln2_b[l], tm, 256)
    return x
```

```python
import functools
import math

import numpy as np
import jax
import jax.numpy as jnp
from jax import lax
from jax.experimental import pallas as pl
from jax.experimental.pallas import tpu as pltpu

HEAD_DIM = 64
N_HEADS_A = 8
N_HEADS_B = 8
DIL_PATTERNS = ((128, 1), (512, 4), (2048, 16))
BAND_BLOCK = 128
ROT_DIM = HEAD_DIM // 4
ROPE_THETA = 500000.0
NSA_GROUPS = 2
NSA_HPG = N_HEADS_B // NSA_GROUPS
CMP_LEN = 32
CMP_STRIDE = 16
CMP_HIDDEN = 4 * HEAD_DIM
SLC_BLOCK = 64
SLC_TOPK = 16
WIN = 512
D_FF = 2816
CONV_W = 3
DEPTH = 1
ALPHA = (2 * DEPTH) ** 0.25
LN_EPS = 1e-5
NEG = -1e30
LOG2E = math.log2(math.e)

LANES = 128
MXU_N = 256
DEN_ROWS = 16
LSE_LANES = LANES // N_HEADS_A
WA = N_HEADS_A * HEAD_DIM
WB = N_HEADS_B * HEAD_DIM
WG = NSA_GROUPS * HEAD_DIM
N_GATES = N_HEADS_B * 3
SRC_VS = 3 * WA + WB + 3 * WG
SRC_KW = SRC_VS + WG
SRC_VW = SRC_KW + WG
SRC_GL = SRC_VW + WG
OFF_QA, OFF_KA, OFF_VA, OFF_QB = 0, WA, 2 * WA, 3 * WA
OFF_KC = OFF_QB + WB
OFF_VC = OFF_KC + WG
OFF_KS = OFF_VC + WG
OFF_KW = OFF_KS + WG
NAT_WIDTH = OFF_KW + WG
TR_VS, TR_VW, TR_GL = 0, WG, 2 * WG
TR_WIDTH = 3 * WG

VMEM_LIMIT = 56 * 1024 * 1024

_NT = (((1,), (1,)), ((), ()))


def _cparams(sem):
    return pltpu.CompilerParams(dimension_semantics=sem, vmem_limit_bytes=VMEM_LIMIT)


def _gelu_tanh(x):
    return 0.5 * x * (1.0 + jnp.tanh(math.sqrt(2.0 / math.pi) * (x + 0.044715 * (x * x * x))))


def _ln(x):
    mu = jnp.mean(x, -1, keepdims=True)
    xc = x - mu
    var = jnp.mean(xc * xc, -1, keepdims=True)
    return xc * lax.rsqrt(var + LN_EPS)


def _ada_kernel(c_ref, w_ref, b_ref, o_ref):
    c = c_ref[...]
    a = c * jax.nn.sigmoid(c)
    o_ref[...] = jnp.dot(a, w_ref[...], preferred_element_type=jnp.float32) + b_ref[...]


def _ada(c, w, b):
    B, D = c.shape
    N = w.shape[1]
    tn = D
    return pl.pallas_call(
        _ada_kernel,
        grid=(N // tn,),
        in_specs=[pl.BlockSpec((B, D), lambda j: (0, 0)),
                  pl.BlockSpec((D, tn), lambda j: (0, j)),
                  pl.BlockSpec((1, tn), lambda j: (0, j))],
        out_specs=pl.BlockSpec((B, tn), lambda j: (0, j)),
        out_shape=jax.ShapeDtypeStruct((B, N), jnp.float32),
        compiler_params=_cparams(("arbitrary",)),
        name="adaln",
    )(c, w, b.reshape(1, N))


def _rope_tile(t, cos, s_lo, s_hi):
    half = ROT_DIM // 2
    return t * cos + pltpu.roll(t, half, 1) * s_hi + pltpu.roll(t, LANES - half, 1) * s_lo


def _inproj_kernel(x_ref, mod_ref, w_ref, wt_ref, cos_ref, slo_ref, shi_ref,
                   qa_ref, ka_ref, va_ref, qbx_ref, kc_ref, vc_ref, ksa_ref, kw_ref, vst_ref, vwt_ref, gate_ref,
                   q4_ref, k4_ref, v4_ref, q16_ref, k16_ref, v16_ref,
                   u_ref, ra_ref, rb_ref, *, tm):
    i = pl.program_id(1)
    x = x_ref[...]
    u = _ln(x) * (1.0 + mod_ref[1:2, :]) + mod_ref[0:1, :]
    u_ref[...] = u.astype(jnp.bfloat16)
    cos, slo, shi = cos_ref[...], slo_ref[...], shi_ref[...]
    lane = lax.broadcasted_iota(jnp.int32, (1, LANES), 1)
    bf = jnp.bfloat16

    def proj(off):
        r = jnp.dot(u_ref[...], w_ref[:, off:off + MXU_N], preferred_element_type=jnp.float32)
        return r[:, :LANES], r[:, LANES:]

    def rope(t):
        return _rope_tile(t, cos, slo, shi)

    def emit(t, cols, nat_ref, r4_ref, r16_ref):
        nat_ref[:, cols] = t.astype(bf)
        ra_ref[...] = t
        n4 = tm // 4
        for r in range(4):
            part = ra_ref[pl.ds(r, n4, stride=4), :]
            r4_ref[r, :, cols] = part.astype(bf)
            rb_ref[r * n4:(r + 1) * n4, :] = part
        for r in range(4):
            for m in range(4):
                part = rb_ref[pl.ds(r * n4 + m, n4 // 4, stride=4), :]
                r16_ref[r + 4 * m, :, cols] = part.astype(bf)

    for j in range(WA // MXU_N):
        for k, t in enumerate(proj(OFF_QA + j * MXU_N)):
            emit(rope(t) * LOG2E, slice((2 * j + k) * LANES, (2 * j + k + 1) * LANES), qa_ref, q4_ref, q16_ref)
        for k, t in enumerate(proj(OFF_KA + j * MXU_N)):
            emit(rope(t), slice((2 * j + k) * LANES, (2 * j + k + 1) * LANES), ka_ref, k4_ref, k16_ref)
        for k, t in enumerate(proj(OFF_VA + j * MXU_N)):
            emit(t, slice((2 * j + k) * LANES, (2 * j + k + 1) * LANES), va_ref, v4_ref, v16_ref)
    lo = lane < HEAD_DIM
    for j in range(WB // MXU_N):
        for k, t in enumerate(proj(OFF_QB + j * MXU_N)):
            pair = 2 * j + k
            t = rope(t) * LOG2E
            g = (2 * pair) // NSA_HPG
            tr = pltpu.roll(t, HEAD_DIM, 1)
            in_g = lo if g == 0 else jnp.logical_not(lo)
            even = jnp.where(in_g, t if g == 0 else tr, 0.0)
            odd = jnp.where(in_g, tr if g == 0 else t, 0.0)
            qbx_ref[:, (2 * pair) * LANES:(2 * pair + 1) * LANES] = even.astype(bf)
            qbx_ref[:, (2 * pair + 1) * LANES:(2 * pair + 2) * LANES] = odd.astype(bf)
    kc, vc = proj(OFF_KC)
    kc_ref[...] = rope(kc)
    vc_ref[...] = vc
    ks, kw = proj(OFF_KS)
    tpos = i * tm + lax.broadcasted_iota(jnp.int32, (tm, LANES), 0)
    blk = lax.broadcasted_iota(jnp.int32, (tm, LANES), 1)
    ksa_ref[:, 0:LANES] = jnp.where((tpos // SLC_BLOCK) == blk, 1.0, 0.0).astype(bf)
    ksa_ref[:, LANES:2 * LANES] = rope(ks).astype(bf)
    kw_ref[...] = rope(kw).astype(bf)
    tr_out = lax.dot_general(wt_ref[...], u_ref[...], _NT, preferred_element_type=jnp.float32)
    vst_ref[...] = tr_out[TR_VS:TR_VS + WG].astype(bf)
    vwt_ref[...] = tr_out[TR_VW:TR_VW + WG].astype(bf)
    gate_ref[...] = jax.nn.sigmoid(tr_out[TR_GL:TR_GL + LANES])


def _rope_tables(S):
    inv = 1.0 / (ROPE_THETA ** (jnp.arange(0, ROT_DIM, 2, dtype=jnp.float32) / ROT_DIM))
    ang = jnp.arange(S, dtype=jnp.float32)[:, None] * inv[None, :]
    cos, sin = jnp.cos(ang), jnp.sin(ang)
    half = ROT_DIM // 2
    d = np.arange(LANES) % HEAD_DIM
    idx = jnp.asarray(d % half)
    rot = jnp.asarray(d < ROT_DIM)
    lo = jnp.asarray(d < half)
    hi = jnp.asarray((d >= half) & (d < ROT_DIM))
    c_t = jnp.where(rot[None], cos[:, idx], 1.0)
    s_lo = jnp.where(lo[None], -sin[:, idx], 0.0)
    s_hi = jnp.where(hi[None], sin[:, idx], 0.0)
    return c_t, s_lo, s_hi


def _prep_w_in(w):
    q_scale = HEAD_DIM ** -0.5
    w_nat = jnp.concatenate([w[:, 0:WA] * q_scale, w[:, WA:3 * WA], w[:, 3 * WA:3 * WA + WB] * q_scale,
                             w[:, 3 * WA + WB:SRC_VS], w[:, SRC_KW:SRC_VW]], axis=1)
    w_tr = jnp.concatenate([w[:, SRC_VS:SRC_KW], w[:, SRC_VW:SRC_GL],
                            jnp.pad(w[:, SRC_GL:], ((0, 0), (0, LANES - N_GATES)))], axis=1)
    return w_nat.astype(jnp.bfloat16), w_tr.T.astype(jnp.bfloat16)


def _inproj(x, mod6, w_nat, w_tr, tables, tm):
    B, S, D = x.shape
    cos, slo, shi = tables
    bs = lambda w: pl.BlockSpec((None, tm, w), lambda b, i: (b, i, 0))
    bst = pl.BlockSpec((None, LANES, tm), lambda b, i: (b, 0, i))
    tab = pl.BlockSpec((tm, LANES), lambda b, i: (i, 0))
    sd = lambda w, dt: jax.ShapeDtypeStruct((B, S, w), dt)
    sdt = lambda dt: jax.ShapeDtypeStruct((B, LANES, S), dt)
    bf, f32 = jnp.bfloat16, jnp.float32
    res = lambda d: pl.BlockSpec((None, d, tm // d, WA), lambda b, i: (b, 0, i, 0))
    res1 = pl.BlockSpec((None, None, tm, WA), lambda b, i: (b, 0, i, 0))
    sdr = lambda d: jax.ShapeDtypeStruct((B, d, S // d, WA), bf)
    return pl.pallas_call(
        functools.partial(_inproj_kernel, tm=tm),
        grid=(B, S // tm),
        in_specs=[bs(D),
                  pl.BlockSpec((None, 6, D), lambda b, i: (b, 0, 0)),
                  pl.BlockSpec((D, NAT_WIDTH), lambda b, i: (0, 0)),
                  pl.BlockSpec((TR_WIDTH, D), lambda b, i: (0, 0)),
                  tab, tab, tab],
        out_specs=[res1, res1, res1, bs(2 * WB), bs(WG), bs(WG), bs(2 * LANES), bs(WG), bst, bst, bst]
                  + [res(4)] * 3 + [res(16)] * 3,
        out_shape=[sdr(1), sdr(1), sdr(1), sd(2 * WB, bf), sd(WG, f32), sd(WG, f32),
                   sd(2 * LANES, bf), sd(WG, bf), sdt(bf), sdt(bf), sdt(f32)]
                  + [sdr(4)] * 3 + [sdr(16)] * 3,
        scratch_shapes=[pltpu.VMEM((tm, D), jnp.bfloat16),
                        pltpu.VMEM((tm, LANES), jnp.float32),
                        pltpu.VMEM((tm, LANES), jnp.float32)],
        compiler_params=_cparams(("parallel", "parallel")),
        name="inproj",
    )(x, mod6, w_nat, w_tr, cos, slo, shi)


def _dil_kernel(q_ref, kc_ref, kp_ref, vc_ref, vp_ref, o_ref, lse_ref, *, nblk):
    i = pl.program_id(2)
    blk_rows = BAND_BLOCK
    qi = lax.broadcasted_iota(jnp.int32, (blk_rows, 2 * blk_rows), 0)
    kj = lax.broadcasted_iota(jnp.int32, (blk_rows, 2 * blk_rows), 1)
    in_prev = kj < blk_rows
    bias = jnp.where(in_prev, jnp.where(kj >= qi, 0.0, NEG), jnp.where(kj - blk_rows <= qi, 0.0, NEG))
    bias0 = jnp.where(in_prev, jnp.where(i == 0, NEG, bias), bias)
    lane = lax.broadcasted_iota(jnp.int32, (1, LANES), 1)
    lo = lane < HEAD_DIM
    n_pair = WA // LANES
    for blk in range(nblk):
        rows = slice(blk * blk_rows, (blk + 1) * blk_rows)
        both = slice((blk - 1) * blk_rows, (blk + 1) * blk_rows)
        b = bias0 if blk == 0 else bias
        vals, ss = [], []
        for hp in range(n_pair):
            cols = slice(hp * LANES, (hp + 1) * LANES)
            qt = q_ref[rows, cols]
            if blk == 0:
                k2 = jnp.concatenate([kp_ref[:, cols], kc_ref[rows, cols]], axis=0)
                vals.append(jnp.concatenate([vp_ref[:, cols], vc_ref[rows, cols]], axis=0))
            else:
                k2 = kc_ref[both, cols]
                vals.append(vc_ref[both, cols])
            for h in range(2):
                qh = jnp.where(lo if h == 0 else jnp.logical_not(lo), qt, jnp.zeros_like(qt))
                ss.append(lax.dot_general(qh, k2, _NT, preferred_element_type=jnp.float32) + b)
        ms = [jnp.max(s, -1, keepdims=True) for s in ss]
        ps = [jnp.exp2(s - m) for s, m in zip(ss, ms)]
        dens = [jnp.sum(p, -1, keepdims=True) for p in ps]
        lse = jnp.zeros((blk_rows, LANES), jnp.float32)
        for hp in range(n_pair):
            cols = slice(hp * LANES, (hp + 1) * LANES)
            outs = []
            for h in range(2):
                n = 2 * hp + h
                o = jnp.dot(ps[n].astype(jnp.bfloat16), vals[hp], preferred_element_type=jnp.float32)
                outs.append(o * (1.0 / dens[n]))
                lse = jnp.where(lane // LSE_LANES == n, ms[n] + jnp.log2(dens[n]), lse)
            o_ref[rows, cols] = jnp.where(lo, outs[0], outs[1])
        lse_ref[rows, :] = lse


def _dilated(q, k, v, name):
    B, dil, L, W = q.shape
    rows = min(L, 4 * BAND_BLOCK)
    nblk = rows // BAND_BLOCK
    cur = pl.BlockSpec((None, None, rows, W), lambda b, r, i: (b, r, i, 0))
    cur_lse = pl.BlockSpec((None, None, rows, LANES), lambda b, r, i: (b, r, i, 0))
    prev = pl.BlockSpec((None, None, BAND_BLOCK, W), lambda b, r, i: (b, r, jnp.maximum(i * nblk - 1, 0), 0))
    return pl.pallas_call(
        functools.partial(_dil_kernel, nblk=nblk),
        grid=(B, dil, L // rows),
        in_specs=[cur, cur, prev, cur, prev],
        out_specs=[cur, cur_lse],
        out_shape=[jax.ShapeDtypeStruct((B, dil, L, W), jnp.float32),
                   jax.ShapeDtypeStruct((B, dil, L, LANES), jnp.float32)],
        compiler_params=_cparams(("parallel", "parallel", "arbitrary")),
        name=name,
    )(q, k, k, v, v)


def _cmp_kernel(ck_ref, cv_ref, pe_ref, wk1_ref, wk2_ref, wv1_ref, wv2t_ref, ok_ref, ovt_ref):
    n = ck_ref.shape[0] // CMP_STRIDE
    bf = jnp.bfloat16

    def hidden(c_ref, w1_ref):
        a = jnp.zeros((n, CMP_HIDDEN), jnp.float32)
        b = jnp.zeros((n, CMP_HIDDEN), jnp.float32)
        for j in range(CMP_STRIDE):
            t = c_ref[pl.ds(j, n, stride=CMP_STRIDE), :]
            a = a + jnp.dot((t + pe_ref[j:j + 1, :]).astype(bf), w1_ref[j], preferred_element_type=jnp.float32)
            b = b + jnp.dot((t + pe_ref[CMP_STRIDE + j:CMP_STRIDE + j + 1, :]).astype(bf), w1_ref[CMP_STRIDE + j],
                            preferred_element_type=jnp.float32)
        return _gelu_tanh(a + pltpu.roll(b, n - 1, 0)).astype(bf)

    ok_ref[...] = jnp.dot(hidden(ck_ref, wk1_ref), wk2_ref[...],
                          preferred_element_type=jnp.float32).astype(ok_ref.dtype)
    ovt_ref[...] = lax.dot_general(wv2t_ref[...], hidden(cv_ref, wv1_ref), _NT,
                                   preferred_element_type=jnp.float32).astype(ovt_ref.dtype)


def _compress(kc, vc, pe, w_ck1, w_ck2, w_cv1, w_cv2):
    B, S, _ = kc.shape
    n = S // CMP_STRIDE
    bf = jnp.bfloat16

    def slabs(w1):
        w = w1.reshape(CMP_LEN, HEAD_DIM, CMP_HIDDEN).astype(bf)
        z = jnp.zeros_like(w)
        return jnp.stack([jnp.concatenate([w, z], axis=1), jnp.concatenate([z, w], axis=1)], axis=0)

    seq = pl.BlockSpec((None, S, WG), lambda b, g: (b, 0, 0))
    full = lambda a: pl.BlockSpec(a.shape, lambda b, g: (0,) * a.ndim)
    slab = pl.BlockSpec((None, CMP_LEN, WG, CMP_HIDDEN), lambda b, g: (g, 0, 0, 0))
    pe2 = jnp.concatenate([pe, pe], axis=-1)
    args = (kc, vc, pe2, slabs(w_ck1), jnp.concatenate([w_ck2, w_ck2], axis=-1).astype(bf),
            slabs(w_cv1), w_cv2.T.astype(bf))
    return pl.pallas_call(
        _cmp_kernel,
        grid=(B, NSA_GROUPS),
        in_specs=[seq, seq, full(pe2), slab, full(args[4]), slab, full(args[6])],
        out_specs=[pl.BlockSpec((None, None, n, LANES), lambda b, g: (b, g, 0, 0)),
                   pl.BlockSpec((None, None, HEAD_DIM, n), lambda b, g: (b, g, 0, 0))],
        out_shape=[jax.ShapeDtypeStruct((B, NSA_GROUPS, n, LANES), bf),
                   jax.ShapeDtypeStruct((B, NSA_GROUPS, HEAD_DIM, n), bf)],
        compiler_params=_cparams(("parallel", "parallel")),
        name="compress",
    )(*args)


def _nsa_cmp_kernel(q_ref, kcc_ref, vcct_ref, ovt_ref, o_ref, sel_ref, *, tq, nsel, topk):
    qi = pl.program_id(2)
    t0 = qi * tq
    nc = kcc_ref.shape[0]
    tpos = t0 + lax.broadcasted_iota(jnp.int32, (nc, tq), 1)
    cend = lax.broadcasted_iota(jnp.int32, (nc, tq), 0) * CMP_STRIDE + (CMP_LEN - 1)
    cbias = jnp.where(cend <= tpos, 0.0, NEG)
    any_vis = (t0 + lax.broadcasted_iota(jnp.int32, (1, tq), 1)) >= CMP_LEN - 1
    kcc, vcct = kcc_ref[...], vcct_ref[...]
    ss = [lax.dot_general(kcc, q_ref[:, h * LANES:(h + 1) * LANES], _NT, preferred_element_type=jnp.float32) + cbias
          for h in range(NSA_HPG)]
    ms = [jnp.max(s, 0, keepdims=True) for s in ss]
    ps = [jnp.exp2(s - m).astype(jnp.bfloat16) for s, m in zip(ss, ms)]
    vaug = jnp.concatenate([vcct, jnp.ones((DEN_ROWS, nc), jnp.bfloat16)], axis=0)
    imp = jnp.zeros((LANES, tq), jnp.float32)
    for h in range(NSA_HPG):
        oa = jnp.dot(vaug, ps[h], preferred_element_type=jnp.float32)
        inv = jnp.where(any_vis, 1.0 / oa[HEAD_DIM:HEAD_DIM + 1], 0.0)
        o_ref[h * HEAD_DIM:(h + 1) * HEAD_DIM, :] = oa[:HEAD_DIM] * inv
        imp = imp + jnp.dot(ovt_ref[...], ps[h], preferred_element_type=jnp.float32) * inv
    blk = lax.broadcasted_iota(jnp.int32, (LANES, tq), 0)
    tq_pos = t0 + lax.broadcasted_iota(jnp.int32, (LANES, tq), 1)
    cur = tq_pos // SLC_BLOCK
    forced = (blk == cur) | (blk == cur - 1) | (blk == 0)
    valid = blk <= cur
    cand = valid & jnp.logical_not(forced) & (blk < nsel)
    score = jnp.where(cand, imp, -jnp.inf)
    blk_f = blk.astype(jnp.float32)
    for _ in range(topk - 3):
        mx = jnp.max(score, 0, keepdims=True)
        first = jnp.min(jnp.where(score == mx, blk_f, float(2 * LANES)), 0, keepdims=True)
        score = jnp.where(blk_f == first, -jnp.inf, score)
    sel = jnp.where(forced, 1.0, jnp.where(cand & (score == -jnp.inf), 1.0, 0.0))
    sel = jnp.where(valid, sel, 0.0)
    sel_ref[...] = ((sel.T - 1.0) * (-NEG)).astype(jnp.bfloat16)


def _overlap_t(S):
    nc = S // CMP_STRIDE
    ns = S // SLC_BLOCK
    cs = np.arange(nc) * CMP_STRIDE
    ss = np.arange(LANES) * SLC_BLOCK
    ov = np.clip(np.minimum(cs[None, :] + CMP_LEN, ss[:, None] + SLC_BLOCK)
                 - np.maximum(cs[None, :], ss[:, None]), 0, None).astype(np.float32) / CMP_LEN
    ov[ns:, :] = 0.0
    ov[:, nc - CMP_LEN // CMP_STRIDE + 1:] = 0.0
    return jnp.asarray(ov, jnp.bfloat16)


def _nsa_cmp(qbx, kcc, vcct, tq):
    B, S, _ = qbx.shape
    nc = kcc.shape[2]
    ns = S // SLC_BLOCK
    gw = NSA_HPG * LANES
    ovt = _overlap_t(S)
    return pl.pallas_call(
        functools.partial(_nsa_cmp_kernel, tq=tq, nsel=ns, topk=min(SLC_TOPK, ns)),
        grid=(B, NSA_GROUPS, S // tq),
        in_specs=[pl.BlockSpec((None, tq, gw), lambda b, g, i: (b, i, g)),
                  pl.BlockSpec((None, None, nc, LANES), lambda b, g, i: (b, g, 0, 0)),
                  pl.BlockSpec((None, None, HEAD_DIM, nc), lambda b, g, i: (b, g, 0, 0)),
                  pl.BlockSpec((LANES, nc), lambda b, g, i: (0, 0))],
        out_specs=[pl.BlockSpec((None, NSA_HPG * HEAD_DIM, tq), lambda b, g, i: (b, g, i)),
                   pl.BlockSpec((None, None, tq, LANES), lambda b, g, i: (b, g, i, 0))],
        out_shape=[jax.ShapeDtypeStruct((B, WB, S), jnp.float32),
                   jax.ShapeDtypeStruct((B, NSA_GROUPS, S, LANES), jnp.bfloat16)],
        compiler_params=_cparams(("parallel", "parallel", "parallel")),
        name="nsa_cmp",
    )(qbx, kcc, vcct, ovt)


def _nsa_main_kernel(q_ref, sel_ref, ksa_ref, vst_ref, kw_ref, vwt_ref, ocmp_ref, gate_ref, o_ref,
                     qa_scr, m_scr, acc_scr, win_scr, sa_scr, sb_scr, ma_scr, mb_scr, *, tq, tk):
    g = pl.program_id(1)
    qi = pl.program_id(2)
    t0 = qi * tq
    cols = NSA_HPG * tq
    bf = jnp.bfloat16
    selb = sel_ref[...]
    for h in range(NSA_HPG):
        qa_scr[h * tq:(h + 1) * tq, 0:LANES] = selb
        qa_scr[h * tq:(h + 1) * tq, LANES:2 * LANES] = q_ref[:, h * LANES:(h + 1) * LANES]
    m_scr[...] = jnp.full((1, cols), -1e37, jnp.float32)
    acc_scr[...] = jnp.zeros(acc_scr.shape, jnp.float32)

    def with_ones(vt):
        return jnp.concatenate([vt, jnp.ones((DEN_ROWS, vt.shape[1]), bf)], axis=0)

    def scores(tile, dst, dst_max):
        start = pl.multiple_of(tile * tk, tk)
        s = lax.dot_general(ksa_ref[pl.ds(start, tk), :], qa_scr[...], _NT,
                            preferred_element_type=jnp.float32)
        dst[...] = s
        dst_max[...] = jnp.max(s, 0, keepdims=True)

    def consume(tile, src, src_max):
        start = pl.multiple_of(tile * tk, tk)
        vt = with_ones(vst_ref[:, pl.ds(start, tk)])
        m_prev = m_scr[...]
        m_new = jnp.maximum(m_prev, src_max[...])
        alpha = jnp.exp2(m_prev - m_new)
        p = jnp.exp2(src[...] - m_new)
        acc_scr[...] = alpha * acc_scr[...] + jnp.dot(vt, p.astype(bf), preferred_element_type=jnp.float32)
        m_scr[...] = m_new

    last = t0 // tk
    scores(0, sa_scr, ma_scr)

    def window():
        wk = WIN + tq
        start = pl.multiple_of(jnp.maximum(t0 - WIN, 0), tq)
        kwin = kw_ref[pl.ds(start, wk), :]
        sw = lax.dot_general(kwin, qa_scr[:, LANES:2 * LANES], _NT, preferred_element_type=jnp.float32)
        dist = ((lax.broadcasted_iota(jnp.int32, (wk, tq), 1) + t0)
                - (lax.broadcasted_iota(jnp.int32, (wk, tq), 0) + start))
        wb = jnp.where((dist >= 0) & (dist < WIN), 0.0, NEG)
        sw = sw + jnp.concatenate([wb] * NSA_HPG, axis=1)
        pw = jnp.exp2(sw - jnp.max(sw, 0, keepdims=True))
        ow = jnp.dot(with_ones(vwt_ref[:, pl.ds(start, wk)]), pw.astype(bf), preferred_element_type=jnp.float32)
        win_scr[...] = ow[:HEAD_DIM] * (1.0 / ow[HEAD_DIM:HEAD_DIM + 1])

    def pair(first):
        scores(first + 1, sb_scr, mb_scr)
        consume(first, sa_scr, ma_scr)
        scores(first + 2, sa_scr, ma_scr)
        consume(first + 1, sb_scr, mb_scr)

    def quad_body(jj, carry):
        pair(4 * jj)
        pair(4 * jj + 2)
        return carry

    n_quad = last // 4
    lax.fori_loop(0, n_quad, quad_body, 0)

    @pl.when(last - 4 * n_quad >= 2)
    def _():
        pair(4 * n_quad)

    kk = lax.broadcasted_iota(jnp.int32, (tq, tq), 0)
    qq = lax.broadcasted_iota(jnp.int32, (tq, tq), 1)
    cb = jnp.where(kk <= qq, 0.0, NEG)
    cb = jnp.concatenate([cb] * NSA_HPG, axis=1)
    diag = pl.multiple_of(t0 - last * tk, tq)

    def finish(src):
        wk = WIN + tq
        wstart = pl.multiple_of(jnp.maximum(t0 - WIN, 0), tq)
        sw = lax.dot_general(kw_ref[pl.ds(wstart, wk), :], qa_scr[:, LANES:2 * LANES], _NT,
                             preferred_element_type=jnp.float32)
        dist = ((lax.broadcasted_iota(jnp.int32, (wk, tq), 1) + t0)
                - (lax.broadcasted_iota(jnp.int32, (wk, tq), 0) + wstart))
        wb = jnp.where((dist >= 0) & (dist < WIN), 0.0, NEG)
        sw = sw + jnp.concatenate([wb] * NSA_HPG, axis=1)
        src[pl.ds(diag, tq), :] = src[pl.ds(diag, tq), :] + cb
        start = pl.multiple_of(last * tk, tk)
        m_prev = m_scr[...]
        m_new = jnp.maximum(m_prev, jnp.max(src[...], 0, keepdims=True))
        mw = jnp.max(sw, 0, keepdims=True)
        alpha = jnp.exp2(m_prev - m_new)
        p = jnp.exp2(src[...] - m_new).astype(bf)
        pw = jnp.exp2(sw - mw).astype(bf)
        pv = jnp.dot(with_ones(vst_ref[:, pl.ds(start, tk)]), p, preferred_element_type=jnp.float32)
        ow = jnp.dot(with_ones(vwt_ref[:, pl.ds(wstart, wk)]), pw, preferred_element_type=jnp.float32)
        acc_scr[...] = alpha * acc_scr[...] + pv
        win_scr[...] = ow[:HEAD_DIM] * (1.0 / ow[HEAD_DIM:HEAD_DIM + 1])

    @pl.when(last % 2 == 1)
    def _():
        scores(last, sb_scr, mb_scr)
        consume(last - 1, sa_scr, ma_scr)
        finish(sb_scr)

    @pl.when(last % 2 == 0)
    def _():
        finish(sa_scr)

    o_slc = acc_scr[0:HEAD_DIM, :] * (1.0 / acc_scr[HEAD_DIM:HEAD_DIM + 1, :])
    o_win = win_scr[...]

    base = g * (NSA_HPG * 3)
    heads = []
    for h in range(NSA_HPG):
        c = slice(h * tq, (h + 1) * tq)
        g_cmp = gate_ref[pl.ds(base + 3 * h, 1), :]
        g_slc = gate_ref[pl.ds(base + 3 * h + 1, 1), :]
        g_win = gate_ref[pl.ds(base + 3 * h + 2, 1), :]
        heads.append(g_cmp * ocmp_ref[h * HEAD_DIM:(h + 1) * HEAD_DIM, :] + g_slc * o_slc[:, c] + g_win * o_win[:, c])
    o_ref[...] = jnp.concatenate(heads, axis=0).T.astype(o_ref.dtype)


def _nsa_main(qbx, selb, ksa, vst, kw, vwt, o_cmp_t, gates_t, tq, tk):
    B, S, _ = qbx.shape
    gw = NSA_HPG * LANES
    ow = NSA_HPG * HEAD_DIM
    cols = NSA_HPG * tq
    seq = lambda w: pl.BlockSpec((None, S, w), lambda b, g, i: (b, 0, 0))
    seqt = pl.BlockSpec((None, HEAD_DIM, S), lambda b, g, i: (b, g, 0))
    return pl.pallas_call(
        functools.partial(_nsa_main_kernel, tq=tq, tk=tk),
        grid=(B, NSA_GROUPS, S // tq),
        in_specs=[pl.BlockSpec((None, tq, gw), lambda b, g, i: (b, i, g)),
                  pl.BlockSpec((None, None, tq, LANES), lambda b, g, i: (b, g, i, 0)),
                  seq(2 * LANES), seqt, seq(WG), seqt,
                  pl.BlockSpec((None, ow, tq), lambda b, g, i: (b, g, i)),
                  pl.BlockSpec((None, LANES, tq), lambda b, g, i: (b, 0, i))],
        out_specs=pl.BlockSpec((None, tq, ow), lambda b, g, i: (b, i, g)),
        out_shape=jax.ShapeDtypeStruct((B, S, WB), jnp.bfloat16),
        scratch_shapes=[pltpu.VMEM((cols, 2 * LANES), jnp.bfloat16),
                        pltpu.VMEM((1, cols), jnp.float32),
                        pltpu.VMEM((HEAD_DIM + DEN_ROWS, cols), jnp.float32),
                        pltpu.VMEM((HEAD_DIM, cols), jnp.float32),
                        pltpu.VMEM((tk, cols), jnp.float32),
                        pltpu.VMEM((tk, cols), jnp.float32),
                        pltpu.VMEM((1, cols), jnp.float32),
                        pltpu.VMEM((1, cols), jnp.float32)],
        compiler_params=_cparams(("parallel", "parallel", "arbitrary")),
        name="nsa_main",
    )(qbx, selb, ksa, vst, kw, vwt, o_cmp_t, gates_t)


def _outproj_kernel(x_ref, o1_ref, l1_ref, o4_ref, l4_ref, o16_ref, l16_ref, ob_ref, mod_ref, wo_ref, g_ref, b_ref,
                    ex_ref, y_ref, oa_scr, s0_scr, s1_scr, s2_scr, s3_scr, *, tm):
    def token_order(src, cols, scr):
        dil = src.shape[0]
        for r in range(dil):
            scr[pl.ds(r, tm // dil, stride=dil), :] = src[r, :, cols]
        return scr[...]

    l1 = l1_ref[...]
    l2 = token_order(l4_ref, slice(None), s0_scr)
    l3 = token_order(l16_ref, slice(None), s1_scr)
    m = jnp.maximum(jnp.maximum(l1, l2), l3)
    es = [jnp.exp2(l1 - m), jnp.exp2(l2 - m), jnp.exp2(l3 - m)]
    inv = 1.0 / (es[0] + es[1] + es[2])

    def widen(w):
        hi = w.astype(jnp.bfloat16)
        lo = (w - hi.astype(jnp.float32)).astype(jnp.bfloat16)
        return jnp.dot(jnp.concatenate([hi, lo], axis=1), ex_ref[...], preferred_element_type=jnp.float32)

    w1, w2, w3 = [widen(e * inv) for e in es]
    for c in range(WA // LANES):
        cols = slice(c * LANES, (c + 1) * LANES)
        o2 = token_order(o4_ref, cols, s2_scr)
        o3 = token_order(o16_ref, cols, s3_scr)
        o_a = w1[:, cols] * o1_ref[:, cols] + w2[:, cols] * o2 + w3[:, cols] * o3
        oa_scr[:, cols] = o_a.astype(jnp.bfloat16)
    y = (jnp.dot(oa_scr[...], wo_ref[:WA, :], preferred_element_type=jnp.float32)
         + jnp.dot(ob_ref[...], wo_ref[WA:, :], preferred_element_type=jnp.float32))
    z = ALPHA * x_ref[...] + mod_ref[2:3, :] * y
    y_ref[...] = _ln(z) * g_ref[...] + b_ref[...]


def _outproj(x, pats, o_b, mod6, wo_bf16, ln_g, ln_b, tm):
    B, S, D = x.shape
    bs = lambda w: pl.BlockSpec((None, tm, w), lambda b, i: (b, i, 0))
    res = lambda d, w: pl.BlockSpec((None, d, tm // d, w), lambda b, i: (b, 0, i, 0))
    res1 = lambda w: pl.BlockSpec((None, None, tm, w), lambda b, i: (b, 0, i, 0))
    vec = pl.BlockSpec((1, D), lambda b, i: (0, 0))
    (o1, l1), (o4, l4), (o16, l16) = pats
    scr = pltpu.VMEM((tm, LANES), jnp.float32)
    expand = np.zeros((LANES, WA), np.float32)
    expand[(np.arange(WA) // HEAD_DIM) * LSE_LANES, np.arange(WA)] = 1.0
    expand = jnp.asarray(np.concatenate([expand, expand], axis=0), jnp.bfloat16)
    return pl.pallas_call(
        functools.partial(_outproj_kernel, tm=tm),
        grid=(B, S // tm),
        in_specs=[bs(D), res1(WA), res1(LANES), res(4, WA), res(4, LANES), res(16, WA), res(16, LANES), bs(WB),
                  pl.BlockSpec((None, 6, D), lambda b, i: (b, 0, 0)),
                  pl.BlockSpec((WA + WB, D), lambda b, i: (0, 0)), vec, vec,
                  pl.BlockSpec((2 * LANES, WA), lambda b, i: (0, 0))],
        out_specs=bs(D),
        out_shape=jax.ShapeDtypeStruct((B, S, D), jnp.float32),
        scratch_shapes=[pltpu.VMEM((tm, WA), jnp.bfloat16), scr, scr, scr, scr],
        compiler_params=_cparams(("parallel", "parallel")),
        name="outproj",
    )(x, o1, l1, o4, l4, o16, l16, o_b, mod6, wo_bf16,
      ln_g.reshape(1, D), ln_b.reshape(1, D), expand)


def _ffn_kernel(x_ref, mod_ref, wup_ref, cw_ref, cb_ref, wdn_ref, g_ref, b_ref, y_ref,
                u_scr, buf_scr, carry_scr, acc_scr, *, tm, fc):
    i = pl.program_id(1)
    F = wdn_ref.shape[0]
    pad = 8

    @pl.when(i == 0)
    def _():
        carry_scr[...] = jnp.zeros_like(carry_scr)

    th = tm // 2
    n_chunk = F // fc
    halves = (slice(0, th), slice(th, tm))

    def prologue(rows):
        u_scr[rows, :] = (_ln(x_ref[rows, :]) * (1.0 + mod_ref[4:5, :]) + mod_ref[3:4, :]).astype(jnp.bfloat16)

    def up(rows, c):
        u = u_scr[rows, :]
        return (jnp.dot(u, wup_ref[:, c * fc:(c + 1) * fc], preferred_element_type=jnp.float32),
                jnp.dot(u, wup_ref[:, F + c * fc:F + (c + 1) * fc], preferred_element_type=jnp.float32))

    def chunk(rows, c, a_gate, a_val):
        cols = slice(c * fc, (c + 1) * fc)
        buf_scr[0:pad, :] = carry_scr[:, cols]
        buf_scr[pad:pad + th, :] = a_gate
        carry_scr[:, cols] = a_gate[th - pad:th, :]
        conv = (cw_ref[0:1, cols] * buf_scr[pad - 2:pad - 2 + th, :]
                + cw_ref[1:2, cols] * buf_scr[pad - 1:pad - 1 + th, :]
                + cw_ref[2:3, cols] * a_gate + cb_ref[:, cols])
        h = (_gelu_tanh(conv) * a_val).astype(jnp.bfloat16)
        d = jnp.dot(h, wdn_ref[cols, :], preferred_element_type=jnp.float32)
        if c == 0:
            acc_scr[rows, :] = d
        else:
            acc_scr[rows, :] += d

    def epilogue(rows):
        z = ALPHA * x_ref[rows, :] + mod_ref[5:6, :] * acc_scr[rows, :]
        y_ref[rows, :] = _ln(z) * g_ref[...] + b_ref[...]

    items = [(rows, c) for rows in halves for c in range(n_chunk)]
    prologue(halves[0])
    nxt = up(*items[0])
    for k, (rows, c) in enumerate(items):
        cur = nxt
        if k == 0:
            prologue(halves[1])
        if k + 1 < len(items):
            nxt = up(*items[k + 1])
        if k == n_chunk + 1:
            epilogue(halves[0])
        chunk(rows, c, *cur)
    epilogue(halves[1])


def _ffn(x, mod6, wup_bf16, conv_w, conv_b, wdn_bf16, ln_g, ln_b, tm, fc):
    B, S, D = x.shape
    F = wdn_bf16.shape[0]
    bs = pl.BlockSpec((None, tm, D), lambda b, i: (b, i, 0))
    full = lambda a: pl.BlockSpec(a.shape, lambda b, i: (0,) * a.ndim)
    args = (wup_bf16, conv_w, conv_b.reshape(1, F), wdn_bf16, ln_g.reshape(1, D), ln_b.reshape(1, D))
    return pl.pallas_call(
        functools.partial(_ffn_kernel, tm=tm, fc=fc),
        grid=(B, S // tm),
        in_specs=[bs, pl.BlockSpec((None, 6, D), lambda b, i: (b, 0, 0))] + [full(a) for a in args],
        out_specs=bs,
        out_shape=jax.ShapeDtypeStruct((B, S, D), jnp.float32),
        scratch_shapes=[pltpu.VMEM((tm, D), jnp.bfloat16),
                        pltpu.VMEM((tm // 2 + 8, fc), jnp.float32),
                        pltpu.VMEM((8, F), jnp.float32),
                        pltpu.VMEM((tm, D), jnp.float32)],
        compiler_params=_cparams(("arbitrary", "arbitrary")),
        name="ffn",
    )(x, mod6, *args)


def kernel(x, c, w_ada, b_ada, w_in, pe_cmp, w_ck1, w_ck2, w_cv1, w_cv2, w_o, ln1_g, ln1_b, w_up, conv_w, conv_b,
           w_down, ln2_g, ln2_b):
    B, S, D = x.shape
    tables = _rope_tables(S)
    tm = min(512, S)
    tq = 128
    tk = min(512, S)
    for l in range(DEPTH):
        mod6 = _ada(c, w_ada[l], b_ada[l]).reshape(B, 6, D)
        w_nat, w_tr = _prep_w_in(w_in[l])
        (qa, ka, va, qbx, kc, vc, ksa, kw, vst, vwt, gates_t,
         q4, k4, v4, q16, k16, v16) = _inproj(x, mod6, w_nat, w_tr, tables, tm)
        pats = [_dilated(qa, ka, va, "dilated1"),
                _dilated(q4, k4, v4, "dilated4"), _dilated(q16, k16, v16, "dilated16")]
        kcc, vcct = _compress(kc, vc, pe_cmp[l], w_ck1[l], w_ck2[l], w_cv1[l], w_cv2[l])
        o_cmp_t, selb = _nsa_cmp(qbx, kcc, vcct, 4 * tq)
        o_b = _nsa_main(qbx, selb, ksa, vst, kw, vwt, o_cmp_t, gates_t, 2 * tq, tk)
        x = _outproj(x, pats, o_b, mod6, w_o[l].astype(jnp.bfloat16), ln1_g[l], ln1_b[l], tm)
        x = _ffn(x, mod6, w_up[l].astype(jnp.bfloat16), conv_w[l], conv_b[l], w_down[l].astype(jnp.bfloat16),
                 ln2_g[l], ln2_b[l], tm, 256)
    return x
```

```python
import functools
import math

import numpy as np
import jax
import jax.numpy as jnp
from jax import lax
from jax.experimental import pallas as pl
from jax.experimental.pallas import tpu as pltpu

HEAD_DIM = 64
N_HEADS_A = 8
N_HEADS_B = 8
DIL_PATTERNS = ((128, 1), (512, 4), (2048, 16))
BAND_BLOCK = 128
ROT_DIM = HEAD_DIM // 4
ROPE_THETA = 500000.0
NSA_GROUPS = 2
NSA_HPG = N_HEADS_B // NSA_GROUPS
CMP_LEN = 32
CMP_STRIDE = 16
CMP_HIDDEN = 4 * HEAD_DIM
SLC_BLOCK = 64
SLC_TOPK = 16
WIN = 512
D_FF = 2816
CONV_W = 3
DEPTH = 1
ALPHA = (2 * DEPTH) ** 0.25
LN_EPS = 1e-5
NEG = -1e30
LOG2E = math.log2(math.e)

LANES = 128
MXU_N = 256
DEN_ROWS = 16
LSE_LANES = LANES // N_HEADS_A
WA = N_HEADS_A * HEAD_DIM
WB = N_HEADS_B * HEAD_DIM
WG = NSA_GROUPS * HEAD_DIM
N_GATES = N_HEADS_B * 3
SRC_VS = 3 * WA + WB + 3 * WG
SRC_KW = SRC_VS + WG
SRC_VW = SRC_KW + WG
SRC_GL = SRC_VW + WG
OFF_QA, OFF_KA, OFF_VA, OFF_QB = 0, WA, 2 * WA, 3 * WA
OFF_KC = OFF_QB + WB
OFF_VC = OFF_KC + WG
OFF_KS = OFF_VC + WG
OFF_KW = OFF_KS + WG
NAT_WIDTH = OFF_KW + WG
TR_VS, TR_VW, TR_GL = 0, WG, 2 * WG
TR_WIDTH = 3 * WG

VMEM_LIMIT = 56 * 1024 * 1024

_NT = (((1,), (1,)), ((), ()))


def _cparams(sem):
    return pltpu.CompilerParams(dimension_semantics=sem, vmem_limit_bytes=VMEM_LIMIT)


def _gelu_tanh(x):
    return 0.5 * x * (1.0 + jnp.tanh(math.sqrt(2.0 / math.pi) * (x + 0.044715 * (x * x * x))))


def _ln(x):
    mu = jnp.mean(x, -1, keepdims=True)
    xc = x - mu
    var = jnp.mean(xc * xc, -1, keepdims=True)
    return xc * lax.rsqrt(var + LN_EPS)


def _ada_kernel(c_ref, w_ref, b_ref, o_ref):
    c = c_ref[...]
    a = c * jax.nn.sigmoid(c)
    o_ref[...] = jnp.dot(a, w_ref[...], preferred_element_type=jnp.float32) + b_ref[...]


def _ada(c, w, b):
    B, D = c.shape
    N = w.shape[1]
    tn = D
    return pl.pallas_call(
        _ada_kernel,
        grid=(N // tn,),
        in_specs=[pl.BlockSpec((B, D), lambda j: (0, 0)),
                  pl.BlockSpec((D, tn), lambda j: (0, j)),
                  pl.BlockSpec((1, tn), lambda j: (0, j))],
        out_specs=pl.BlockSpec((B, tn), lambda j: (0, j)),
        out_shape=jax.ShapeDtypeStruct((B, N), jnp.float32),
        compiler_params=_cparams(("arbitrary",)),
        name="adaln",
    )(c, w, b.reshape(1, N))


def _rope_tile(t, cos, s_lo, s_hi):
    half = ROT_DIM // 2
    return t * cos + pltpu.roll(t, half, 1) * s_hi + pltpu.roll(t, LANES - half, 1) * s_lo


def _inproj_kernel(x_ref, mod_ref, w_ref, wt_ref, cos_ref, slo_ref, shi_ref,
                   qa_ref, ka_ref, va_ref, qbx_ref, kc_ref, vc_ref, ksa_ref, kw_ref, vst_ref, vwt_ref, gate_ref,
                   q4_ref, k4_ref, v4_ref, q16_ref, k16_ref, v16_ref,
                   u_ref, ra_ref, rb_ref, *, tm):
    i = pl.program_id(1)
    x = x_ref[...]
    u = _ln(x) * (1.0 + mod_ref[1:2, :]) + mod_ref[0:1, :]
    u_ref[...] = u.astype(jnp.bfloat16)
    cos, slo, shi = cos_ref[...], slo_ref[...], shi_ref[...]
    lane = lax.broadcasted_iota(jnp.int32, (1, LANES), 1)
    bf = jnp.bfloat16

    def proj(off):
        r = jnp.dot(u_ref[...], w_ref[:, off:off + MXU_N], preferred_element_type=jnp.float32)
        return r[:, :LANES], r[:, LANES:]

    def rope(t):
        return _rope_tile(t, cos, slo, shi)

    def emit(t, cols, nat_ref, r4_ref, r16_ref):
        nat_ref[:, cols] = t.astype(bf)
        ra_ref[...] = t
        n4 = tm // 4
        for r in range(4):
            part = ra_ref[pl.ds(r, n4, stride=4), :]
            r4_ref[r, :, cols] = part.astype(bf)
            rb_ref[r * n4:(r + 1) * n4, :] = part
        for r in range(4):
            for m in range(4):
                part = rb_ref[pl.ds(r * n4 + m, n4 // 4, stride=4), :]
                r16_ref[r + 4 * m, :, cols] = part.astype(bf)

    for j in range(WA // MXU_N):
        for k, t in enumerate(proj(OFF_QA + j * MXU_N)):
            emit(rope(t) * LOG2E, slice((2 * j + k) * LANES, (2 * j + k + 1) * LANES), qa_ref, q4_ref, q16_ref)
        for k, t in enumerate(proj(OFF_KA + j * MXU_N)):
            emit(rope(t), slice((2 * j + k) * LANES, (2 * j + k + 1) * LANES), ka_ref, k4_ref, k16_ref)
        for k, t in enumerate(proj(OFF_VA + j * MXU_N)):
            emit(t, slice((2 * j + k) * LANES, (2 * j + k + 1) * LANES), va_ref, v4_ref, v16_ref)
    lo = lane < HEAD_DIM
    for j in range(WB // MXU_N):
        for k, t in enumerate(proj(OFF_QB + j * MXU_N)):
            pair = 2 * j + k
            t = rope(t) * LOG2E
            g = (2 * pair) // NSA_HPG
            tr = pltpu.roll(t, HEAD_DIM, 1)
            in_g = lo if g == 0 else jnp.logical_not(lo)
            even = jnp.where(in_g, t if g == 0 else tr, 0.0)
            odd = jnp.where(in_g, tr if g == 0 else t, 0.0)
            qbx_ref[:, (2 * pair) * LANES:(2 * pair + 1) * LANES] = even.astype(bf)
            qbx_ref[:, (2 * pair + 1) * LANES:(2 * pair + 2) * LANES] = odd.astype(bf)
    kc, vc = proj(OFF_KC)
    kc_ref[...] = rope(kc)
    vc_ref[...] = vc
    ks, kw = proj(OFF_KS)
    tpos = i * tm + lax.broadcasted_iota(jnp.int32, (tm, LANES), 0)
    blk = lax.broadcasted_iota(jnp.int32, (tm, LANES), 1)
    ksa_ref[:, 0:LANES] = jnp.where((tpos // SLC_BLOCK) == blk, 1.0, 0.0).astype(bf)
    ksa_ref[:, LANES:2 * LANES] = rope(ks).astype(bf)
    kw_ref[...] = rope(kw).astype(bf)
    tr_out = lax.dot_general(wt_ref[...], u_ref[...], _NT, preferred_element_type=jnp.float32)
    vst_ref[...] = tr_out[TR_VS:TR_VS + WG].astype(bf)
    vwt_ref[...] = tr_out[TR_VW:TR_VW + WG].astype(bf)
    gate_ref[...] = jax.nn.sigmoid(tr_out[TR_GL:TR_GL + LANES])


def _rope_tables(S):
    inv = 1.0 / (ROPE_THETA ** (jnp.arange(0, ROT_DIM, 2, dtype=jnp.float32) / ROT_DIM))
    ang = jnp.arange(S, dtype=jnp.float32)[:, None] * inv[None, :]
    cos, sin = jnp.cos(ang), jnp.sin(ang)
    half = ROT_DIM // 2
    d = np.arange(LANES) % HEAD_DIM
    idx = jnp.asarray(d % half)
    rot = jnp.asarray(d < ROT_DIM)
    lo = jnp.asarray(d < half)
    hi = jnp.asarray((d >= half) & (d < ROT_DIM))
    c_t = jnp.where(rot[None], cos[:, idx], 1.0)
    s_lo = jnp.where(lo[None], -sin[:, idx], 0.0)
    s_hi = jnp.where(hi[None], sin[:, idx], 0.0)
    return c_t, s_lo, s_hi


def _prep_w_in(w):
    q_scale = HEAD_DIM ** -0.5
    w_nat = jnp.concatenate([w[:, 0:WA] * q_scale, w[:, WA:3 * WA], w[:, 3 * WA:3 * WA + WB] * q_scale,
                             w[:, 3 * WA + WB:SRC_VS], w[:, SRC_KW:SRC_VW]], axis=1)
    w_tr = jnp.concatenate([w[:, SRC_VS:SRC_KW], w[:, SRC_VW:SRC_GL],
                            jnp.pad(w[:, SRC_GL:], ((0, 0), (0, LANES - N_GATES)))], axis=1)
    return w_nat.astype(jnp.bfloat16), w_tr.T.astype(jnp.bfloat16)


def _inproj(x, mod6, w_nat, w_tr, tables, tm):
    B, S, D = x.shape
    cos, slo, shi = tables
    bs = lambda w: pl.BlockSpec((None, tm, w), lambda b, i: (b, i, 0))
    bst = pl.BlockSpec((None, LANES, tm), lambda b, i: (b, 0, i))
    tab = pl.BlockSpec((tm, LANES), lambda b, i: (i, 0))
    sd = lambda w, dt: jax.ShapeDtypeStruct((B, S, w), dt)
    sdt = lambda dt: jax.ShapeDtypeStruct((B, LANES, S), dt)
    bf, f32 = jnp.bfloat16, jnp.float32
    res = lambda d: pl.BlockSpec((None, d, tm // d, WA), lambda b, i: (b, 0, i, 0))
    res1 = pl.BlockSpec((None, None, tm, WA), lambda b, i: (b, 0, i, 0))
    sdr = lambda d: jax.ShapeDtypeStruct((B, d, S // d, WA), bf)
    return pl.pallas_call(
        functools.partial(_inproj_kernel, tm=tm),
        grid=(B, S // tm),
        in_specs=[bs(D),
                  pl.BlockSpec((None, 6, D), lambda b, i: (b, 0, 0)),
                  pl.BlockSpec((D, NAT_WIDTH), lambda b, i: (0, 0)),
                  pl.BlockSpec((TR_WIDTH, D), lambda b, i: (0, 0)),
                  tab, tab, tab],
        out_specs=[res1, res1, res1, bs(2 * WB), bs(WG), bs(WG), bs(2 * LANES), bs(WG), bst, bst, bst]
                  + [res(4)] * 3 + [res(16)] * 3,
        out_shape=[sdr(1), sdr(1), sdr(1), sd(2 * WB, bf), sd(WG, f32), sd(WG, f32),
                   sd(2 * LANES, bf), sd(WG, bf), sdt(bf), sdt(bf), sdt(f32)]
                  + [sdr(4)] * 3 + [sdr(16)] * 3,
        scratch_shapes=[pltpu.VMEM((tm, D), jnp.bfloat16),
                        pltpu.VMEM((tm, LANES), jnp.float32),
                        pltpu.VMEM((tm, LANES), jnp.float32)],
        compiler_params=_cparams(("parallel", "parallel")),
        name="inproj",
    )(x, mod6, w_nat, w_tr, cos, slo, shi)


def _dil_kernel(q_ref, kc_ref, kp_ref, vc_ref, vp_ref, o_ref, lse_ref, *, nblk):
    i = pl.program_id(2)
    blk_rows = BAND_BLOCK
    qi = lax.broadcasted_iota(jnp.int32, (blk_rows, 2 * blk_rows), 0)
    kj = lax.broadcasted_iota(jnp.int32, (blk_rows, 2 * blk_rows), 1)
    in_prev = kj < blk_rows
    bias = jnp.where(in_prev, jnp.where(kj >= qi, 0.0, NEG), jnp.where(kj - blk_rows <= qi, 0.0, NEG))
    bias0 = jnp.where(in_prev, jnp.where(i == 0, NEG, bias), bias)
    lane = lax.broadcasted_iota(jnp.int32, (1, LANES), 1)
    lo = lane < HEAD_DIM
    n_pair = WA // LANES
    for blk in range(nblk):
        rows = slice(blk * blk_rows, (blk + 1) * blk_rows)
        both = slice((blk - 1) * blk_rows, (blk + 1) * blk_rows)
        b = bias0 if blk == 0 else bias
        vals, ss = [], []
        for hp in range(n_pair):
            cols = slice(hp * LANES, (hp + 1) * LANES)
            qt = q_ref[rows, cols]
            if blk == 0:
                k2 = jnp.concatenate([kp_ref[:, cols], kc_ref[rows, cols]], axis=0)
                vals.append(jnp.concatenate([vp_ref[:, cols], vc_ref[rows, cols]], axis=0))
            else:
                k2 = kc_ref[both, cols]
                vals.append(vc_ref[both, cols])
            for h in range(2):
                qh = jnp.where(lo if h == 0 else jnp.logical_not(lo), qt, jnp.zeros_like(qt))
                ss.append(lax.dot_general(qh, k2, _NT, preferred_element_type=jnp.float32) + b)
        ms = [jnp.max(s, -1, keepdims=True) for s in ss]
        ps = [jnp.exp2(s - m) for s, m in zip(ss, ms)]
        dens = [jnp.sum(p, -1, keepdims=True) for p in ps]
        lse = jnp.zeros((blk_rows, LANES), jnp.float32)
        for hp in range(n_pair):
            cols = slice(hp * LANES, (hp + 1) * LANES)
            outs = []
            for h in range(2):
                n = 2 * hp + h
                o = jnp.dot(ps[n].astype(jnp.bfloat16), vals[hp], preferred_element_type=jnp.float32)
                outs.append(o * (1.0 / dens[n]))
                lse = jnp.where(lane // LSE_LANES == n, ms[n] + jnp.log2(dens[n]), lse)
            o_ref[rows, cols] = jnp.where(lo, outs[0], outs[1])
        lse_ref[rows, :] = lse


def _dilated(q, k, v, name):
    B, dil, L, W = q.shape
    rows = min(L, 4 * BAND_BLOCK)
    nblk = rows // BAND_BLOCK
    cur = pl.BlockSpec((None, None, rows, W), lambda b, r, i: (b, r, i, 0))
    cur_lse = pl.BlockSpec((None, None, rows, LANES), lambda b, r, i: (b, r, i, 0))
    prev = pl.BlockSpec((None, None, BAND_BLOCK, W), lambda b, r, i: (b, r, jnp.maximum(i * nblk - 1, 0), 0))
    return pl.pallas_call(
        functools.partial(_dil_kernel, nblk=nblk),
        grid=(B, dil, L // rows),
        in_specs=[cur, cur, prev, cur, prev],
        out_specs=[cur, cur_lse],
        out_shape=[jax.ShapeDtypeStruct((B, dil, L, W), jnp.float32),
                   jax.ShapeDtypeStruct((B, dil, L, LANES), jnp.float32)],
        compiler_params=_cparams(("parallel", "parallel", "arbitrary")),
        name=name,
    )(q, k, k, v, v)


def _cmp_kernel(ck_ref, cv_ref, pe_ref, wk1_ref, wk2_ref, wv1_ref, wv2t_ref, ok_ref, ovt_ref):
    n = ck_ref.shape[0] // CMP_STRIDE
    bf = jnp.bfloat16

    def hidden(c_ref, w1_ref):
        a = jnp.zeros((n, CMP_HIDDEN), jnp.float32)
        b = jnp.zeros((n, CMP_HIDDEN), jnp.float32)
        for j in range(CMP_STRIDE):
            t = c_ref[pl.ds(j, n, stride=CMP_STRIDE), :]
            a = a + jnp.dot((t + pe_ref[j:j + 1, :]).astype(bf), w1_ref[j], preferred_element_type=jnp.float32)
            b = b + jnp.dot((t + pe_ref[CMP_STRIDE + j:CMP_STRIDE + j + 1, :]).astype(bf), w1_ref[CMP_STRIDE + j],
                            preferred_element_type=jnp.float32)
        return _gelu_tanh(a + pltpu.roll(b, n - 1, 0)).astype(bf)

    ok_ref[...] = jnp.dot(hidden(ck_ref, wk1_ref), wk2_ref[...],
                          preferred_element_type=jnp.float32).astype(ok_ref.dtype)
    ovt_ref[...] = lax.dot_general(wv2t_ref[...], hidden(cv_ref, wv1_ref), _NT,
                                   preferred_element_type=jnp.float32).astype(ovt_ref.dtype)


def _compress(kc, vc, pe, w_ck1, w_ck2, w_cv1, w_cv2):
    B, S, _ = kc.shape
    n = S // CMP_STRIDE
    bf = jnp.bfloat16

    def slabs(w1):
        w = w1.reshape(CMP_LEN, HEAD_DIM, CMP_HIDDEN).astype(bf)
        z = jnp.zeros_like(w)
        return jnp.stack([jnp.concatenate([w, z], axis=1), jnp.concatenate([z, w], axis=1)], axis=0)

    seq = pl.BlockSpec((None, S, WG), lambda b, g: (b, 0, 0))
    full = lambda a: pl.BlockSpec(a.shape, lambda b, g: (0,) * a.ndim)
    slab = pl.BlockSpec((None, CMP_LEN, WG, CMP_HIDDEN), lambda b, g: (g, 0, 0, 0))
    pe2 = jnp.concatenate([pe, pe], axis=-1)
    args = (kc, vc, pe2, slabs(w_ck1), jnp.concatenate([w_ck2, w_ck2], axis=-1).astype(bf),
            slabs(w_cv1), w_cv2.T.astype(bf))
    return pl.pallas_call(
        _cmp_kernel,
        grid=(B, NSA_GROUPS),
        in_specs=[seq, seq, full(pe2), slab, full(args[4]), slab, full(args[6])],
        out_specs=[pl.BlockSpec((None, None, n, LANES), lambda b, g: (b, g, 0, 0)),
                   pl.BlockSpec((None, None, HEAD_DIM, n), lambda b, g: (b, g, 0, 0))],
        out_shape=[jax.ShapeDtypeStruct((B, NSA_GROUPS, n, LANES), bf),
                   jax.ShapeDtypeStruct((B, NSA_GROUPS, HEAD_DIM, n), bf)],
        compiler_params=_cparams(("parallel", "parallel")),
        name="compress",
    )(*args)


def _nsa_cmp_kernel(q_ref, kcc_ref, vcct_ref, ovt_ref, o_ref, sel_ref, *, tq, nsel, topk):
    qi = pl.program_id(2)
    t0 = qi * tq
    nc = kcc_ref.shape[0]
    tpos = t0 + lax.broadcasted_iota(jnp.int32, (nc, tq), 1)
    cend = lax.broadcasted_iota(jnp.int32, (nc, tq), 0) * CMP_STRIDE + (CMP_LEN - 1)
    cbias = jnp.where(cend <= tpos, 0.0, NEG)
    any_vis = (t0 + lax.broadcasted_iota(jnp.int32, (1, tq), 1)) >= CMP_LEN - 1
    kcc, vcct = kcc_ref[...], vcct_ref[...]
    ss = [lax.dot_general(kcc, q_ref[:, h * LANES:(h + 1) * LANES], _NT, preferred_element_type=jnp.float32) + cbias
          for h in range(NSA_HPG)]
    ms = [jnp.max(s, 0, keepdims=True) for s in ss]
    ps = [jnp.exp2(s - m).astype(jnp.bfloat16) for s, m in zip(ss, ms)]
    vaug = jnp.concatenate([vcct, jnp.ones((DEN_ROWS, nc), jnp.bfloat16)], axis=0)
    imp = jnp.zeros((LANES, tq), jnp.float32)
    for h in range(NSA_HPG):
        oa = jnp.dot(vaug, ps[h], preferred_element_type=jnp.float32)
        inv = jnp.where(any_vis, 1.0 / oa[HEAD_DIM:HEAD_DIM + 1], 0.0)
        o_ref[h * HEAD_DIM:(h + 1) * HEAD_DIM, :] = oa[:HEAD_DIM] * inv
        imp = imp + jnp.dot(ovt_ref[...], ps[h], preferred_element_type=jnp.float32) * inv
    blk = lax.broadcasted_iota(jnp.int32, (LANES, tq), 0)
    tq_pos = t0 + lax.broadcasted_iota(jnp.int32, (LANES, tq), 1)
    cur = tq_pos // SLC_BLOCK
    forced = (blk == cur) | (blk == cur - 1) | (blk == 0)
    valid = blk <= cur
    cand = valid & jnp.logical_not(forced) & (blk < nsel)
    score = jnp.where(cand, imp, -jnp.inf)
    blk_f = blk.astype(jnp.float32)
    for _ in range(topk - 3):
        mx = jnp.max(score, 0, keepdims=True)
        first = jnp.min(jnp.where(score == mx, blk_f, float(2 * LANES)), 0, keepdims=True)
        score = jnp.where(blk_f == first, -jnp.inf, score)
    sel = jnp.where(forced, 1.0, jnp.where(cand & (score == -jnp.inf), 1.0, 0.0))
    sel = jnp.where(valid, sel, 0.0)
    sel_ref[...] = ((sel.T - 1.0) * (-NEG)).astype(jnp.bfloat16)


def _overlap_t(S):
    nc = S // CMP_STRIDE
    ns = S // SLC_BLOCK
    cs = np.arange(nc) * CMP_STRIDE
    ss = np.arange(LANES) * SLC_BLOCK
    ov = np.clip(np.minimum(cs[None, :] + CMP_LEN, ss[:, None] + SLC_BLOCK)
                 - np.maximum(cs[None, :], ss[:, None]), 0, None).astype(np.float32) / CMP_LEN
    ov[ns:, :] = 0.0
    ov[:, nc - CMP_LEN // CMP_STRIDE + 1:] = 0.0
    return jnp.asarray(ov, jnp.bfloat16)


def _nsa_cmp(qbx, kcc, vcct, tq):
    B, S, _ = qbx.shape
    nc = kcc.shape[2]
    ns = S // SLC_BLOCK
    gw = NSA_HPG * LANES
    ovt = _overlap_t(S)
    return pl.pallas_call(
        functools.partial(_nsa_cmp_kernel, tq=tq, nsel=ns, topk=min(SLC_TOPK, ns)),
        grid=(B, NSA_GROUPS, S // tq),
        in_specs=[pl.BlockSpec((None, tq, gw), lambda b, g, i: (b, i, g)),
                  pl.BlockSpec((None, None, nc, LANES), lambda b, g, i: (b, g, 0, 0)),
                  pl.BlockSpec((None, None, HEAD_DIM, nc), lambda b, g, i: (b, g, 0, 0)),
                  pl.BlockSpec((LANES, nc), lambda b, g, i: (0, 0))],
        out_specs=[pl.BlockSpec((None, NSA_HPG * HEAD_DIM, tq), lambda b, g, i: (b, g, i)),
                   pl.BlockSpec((None, None, tq, LANES), lambda b, g, i: (b, g, i, 0))],
        out_shape=[jax.ShapeDtypeStruct((B, WB, S), jnp.float32),
                   jax.ShapeDtypeStruct((B, NSA_GROUPS, S, LANES), jnp.bfloat16)],
        compiler_params=_cparams(("parallel", "parallel", "parallel")),
        name="nsa_cmp",
    )(qbx, kcc, vcct, ovt)


def _nsa_main_kernel(q_ref, sel_ref, ksa_ref, vst_ref, kw_ref, vwt_ref, ocmp_ref, gate_ref, o_ref,
                     qa_scr, m_scr, acc_scr, win_scr, sa_scr, sb_scr, ma_scr, mb_scr, *, tq, tk):
    g = pl.program_id(1)
    qi = pl.program_id(2)
    t0 = qi * tq
    cols = NSA_HPG * tq
    bf = jnp.bfloat16
    selb = sel_ref[...]
    for h in range(NSA_HPG):
        qa_scr[h * tq:(h + 1) * tq, 0:LANES] = selb
        qa_scr[h * tq:(h + 1) * tq, LANES:2 * LANES] = q_ref[:, h * LANES:(h + 1) * LANES]
    m_scr[...] = jnp.full((1, cols), -1e37, jnp.float32)
    acc_scr[...] = jnp.zeros(acc_scr.shape, jnp.float32)

    def with_ones(vt):
        return jnp.concatenate([vt, jnp.ones((DEN_ROWS, vt.shape[1]), bf)], axis=0)

    def scores(tile, dst, dst_max):
        start = pl.multiple_of(tile * tk, tk)
        s = lax.dot_general(ksa_ref[pl.ds(start, tk), :], qa_scr[...], _NT,
                            preferred_element_type=jnp.float32)
        dst[...] = s
        dst_max[...] = jnp.max(s, 0, keepdims=True)

    def consume(tile, src, src_max):
        start = pl.multiple_of(tile * tk, tk)
        vt = with_ones(vst_ref[:, pl.ds(start, tk)])
        m_prev = m_scr[...]
        m_new = jnp.maximum(m_prev, src_max[...])
        alpha = jnp.exp2(m_prev - m_new)
        p = jnp.exp2(src[...] - m_new)
        acc_scr[...] = alpha * acc_scr[...] + jnp.dot(vt, p.astype(bf), preferred_element_type=jnp.float32)
        m_scr[...] = m_new

    last = t0 // tk

    wk = WIN + tq
    wstart = pl.multiple_of(jnp.maximum(t0 - WIN, 0), tq)
    sw = lax.dot_general(kw_ref[pl.ds(wstart, wk), :], qa_scr[:, LANES:2 * LANES], _NT,
                         preferred_element_type=jnp.float32)
    dist = ((lax.broadcasted_iota(jnp.int32, (wk, tq), 1) + t0)
            - (lax.broadcasted_iota(jnp.int32, (wk, tq), 0) + wstart))
    wb = jnp.where((dist >= 0) & (dist < WIN), 0.0, NEG)
    sw = sw + jnp.concatenate([wb] * NSA_HPG, axis=1)
    mw = jnp.max(sw, 0, keepdims=True)
    scores(0, sa_scr, ma_scr)
    pw = jnp.exp2(sw - mw).astype(bf)
    ow = jnp.dot(with_ones(vwt_ref[:, pl.ds(wstart, wk)]), pw, preferred_element_type=jnp.float32)
    win_scr[...] = ow[:HEAD_DIM] * (1.0 / ow[HEAD_DIM:HEAD_DIM + 1])

    def pair(first):
        scores(first + 1, sb_scr, mb_scr)
        consume(first, sa_scr, ma_scr)
        scores(first + 2, sa_scr, ma_scr)
        consume(first + 1, sb_scr, mb_scr)

    def quad_body(jj, carry):
        pair(4 * jj)
        pair(4 * jj + 2)
        return carry

    n_quad = last // 4
    lax.fori_loop(0, n_quad, quad_body, 0)

    @pl.when(last - 4 * n_quad >= 2)
    def _():
        pair(4 * n_quad)

    kk = lax.broadcasted_iota(jnp.int32, (tq, tq), 0)
    qq = lax.broadcasted_iota(jnp.int32, (tq, tq), 1)
    cb = jnp.where(kk <= qq, 0.0, NEG)
    cb = jnp.concatenate([cb] * NSA_HPG, axis=1)
    diag = pl.multiple_of(t0 - last * tk, tq)

    def finish(src):
        src[pl.ds(diag, tq), :] = src[pl.ds(diag, tq), :] + cb
        start = pl.multiple_of(last * tk, tk)
        m_prev = m_scr[...]
        m_new = jnp.maximum(m_prev, jnp.max(src[...], 0, keepdims=True))
        alpha = jnp.exp2(m_prev - m_new)
        p = jnp.exp2(src[...] - m_new).astype(bf)
        pv = jnp.dot(with_ones(vst_ref[:, pl.ds(start, tk)]), p, preferred_element_type=jnp.float32)
        acc_scr[...] = alpha * acc_scr[...] + pv

    @pl.when(last % 2 == 1)
    def _():
        scores(last, sb_scr, mb_scr)
        consume(last - 1, sa_scr, ma_scr)
        finish(sb_scr)

    @pl.when(last % 2 == 0)
    def _():
        finish(sa_scr)

    o_slc = acc_scr[0:HEAD_DIM, :] * (1.0 / acc_scr[HEAD_DIM:HEAD_DIM + 1, :])
    o_win = win_scr[...]

    base = g * (NSA_HPG * 3)
    heads = []
    for h in range(NSA_HPG):
        c = slice(h * tq, (h + 1) * tq)
        g_cmp = gate_ref[pl.ds(base + 3 * h, 1), :]
        g_slc = gate_ref[pl.ds(base + 3 * h + 1, 1), :]
        g_win = gate_ref[pl.ds(base + 3 * h + 2, 1), :]
        heads.append(g_cmp * ocmp_ref[h * HEAD_DIM:(h + 1) * HEAD_DIM, :] + g_slc * o_slc[:, c] + g_win * o_win[:, c])
    o_ref[...] = jnp.concatenate(heads, axis=0).T.astype(o_ref.dtype)


def _nsa_main(qbx, selb, ksa, vst, kw, vwt, o_cmp_t, gates_t, tq, tk):
    B, S, _ = qbx.shape
    gw = NSA_HPG * LANES
    ow = NSA_HPG * HEAD_DIM
    cols = NSA_HPG * tq
    seq = lambda w: pl.BlockSpec((None, S, w), lambda b, g, i: (b, 0, 0))
    seqt = pl.BlockSpec((None, HEAD_DIM, S), lambda b, g, i: (b, g, 0))
    return pl.pallas_call(
        functools.partial(_nsa_main_kernel, tq=tq, tk=tk),
        grid=(B, NSA_GROUPS, S // tq),
        in_specs=[pl.BlockSpec((None, tq, gw), lambda b, g, i: (b, i, g)),
                  pl.BlockSpec((None, None, tq, LANES), lambda b, g, i: (b, g, i, 0)),
                  seq(2 * LANES), seqt, seq(WG), seqt,
                  pl.BlockSpec((None, ow, tq), lambda b, g, i: (b, g, i)),
                  pl.BlockSpec((None, LANES, tq), lambda b, g, i: (b, 0, i))],
        out_specs=pl.BlockSpec((None, tq, ow), lambda b, g, i: (b, i, g)),
        out_shape=jax.ShapeDtypeStruct((B, S, WB), jnp.bfloat16),
        scratch_shapes=[pltpu.VMEM((cols, 2 * LANES), jnp.bfloat16),
                        pltpu.VMEM((1, cols), jnp.float32),
                        pltpu.VMEM((HEAD_DIM + DEN_ROWS, cols), jnp.float32),
                        pltpu.VMEM((HEAD_DIM, cols), jnp.float32),
                        pltpu.VMEM((tk, cols), jnp.float32),
                        pltpu.VMEM((tk, cols), jnp.float32),
                        pltpu.VMEM((1, cols), jnp.float32),
                        pltpu.VMEM((1, cols), jnp.float32)],
        compiler_params=_cparams(("parallel", "parallel", "arbitrary")),
        name="nsa_main",
    )(qbx, selb, ksa, vst, kw, vwt, o_cmp_t, gates_t)


def _outproj_kernel(x_ref, o1_ref, l1_ref, o4_ref, l4_ref, o16_ref, l16_ref, ob_ref, mod_ref, wo_ref, g_ref, b_ref,
                    ex_ref, y_ref, oa_scr, s0_scr, s1_scr, s2_scr, s3_scr, *, tm):
    def token_order(src, cols, scr):
        dil = src.shape[0]
        for r in range(dil):
            scr[pl.ds(r, tm // dil, stride=dil), :] = src[r, :, cols]
        return scr[...]

    l1 = l1_ref[...]
    l2 = token_order(l4_ref, slice(None), s0_scr)
    l3 = token_order(l16_ref, slice(None), s1_scr)
    m = jnp.maximum(jnp.maximum(l1, l2), l3)
    es = [jnp.exp2(l1 - m), jnp.exp2(l2 - m), jnp.exp2(l3 - m)]
    inv = 1.0 / (es[0] + es[1] + es[2])

    def widen(w):
        hi = w.astype(jnp.bfloat16)
        lo = (w - hi.astype(jnp.float32)).astype(jnp.bfloat16)
        return jnp.dot(jnp.concatenate([hi, lo], axis=1), ex_ref[...], preferred_element_type=jnp.float32)

    w1, w2, w3 = [widen(e * inv) for e in es]
    for c in range(WA // LANES):
        cols = slice(c * LANES, (c + 1) * LANES)
        o2 = token_order(o4_ref, cols, s2_scr)
        o3 = token_order(o16_ref, cols, s3_scr)
        o_a = w1[:, cols] * o1_ref[:, cols] + w2[:, cols] * o2 + w3[:, cols] * o3
        oa_scr[:, cols] = o_a.astype(jnp.bfloat16)
    y = (jnp.dot(oa_scr[...], wo_ref[:WA, :], preferred_element_type=jnp.float32)
         + jnp.dot(ob_ref[...], wo_ref[WA:, :], preferred_element_type=jnp.float32))
    z = ALPHA * x_ref[...] + mod_ref[2:3, :] * y
    y_ref[...] = _ln(z) * g_ref[...] + b_ref[...]


def _outproj(x, pats, o_b, mod6, wo_bf16, ln_g, ln_b, tm):
    B, S, D = x.shape
    bs = lambda w: pl.BlockSpec((None, tm, w), lambda b, i: (b, i, 0))
    res = lambda d, w: pl.BlockSpec((None, d, tm // d, w), lambda b, i: (b, 0, i, 0))
    res1 = lambda w: pl.BlockSpec((None, None, tm, w), lambda b, i: (b, 0, i, 0))
    vec = pl.BlockSpec((1, D), lambda b, i: (0, 0))
    (o1, l1), (o4, l4), (o16, l16) = pats
    scr = pltpu.VMEM((tm, LANES), jnp.float32)
    expand = np.zeros((LANES, WA), np.float32)
    expand[(np.arange(WA) // HEAD_DIM) * LSE_LANES, np.arange(WA)] = 1.0
    expand = jnp.asarray(np.concatenate([expand, expand], axis=0), jnp.bfloat16)
    return pl.pallas_call(
        functools.partial(_outproj_kernel, tm=tm),
        grid=(B, S // tm),
        in_specs=[bs(D), res1(WA), res1(LANES), res(4, WA), res(4, LANES), res(16, WA), res(16, LANES), bs(WB),
                  pl.BlockSpec((None, 6, D), lambda b, i: (b, 0, 0)),
                  pl.BlockSpec((WA + WB, D), lambda b, i: (0, 0)), vec, vec,
                  pl.BlockSpec((2 * LANES, WA), lambda b, i: (0, 0))],
        out_specs=bs(D),
        out_shape=jax.ShapeDtypeStruct((B, S, D), jnp.float32),
        scratch_shapes=[pltpu.VMEM((tm, WA), jnp.bfloat16), scr, scr, scr, scr],
        compiler_params=_cparams(("parallel", "parallel")),
        name="outproj",
    )(x, o1, l1, o4, l4, o16, l16, o_b, mod6, wo_bf16,
      ln_g.reshape(1, D), ln_b.reshape(1, D), expand)


def _ffn_kernel(x_ref, mod_ref, wup_ref, cw_ref, cb_ref, wdn_ref, g_ref, b_ref, y_ref,
                u_scr, buf_scr, carry_scr, acc_scr, *, tm, fc):
    i = pl.program_id(1)
    F = wdn_ref.shape[0]
    pad = 8

    @pl.when(i == 0)
    def _():
        carry_scr[...] = jnp.zeros_like(carry_scr)

    th = tm // 2
    n_chunk = F // fc
    halves = (slice(0, th), slice(th, tm))

    def prologue(rows):
        u_scr[rows, :] = (_ln(x_ref[rows, :]) * (1.0 + mod_ref[4:5, :]) + mod_ref[3:4, :]).astype(jnp.bfloat16)

    def up(rows, c):
        u = u_scr[rows, :]
        return (jnp.dot(u, wup_ref[:, c * fc:(c + 1) * fc], preferred_element_type=jnp.float32),
                jnp.dot(u, wup_ref[:, F + c * fc:F + (c + 1) * fc], preferred_element_type=jnp.float32))

    def chunk(rows, c, a_gate, a_val):
        cols = slice(c * fc, (c + 1) * fc)
        buf_scr[0:pad, :] = carry_scr[:, cols]
        buf_scr[pad:pad + th, :] = a_gate
        carry_scr[:, cols] = a_gate[th - pad:th, :]
        conv = (cw_ref[0:1, cols] * buf_scr[pad - 2:pad - 2 + th, :]
                + cw_ref[1:2, cols] * buf_scr[pad - 1:pad - 1 + th, :]
                + cw_ref[2:3, cols] * a_gate + cb_ref[:, cols])
        h = (_gelu_tanh(conv) * a_val).astype(jnp.bfloat16)
        d = jnp.dot(h, wdn_ref[cols, :], preferred_element_type=jnp.float32)
        if c == 0:
            acc_scr[rows, :] = d
        else:
            acc_scr[rows, :] += d

    def epilogue(rows):
        z = ALPHA * x_ref[rows, :] + mod_ref[5:6, :] * acc_scr[rows, :]
        y_ref[rows, :] = _ln(z) * g_ref[...] + b_ref[...]

    items = [(rows, c) for rows in halves for c in range(n_chunk)]
    prologue(halves[0])
    nxt = up(*items[0])
    for k, (rows, c) in enumerate(items):
        cur = nxt
        if k == 0:
            prologue(halves[1])
        if k + 1 < len(items):
            nxt = up(*items[k + 1])
        if k == n_chunk + 1:
            epilogue(halves[0])
        chunk(rows, c, *cur)
    epilogue(halves[1])


def _ffn(x, mod6, wup_bf16, conv_w, conv_b, wdn_bf16, ln_g, ln_b, tm, fc):
    B, S, D = x.shape
    F = wdn_bf16.shape[0]
    bs = pl.BlockSpec((None, tm, D), lambda b, i: (b, i, 0))
    full = lambda a: pl.BlockSpec(a.shape, lambda b, i: (0,) * a.ndim)
    args = (wup_bf16, conv_w, conv_b.reshape(1, F), wdn_bf16, ln_g.reshape(1, D), ln_b.reshape(1, D))
    return pl.pallas_call(
        functools.partial(_ffn_kernel, tm=tm, fc=fc),
        grid=(B, S // tm),
        in_specs=[bs, pl.BlockSpec((None, 6, D), lambda b, i: (b, 0, 0))] + [full(a) for a in args],
        out_specs=bs,
        out_shape=jax.ShapeDtypeStruct((B, S, D), jnp.float32),
        scratch_shapes=[pltpu.VMEM((tm, D), jnp.bfloat16),
                        pltpu.VMEM((tm // 2 + 8, fc), jnp.float32),
                        pltpu.VMEM((8, F), jnp.float32),
                        pltpu.VMEM((tm, D), jnp.float32)],
        compiler_params=_cparams(("arbitrary", "arbitrary")),
        name="ffn",
    )(x, mod6, *args)


def kernel(x, c, w_ada, b_ada, w_in, pe_cmp, w_ck1, w_ck2, w_cv1, w_cv2, w_o, ln1_g, ln1_b, w_up, conv_w, conv_b,
           w_down, ln2_g, ln2_b):
    B, S, D = x.shape
    tables = _rope_tables(S)
    tm = min(512, S)
    tq = 128
    tk = min(512, S)
    for l in range(DEPTH):
        mod6 = _ada(c, w_ada[l], b_ada[l]).reshape(B, 6, D)
        w_nat, w_tr = _prep_w_in(w_in[l])
        (qa, ka, va, qbx, kc, vc, ksa, kw, vst, vwt, gates_t,
         q4, k4, v4, q16, k16, v16) = _inproj(x, mod6, w_nat, w_tr, tables, tm)
        pats = [_dilated(qa, ka, va, "dilated1"),
                _dilated(q4, k4, v4, "dilated4"), _dilated(q16, k16, v16, "dilated16")]
        kcc, vcct = _compress(kc, vc, pe_cmp[l], w_ck1[l], w_ck2[l], w_cv1[l], w_cv2[l])
        o_cmp_t, selb = _nsa_cmp(qbx, kcc, vcct, 4 * tq)
        o_b = _nsa_main(qbx, selb, ksa, vst, kw, vwt, o_cmp_t, gates_t, 2 * tq, tk)
        x = _outproj(x, pats, o_b, mod6, w_o[l].astype(jnp.bfloat16), ln1_g[l], ln1_b[l], tm)
        x = _ffn(x, mod6, w_up[l].astype(jnp.bfloat16), conv_w[l], conv_b[l], w_down[l].astype(jnp.bfloat16),
                 ln2_g[l], ln2_b[l], tm, 256)
    return x
```

```python
import functools
import math

import numpy as np
import jax
import jax.numpy as jnp
from jax import lax
from jax.experimental import pallas as pl
from jax.experimental.pallas import tpu as pltpu

HEAD_DIM = 64
N_HEADS_A = 8
N_HEADS_B = 8
DIL_PATTERNS = ((128, 1), (512, 4), (2048, 16))
BAND_BLOCK = 128
ROT_DIM = HEAD_DIM // 4
ROPE_THETA = 500000.0
NSA_GROUPS = 2
NSA_HPG = N_HEADS_B // NSA_GROUPS
CMP_LEN = 32
CMP_STRIDE = 16
CMP_HIDDEN = 4 * HEAD_DIM
SLC_BLOCK = 64
SLC_TOPK = 16
WIN = 512
D_FF = 2816
CONV_W = 3
DEPTH = 1
ALPHA = (2 * DEPTH) ** 0.25
LN_EPS = 1e-5
NEG = -1e30
LOG2E = math.log2(math.e)

LANES = 128
MXU_N = 256
DEN_ROWS = 16
LSE_LANES = LANES // N_HEADS_A
CMP_SPAN = LANES * CMP_STRIDE
WA = N_HEADS_A * HEAD_DIM
WB = N_HEADS_B * HEAD_DIM
WG = NSA_GROUPS * HEAD_DIM
N_GATES = N_HEADS_B * 3
SRC_VS = 3 * WA + WB + 3 * WG
SRC_KW = SRC_VS + WG
SRC_VW = SRC_KW + WG
SRC_GL = SRC_VW + WG
OFF_QA, OFF_KA, OFF_VA, OFF_QB = 0, WA, 2 * WA, 3 * WA
OFF_KC = OFF_QB + WB
OFF_VC = OFF_KC + WG
OFF_KS = OFF_VC + WG
OFF_KW = OFF_KS + WG
NAT_WIDTH = OFF_KW + WG
TR_VS, TR_VW, TR_GL = 0, WG, 2 * WG
TR_WIDTH = 3 * WG

VMEM_LIMIT = 56 * 1024 * 1024

_NT = (((1,), (1,)), ((), ()))


def _cparams(sem):
    return pltpu.CompilerParams(dimension_semantics=sem, vmem_limit_bytes=VMEM_LIMIT)


def _gelu_tanh(x):
    return 0.5 * x * (1.0 + jnp.tanh(math.sqrt(2.0 / math.pi) * (x + 0.044715 * (x * x * x))))


def _ln(x):
    mu = jnp.mean(x, -1, keepdims=True)
    xc = x - mu
    var = jnp.mean(xc * xc, -1, keepdims=True)
    return xc * lax.rsqrt(var + LN_EPS)


def _ada_kernel(c_ref, w_ref, b_ref, o_ref):
    c = c_ref[...]
    a = c * jax.nn.sigmoid(c)
    o_ref[...] = jnp.dot(a, w_ref[...], preferred_element_type=jnp.float32) + b_ref[...]


def _ada(c, w, b):
    B, D = c.shape
    N = w.shape[1]
    tn = D
    return pl.pallas_call(
        _ada_kernel,
        grid=(N // tn,),
        in_specs=[pl.BlockSpec((B, D), lambda j: (0, 0)),
                  pl.BlockSpec((D, tn), lambda j: (0, j)),
                  pl.BlockSpec((1, tn), lambda j: (0, j))],
        out_specs=pl.BlockSpec((B, tn), lambda j: (0, j)),
        out_shape=jax.ShapeDtypeStruct((B, N), jnp.float32),
        compiler_params=_cparams(("arbitrary",)),
        name="adaln",
    )(c, w, b.reshape(1, N))


def _rope_tile(t, cos, s_lo, s_hi):
    half = ROT_DIM // 2
    return t * cos + pltpu.roll(t, half, 1) * s_hi + pltpu.roll(t, LANES - half, 1) * s_lo


def _inproj_kernel(x_ref, mod_ref, w_ref, wt_ref, cos_ref, slo_ref, shi_ref,
                   qa_ref, ka_ref, va_ref, qbx_ref, kc_ref, vc_ref, ksa_ref, kw_ref, vst_ref, vwt_ref, gate_ref,
                   q4_ref, k4_ref, v4_ref, q16_ref, k16_ref, v16_ref,
                   u_ref, ra_ref, rb_ref, *, tm):
    i = pl.program_id(1)
    x = x_ref[...]
    u = _ln(x) * (1.0 + mod_ref[1:2, :]) + mod_ref[0:1, :]
    u_ref[...] = u.astype(jnp.bfloat16)
    cos, slo, shi = cos_ref[...], slo_ref[...], shi_ref[...]
    lane = lax.broadcasted_iota(jnp.int32, (1, LANES), 1)
    bf = jnp.bfloat16

    def proj(off):
        r = jnp.dot(u_ref[...], w_ref[:, off:off + MXU_N], preferred_element_type=jnp.float32)
        return r[:, :LANES], r[:, LANES:]

    def rope(t):
        return _rope_tile(t, cos, slo, shi)

    def emit(t, cols, nat_ref, r4_ref, r16_ref):
        nat_ref[:, cols] = t.astype(bf)
        ra_ref[...] = t
        n4 = tm // 4
        for r in range(4):
            part = ra_ref[pl.ds(r, n4, stride=4), :]
            r4_ref[r, :, cols] = part.astype(bf)
            rb_ref[r * n4:(r + 1) * n4, :] = part
        for r in range(4):
            for m in range(4):
                part = rb_ref[pl.ds(r * n4 + m, n4 // 4, stride=4), :]
                r16_ref[r + 4 * m, :, cols] = part.astype(bf)

    for j in range(WA // MXU_N):
        for k, t in enumerate(proj(OFF_QA + j * MXU_N)):
            emit(rope(t) * LOG2E, slice((2 * j + k) * LANES, (2 * j + k + 1) * LANES), qa_ref, q4_ref, q16_ref)
        for k, t in enumerate(proj(OFF_KA + j * MXU_N)):
            emit(rope(t), slice((2 * j + k) * LANES, (2 * j + k + 1) * LANES), ka_ref, k4_ref, k16_ref)
        for k, t in enumerate(proj(OFF_VA + j * MXU_N)):
            emit(t, slice((2 * j + k) * LANES, (2 * j + k + 1) * LANES), va_ref, v4_ref, v16_ref)
    lo = lane < HEAD_DIM
    for j in range(WB // MXU_N):
        for k, t in enumerate(proj(OFF_QB + j * MXU_N)):
            pair = 2 * j + k
            t = rope(t) * LOG2E
            g = (2 * pair) // NSA_HPG
            tr = pltpu.roll(t, HEAD_DIM, 1)
            in_g = lo if g == 0 else jnp.logical_not(lo)
            even = jnp.where(in_g, t if g == 0 else tr, 0.0)
            odd = jnp.where(in_g, tr if g == 0 else t, 0.0)
            qbx_ref[:, (2 * pair) * LANES:(2 * pair + 1) * LANES] = even.astype(bf)
            qbx_ref[:, (2 * pair + 1) * LANES:(2 * pair + 2) * LANES] = odd.astype(bf)
    kc, vc = proj(OFF_KC)
    kc_ref[...] = rope(kc)
    vc_ref[...] = vc
    ks, kw = proj(OFF_KS)
    tpos = i * tm + lax.broadcasted_iota(jnp.int32, (tm, LANES), 0)
    blk = lax.broadcasted_iota(jnp.int32, (tm, LANES), 1)
    ksa_ref[:, 0:LANES] = jnp.where((tpos // SLC_BLOCK) == blk, 1.0, 0.0).astype(bf)
    ksa_ref[:, LANES:2 * LANES] = rope(ks).astype(bf)
    kw_ref[...] = rope(kw).astype(bf)
    tr_out = lax.dot_general(wt_ref[...], u_ref[...], _NT, preferred_element_type=jnp.float32)
    vst_ref[...] = tr_out[TR_VS:TR_VS + WG].astype(bf)
    vwt_ref[...] = tr_out[TR_VW:TR_VW + WG].astype(bf)
    gate_ref[...] = jax.nn.sigmoid(tr_out[TR_GL:TR_GL + LANES])


def _rope_tables(S):
    inv = 1.0 / (ROPE_THETA ** (jnp.arange(0, ROT_DIM, 2, dtype=jnp.float32) / ROT_DIM))
    ang = jnp.arange(S, dtype=jnp.float32)[:, None] * inv[None, :]
    cos, sin = jnp.cos(ang), jnp.sin(ang)
    half = ROT_DIM // 2
    d = np.arange(LANES) % HEAD_DIM
    idx = jnp.asarray(d % half)
    rot = jnp.asarray(d < ROT_DIM)
    lo = jnp.asarray(d < half)
    hi = jnp.asarray((d >= half) & (d < ROT_DIM))
    c_t = jnp.where(rot[None], cos[:, idx], 1.0)
    s_lo = jnp.where(lo[None], -sin[:, idx], 0.0)
    s_hi = jnp.where(hi[None], sin[:, idx], 0.0)
    return c_t, s_lo, s_hi


def _prep_w_in(w):
    q_scale = HEAD_DIM ** -0.5
    w_nat = jnp.concatenate([w[:, 0:WA] * q_scale, w[:, WA:3 * WA], w[:, 3 * WA:3 * WA + WB] * q_scale,
                             w[:, 3 * WA + WB:SRC_VS], w[:, SRC_KW:SRC_VW]], axis=1)
    w_tr = jnp.concatenate([w[:, SRC_VS:SRC_KW], w[:, SRC_VW:SRC_GL],
                            jnp.pad(w[:, SRC_GL:], ((0, 0), (0, LANES - N_GATES)))], axis=1)
    return w_nat.astype(jnp.bfloat16), w_tr.T.astype(jnp.bfloat16)


def _inproj(x, mod6, w_nat, w_tr, tables, tm):
    B, S, D = x.shape
    cos, slo, shi = tables
    bs = lambda w: pl.BlockSpec((None, tm, w), lambda b, i: (b, i, 0))
    bst = pl.BlockSpec((None, LANES, tm), lambda b, i: (b, 0, i))
    tab = pl.BlockSpec((tm, LANES), lambda b, i: (i, 0))
    sd = lambda w, dt: jax.ShapeDtypeStruct((B, S, w), dt)
    sdt = lambda dt: jax.ShapeDtypeStruct((B, LANES, S), dt)
    bf, f32 = jnp.bfloat16, jnp.float32
    res = lambda d: pl.BlockSpec((None, d, tm // d, WA), lambda b, i: (b, 0, i, 0))
    res1 = pl.BlockSpec((None, None, tm, WA), lambda b, i: (b, 0, i, 0))
    sdr = lambda d: jax.ShapeDtypeStruct((B, d, S // d, WA), bf)
    return pl.pallas_call(
        functools.partial(_inproj_kernel, tm=tm),
        grid=(B, S // tm),
        in_specs=[bs(D),
                  pl.BlockSpec((None, 6, D), lambda b, i: (b, 0, 0)),
                  pl.BlockSpec((D, NAT_WIDTH), lambda b, i: (0, 0)),
                  pl.BlockSpec((TR_WIDTH, D), lambda b, i: (0, 0)),
                  tab, tab, tab],
        out_specs=[res1, res1, res1, bs(2 * WB), bs(WG), bs(WG), bs(2 * LANES), bs(WG), bst, bst, bst]
                  + [res(4)] * 3 + [res(16)] * 3,
        out_shape=[sdr(1), sdr(1), sdr(1), sd(2 * WB, bf), sd(WG, f32), sd(WG, f32),
                   sd(2 * LANES, bf), sd(WG, bf), sdt(bf), sdt(bf), sdt(f32)]
                  + [sdr(4)] * 3 + [sdr(16)] * 3,
        scratch_shapes=[pltpu.VMEM((tm, D), jnp.bfloat16),
                        pltpu.VMEM((tm, LANES), jnp.float32),
                        pltpu.VMEM((tm, LANES), jnp.float32)],
        compiler_params=_cparams(("parallel", "parallel")),
        name="inproj",
    )(x, mod6, w_nat, w_tr, cos, slo, shi)


def _dil_kernel(q_ref, kc_ref, kp_ref, vc_ref, vp_ref, o_ref, lse_ref, *, nblk):
    i = pl.program_id(2)
    blk_rows = BAND_BLOCK
    qi = lax.broadcasted_iota(jnp.int32, (blk_rows, 2 * blk_rows), 0)
    kj = lax.broadcasted_iota(jnp.int32, (blk_rows, 2 * blk_rows), 1)
    in_prev = kj < blk_rows
    bias = jnp.where(in_prev, jnp.where(kj >= qi, 0.0, NEG), jnp.where(kj - blk_rows <= qi, 0.0, NEG))
    bias0 = jnp.where(in_prev, jnp.where(i == 0, NEG, bias), bias)
    lane = lax.broadcasted_iota(jnp.int32, (1, LANES), 1)
    lo = lane < HEAD_DIM
    n_pair = WA // LANES
    for blk in range(nblk):
        rows = slice(blk * blk_rows, (blk + 1) * blk_rows)
        both = slice((blk - 1) * blk_rows, (blk + 1) * blk_rows)
        b = bias0 if blk == 0 else bias
        vals, ss = [], []
        for hp in range(n_pair):
            cols = slice(hp * LANES, (hp + 1) * LANES)
            qt = q_ref[rows, cols]
            if blk == 0:
                k2 = jnp.concatenate([kp_ref[:, cols], kc_ref[rows, cols]], axis=0)
                vals.append(jnp.concatenate([vp_ref[:, cols], vc_ref[rows, cols]], axis=0))
            else:
                k2 = kc_ref[both, cols]
                vals.append(vc_ref[both, cols])
            for h in range(2):
                qh = jnp.where(lo if h == 0 else jnp.logical_not(lo), qt, jnp.zeros_like(qt))
                ss.append(lax.dot_general(qh, k2, _NT, preferred_element_type=jnp.float32) + b)
        ms = [jnp.max(s, -1, keepdims=True) for s in ss]
        ps = [jnp.exp2(s - m) for s, m in zip(ss, ms)]
        dens = [jnp.sum(p, -1, keepdims=True) for p in ps]
        lse = jnp.zeros((blk_rows, LANES), jnp.float32)
        for hp in range(n_pair):
            cols = slice(hp * LANES, (hp + 1) * LANES)
            outs = []
            for h in range(2):
                n = 2 * hp + h
                o = jnp.dot(ps[n].astype(jnp.bfloat16), vals[hp], preferred_element_type=jnp.float32)
                outs.append(o * (1.0 / dens[n]))
                lse = jnp.where(lane // LSE_LANES == n, ms[n] + jnp.log2(dens[n]), lse)
            o_ref[rows, cols] = jnp.where(lo, outs[0], outs[1])
        lse_ref[rows, :] = lse


def _dilated(q, k, v, name):
    B, dil, L, W = q.shape
    rows = min(L, 4 * BAND_BLOCK)
    nblk = rows // BAND_BLOCK
    cur = pl.BlockSpec((None, None, rows, W), lambda b, r, i: (b, r, i, 0))
    cur_lse = pl.BlockSpec((None, None, rows, LANES), lambda b, r, i: (b, r, i, 0))
    prev = pl.BlockSpec((None, None, BAND_BLOCK, W), lambda b, r, i: (b, r, jnp.maximum(i * nblk - 1, 0), 0))
    return pl.pallas_call(
        functools.partial(_dil_kernel, nblk=nblk),
        grid=(B, dil, L // rows),
        in_specs=[cur, cur, prev, cur, prev],
        out_specs=[cur, cur_lse],
        out_shape=[jax.ShapeDtypeStruct((B, dil, L, W), jnp.float32),
                   jax.ShapeDtypeStruct((B, dil, L, LANES), jnp.float32)],
        compiler_params=_cparams(("parallel", "parallel", "arbitrary")),
        name=name,
    )(q, k, k, v, v)


def _cmp_kernel(ck_ref, cv_ref, pe_ref, wk1_ref, wk2_ref, wv1_ref, wv2t_ref, ok_ref, ovt_ref):
    n = ck_ref.shape[0] // CMP_STRIDE
    bf = jnp.bfloat16

    def hidden(c_ref, w1_ref):
        a = jnp.zeros((n, CMP_HIDDEN), jnp.float32)
        b = jnp.zeros((n, CMP_HIDDEN), jnp.float32)
        for j in range(CMP_STRIDE):
            t = c_ref[pl.ds(j, n, stride=CMP_STRIDE), :]
            a = a + jnp.dot((t + pe_ref[j:j + 1, :]).astype(bf), w1_ref[j], preferred_element_type=jnp.float32)
            b = b + jnp.dot((t + pe_ref[CMP_STRIDE + j:CMP_STRIDE + j + 1, :]).astype(bf), w1_ref[CMP_STRIDE + j],
                            preferred_element_type=jnp.float32)
        return _gelu_tanh(a + pltpu.roll(b, n - 1, 0)).astype(bf)

    ok_ref[...] = jnp.dot(hidden(ck_ref, wk1_ref), wk2_ref[...],
                          preferred_element_type=jnp.float32).astype(ok_ref.dtype)
    ovt_ref[...] = lax.dot_general(wv2t_ref[...], hidden(cv_ref, wv1_ref), _NT,
                                   preferred_element_type=jnp.float32).astype(ovt_ref.dtype)


def _compress(kc, vc, pe, w_ck1, w_ck2, w_cv1, w_cv2):
    B, S, _ = kc.shape
    n = S // CMP_STRIDE
    bf = jnp.bfloat16

    def slabs(w1):
        w = w1.reshape(CMP_LEN, HEAD_DIM, CMP_HIDDEN).astype(bf)
        z = jnp.zeros_like(w)
        return jnp.stack([jnp.concatenate([w, z], axis=1), jnp.concatenate([z, w], axis=1)], axis=0)

    seq = pl.BlockSpec((None, S, WG), lambda b, g: (b, 0, 0))
    full = lambda a: pl.BlockSpec(a.shape, lambda b, g: (0,) * a.ndim)
    slab = pl.BlockSpec((None, CMP_LEN, WG, CMP_HIDDEN), lambda b, g: (g, 0, 0, 0))
    pe2 = jnp.concatenate([pe, pe], axis=-1)
    args = (kc, vc, pe2, slabs(w_ck1), jnp.concatenate([w_ck2, w_ck2], axis=-1).astype(bf),
            slabs(w_cv1), w_cv2.T.astype(bf))
    return pl.pallas_call(
        _cmp_kernel,
        grid=(B, NSA_GROUPS),
        in_specs=[seq, seq, full(pe2), slab, full(args[4]), slab, full(args[6])],
        out_specs=[pl.BlockSpec((None, None, n, LANES), lambda b, g: (b, g, 0, 0)),
                   pl.BlockSpec((None, None, HEAD_DIM, n), lambda b, g: (b, g, 0, 0))],
        out_shape=[jax.ShapeDtypeStruct((B, NSA_GROUPS, n, LANES), bf),
                   jax.ShapeDtypeStruct((B, NSA_GROUPS, HEAD_DIM, n), bf)],
        compiler_params=_cparams(("parallel", "parallel")),
        name="compress",
    )(*args)


def _nsa_cmp_kernel(q_ref, kcc_ref, vcct_ref, ovt_ref, o_ref, sel_ref, *, tq, nsel, topk, span):
    qi = pl.program_id(2)
    t0 = qi * tq
    nc_all = kcc_ref.shape[0]
    n_var = max(1, (nc_all * CMP_STRIDE) // span)

    def body(nc, nb):
        tpos = t0 + lax.broadcasted_iota(jnp.int32, (nc, tq), 1)
        cend = lax.broadcasted_iota(jnp.int32, (nc, tq), 0) * CMP_STRIDE + (CMP_LEN - 1)
        cbias = jnp.where(cend <= tpos, 0.0, NEG)
        any_vis = (t0 + lax.broadcasted_iota(jnp.int32, (1, tq), 1)) >= CMP_LEN - 1
        kcc, vcct, ovt = kcc_ref[0:nc, :], vcct_ref[:, 0:nc], ovt_ref[0:nb, 0:nc]
        ss = [lax.dot_general(kcc, q_ref[:, h * LANES:(h + 1) * LANES], _NT,
                              preferred_element_type=jnp.float32) + cbias for h in range(NSA_HPG)]
        ms = [jnp.max(s, 0, keepdims=True) for s in ss]
        ps = [jnp.exp2(s - m).astype(jnp.bfloat16) for s, m in zip(ss, ms)]
        vaug = jnp.concatenate([vcct, jnp.ones((DEN_ROWS, nc), jnp.bfloat16)], axis=0)
        imp = jnp.zeros((nb, tq), jnp.float32)
        for h in range(NSA_HPG):
            oa = jnp.dot(vaug, ps[h], preferred_element_type=jnp.float32)
            inv = jnp.where(any_vis, 1.0 / oa[HEAD_DIM:HEAD_DIM + 1], 0.0)
            o_ref[h * HEAD_DIM:(h + 1) * HEAD_DIM, :] = oa[:HEAD_DIM] * inv
            imp = imp + jnp.dot(ovt, ps[h], preferred_element_type=jnp.float32) * inv
        blk = lax.broadcasted_iota(jnp.int32, (nb, tq), 0)
        tq_pos = t0 + lax.broadcasted_iota(jnp.int32, (nb, tq), 1)
        cur = tq_pos // SLC_BLOCK
        forced = (blk == cur) | (blk == cur - 1) | (blk == 0)
        valid = blk <= cur
        cand = valid & jnp.logical_not(forced) & (blk < nsel)
        score = jnp.where(cand, imp, -jnp.inf)
        blk_f = blk.astype(jnp.float32)
        for _ in range(topk - 3):
            mx = jnp.max(score, 0, keepdims=True)
            first = jnp.min(jnp.where(score == mx, blk_f, float(2 * LANES)), 0, keepdims=True)
            score = jnp.where(blk_f == first, -jnp.inf, score)
        sel = jnp.where(forced, 1.0, jnp.where(cand & (score == -jnp.inf), 1.0, 0.0))
        sel = jnp.where(valid, sel, 0.0)
        if nb < LANES:
            sel = jnp.concatenate([sel, jnp.zeros((LANES - nb, tq), jnp.float32)], axis=0)
        sel_ref[...] = ((sel.T - 1.0) * (-NEG)).astype(jnp.bfloat16)

    variant = jnp.minimum((t0 + tq - 1) // span, n_var - 1)
    for k in range(n_var):
        last_var = k == n_var - 1
        nc = nc_all if last_var else (k + 1) * span // CMP_STRIDE
        nb = LANES if last_var else (k + 1) * span // SLC_BLOCK
        pl.when(variant == k)(functools.partial(body, nc, nb))


def _overlap_t(S):
    nc = S // CMP_STRIDE
    ns = S // SLC_BLOCK
    cs = np.arange(nc) * CMP_STRIDE
    ss = np.arange(LANES) * SLC_BLOCK
    ov = np.clip(np.minimum(cs[None, :] + CMP_LEN, ss[:, None] + SLC_BLOCK)
                 - np.maximum(cs[None, :], ss[:, None]), 0, None).astype(np.float32) / CMP_LEN
    ov[ns:, :] = 0.0
    ov[:, nc - CMP_LEN // CMP_STRIDE + 1:] = 0.0
    return jnp.asarray(ov, jnp.bfloat16)


def _nsa_cmp(qbx, kcc, vcct, tq):
    B, S, _ = qbx.shape
    nc = kcc.shape[2]
    ns = S // SLC_BLOCK
    gw = NSA_HPG * LANES
    ovt = _overlap_t(S)
    return pl.pallas_call(
        functools.partial(_nsa_cmp_kernel, tq=tq, nsel=ns, topk=min(SLC_TOPK, ns), span=CMP_SPAN),
        grid=(B, NSA_GROUPS, S // tq),
        in_specs=[pl.BlockSpec((None, tq, gw), lambda b, g, i: (b, i, g)),
                  pl.BlockSpec((None, None, nc, LANES), lambda b, g, i: (b, g, 0, 0)),
                  pl.BlockSpec((None, None, HEAD_DIM, nc), lambda b, g, i: (b, g, 0, 0)),
                  pl.BlockSpec((LANES, nc), lambda b, g, i: (0, 0))],
        out_specs=[pl.BlockSpec((None, NSA_HPG * HEAD_DIM, tq), lambda b, g, i: (b, g, i)),
                   pl.BlockSpec((None, None, tq, LANES), lambda b, g, i: (b, g, i, 0))],
        out_shape=[jax.ShapeDtypeStruct((B, WB, S), jnp.float32),
                   jax.ShapeDtypeStruct((B, NSA_GROUPS, S, LANES), jnp.bfloat16)],
        compiler_params=_cparams(("parallel", "parallel", "parallel")),
        name="nsa_cmp",
    )(qbx, kcc, vcct, ovt)


def _nsa_main_kernel(q_ref, sel_ref, ksa_ref, vst_ref, kw_ref, vwt_ref, ocmp_ref, gate_ref, o_ref,
                     qa_scr, m_scr, acc_scr, win_scr, sa_scr, sb_scr, ma_scr, mb_scr, *, tq, tk):
    g = pl.program_id(1)
    qi = pl.program_id(2)
    t0 = qi * tq
    cols = NSA_HPG * tq
    bf = jnp.bfloat16
    selb = sel_ref[...]
    for h in range(NSA_HPG):
        qa_scr[h * tq:(h + 1) * tq, 0:LANES] = selb
        qa_scr[h * tq:(h + 1) * tq, LANES:2 * LANES] = q_ref[:, h * LANES:(h + 1) * LANES]
    m_scr[...] = jnp.full((1, cols), -1e37, jnp.float32)
    acc_scr[...] = jnp.zeros(acc_scr.shape, jnp.float32)

    def with_ones(vt):
        return jnp.concatenate([vt, jnp.ones((DEN_ROWS, vt.shape[1]), bf)], axis=0)

    def scores(tile, dst, dst_max):
        start = pl.multiple_of(tile * tk, tk)
        s = lax.dot_general(ksa_ref[pl.ds(start, tk), :], qa_scr[...], _NT,
                            preferred_element_type=jnp.float32)
        dst[...] = s
        dst_max[...] = jnp.max(s, 0, keepdims=True)

    def consume(tile, src, src_max):
        start = pl.multiple_of(tile * tk, tk)
        vt = with_ones(vst_ref[:, pl.ds(start, tk)])
        m_prev = m_scr[...]
        m_new = jnp.maximum(m_prev, src_max[...])
        alpha = jnp.exp2(m_prev - m_new)
        p = jnp.exp2(src[...] - m_new)
        acc_scr[...] = alpha * acc_scr[...] + jnp.dot(vt, p.astype(bf), preferred_element_type=jnp.float32)
        m_scr[...] = m_new

    last = t0 // tk

    wk = WIN + tq
    wstart = pl.multiple_of(jnp.maximum(t0 - WIN, 0), tq)
    sw = lax.dot_general(kw_ref[pl.ds(wstart, wk), :], qa_scr[:, LANES:2 * LANES], _NT,
                         preferred_element_type=jnp.float32)
    dist = ((lax.broadcasted_iota(jnp.int32, (wk, tq), 1) + t0)
            - (lax.broadcasted_iota(jnp.int32, (wk, tq), 0) + wstart))
    wb = jnp.where((dist >= 0) & (dist < WIN), 0.0, NEG)
    sw = sw + jnp.concatenate([wb] * NSA_HPG, axis=1)
    mw = jnp.max(sw, 0, keepdims=True)
    scores(0, sa_scr, ma_scr)
    pw = jnp.exp2(sw - mw).astype(bf)
    ow = jnp.dot(with_ones(vwt_ref[:, pl.ds(wstart, wk)]), pw, preferred_element_type=jnp.float32)
    win_scr[...] = ow[:HEAD_DIM] * (1.0 / ow[HEAD_DIM:HEAD_DIM + 1])

    def pair(first):
        scores(first + 1, sb_scr, mb_scr)
        consume(first, sa_scr, ma_scr)
        scores(first + 2, sa_scr, ma_scr)
        consume(first + 1, sb_scr, mb_scr)

    def quad_body(jj, carry):
        pair(4 * jj)
        pair(4 * jj + 2)
        return carry

    n_quad = last // 4
    lax.fori_loop(0, n_quad, quad_body, 0)

    @pl.when(last - 4 * n_quad >= 2)
    def _():
        pair(4 * n_quad)

    kk = lax.broadcasted_iota(jnp.int32, (tq, tq), 0)
    qq = lax.broadcasted_iota(jnp.int32, (tq, tq), 1)
    cb = jnp.where(kk <= qq, 0.0, NEG)
    cb = jnp.concatenate([cb] * NSA_HPG, axis=1)
    diag = pl.multiple_of(t0 - last * tk, tq)

    def finish(src):
        src[pl.ds(diag, tq), :] = src[pl.ds(diag, tq), :] + cb
        start = pl.multiple_of(last * tk, tk)
        m_prev = m_scr[...]
        m_new = jnp.maximum(m_prev, jnp.max(src[...], 0, keepdims=True))
        alpha = jnp.exp2(m_prev - m_new)
        p = jnp.exp2(src[...] - m_new).astype(bf)
        pv = jnp.dot(with_ones(vst_ref[:, pl.ds(start, tk)]), p, preferred_element_type=jnp.float32)
        acc_scr[...] = alpha * acc_scr[...] + pv

    @pl.when(last % 2 == 1)
    def _():
        scores(last, sb_scr, mb_scr)
        consume(last - 1, sa_scr, ma_scr)
        finish(sb_scr)

    @pl.when(last % 2 == 0)
    def _():
        finish(sa_scr)

    o_slc = acc_scr[0:HEAD_DIM, :] * (1.0 / acc_scr[HEAD_DIM:HEAD_DIM + 1, :])
    o_win = win_scr[...]

    base = g * (NSA_HPG * 3)
    heads = []
    for h in range(NSA_HPG):
        c = slice(h * tq, (h + 1) * tq)
        g_cmp = gate_ref[pl.ds(base + 3 * h, 1), :]
        g_slc = gate_ref[pl.ds(base + 3 * h + 1, 1), :]
        g_win = gate_ref[pl.ds(base + 3 * h + 2, 1), :]
        heads.append(g_cmp * ocmp_ref[h * HEAD_DIM:(h + 1) * HEAD_DIM, :] + g_slc * o_slc[:, c] + g_win * o_win[:, c])
    o_ref[...] = jnp.concatenate(heads, axis=0).T.astype(o_ref.dtype)


def _nsa_main(qbx, selb, ksa, vst, kw, vwt, o_cmp_t, gates_t, tq, tk):
    B, S, _ = qbx.shape
    gw = NSA_HPG * LANES
    ow = NSA_HPG * HEAD_DIM
    cols = NSA_HPG * tq
    seq = lambda w: pl.BlockSpec((None, S, w), lambda b, g, i: (b, 0, 0))
    seqt = pl.BlockSpec((None, HEAD_DIM, S), lambda b, g, i: (b, g, 0))
    return pl.pallas_call(
        functools.partial(_nsa_main_kernel, tq=tq, tk=tk),
        grid=(B, NSA_GROUPS, S // tq),
        in_specs=[pl.BlockSpec((None, tq, gw), lambda b, g, i: (b, i, g)),
                  pl.BlockSpec((None, None, tq, LANES), lambda b, g, i: (b, g, i, 0)),
                  seq(2 * LANES), seqt, seq(WG), seqt,
                  pl.BlockSpec((None, ow, tq), lambda b, g, i: (b, g, i)),
                  pl.BlockSpec((None, LANES, tq), lambda b, g, i: (b, 0, i))],
        out_specs=pl.BlockSpec((None, tq, ow), lambda b, g, i: (b, i, g)),
        out_shape=jax.ShapeDtypeStruct((B, S, WB), jnp.bfloat16),
        scratch_shapes=[pltpu.VMEM((cols, 2 * LANES), jnp.bfloat16),
                        pltpu.VMEM((1, cols), jnp.float32),
                        pltpu.VMEM((HEAD_DIM + DEN_ROWS, cols), jnp.float32),
                        pltpu.VMEM((HEAD_DIM, cols), jnp.float32),
                        pltpu.VMEM((tk, cols), jnp.float32),
                        pltpu.VMEM((tk, cols), jnp.float32),
                        pltpu.VMEM((1, cols), jnp.float32),
                        pltpu.VMEM((1, cols), jnp.float32)],
        compiler_params=_cparams(("parallel", "parallel", "arbitrary")),
        name="nsa_main",
    )(qbx, selb, ksa, vst, kw, vwt, o_cmp_t, gates_t)


def _outproj_kernel(x_ref, o1_ref, l1_ref, o4_ref, l4_ref, o16_ref, l16_ref, ob_ref, mod_ref, wo_ref, g_ref, b_ref,
                    ex_ref, y_ref, oa_scr, s0_scr, s1_scr, s2_scr, s3_scr, *, tm):
    def token_order(src, cols, scr):
        dil = src.shape[0]
        for r in range(dil):
            scr[pl.ds(r, tm // dil, stride=dil), :] = src[r, :, cols]
        return scr[...]

    l1 = l1_ref[...]
    l2 = token_order(l4_ref, slice(None), s0_scr)
    l3 = token_order(l16_ref, slice(None), s1_scr)
    m = jnp.maximum(jnp.maximum(l1, l2), l3)
    es = [jnp.exp2(l1 - m), jnp.exp2(l2 - m), jnp.exp2(l3 - m)]
    inv = 1.0 / (es[0] + es[1] + es[2])

    def widen(w):
        hi = w.astype(jnp.bfloat16)
        lo = (w - hi.astype(jnp.float32)).astype(jnp.bfloat16)
        return jnp.dot(jnp.concatenate([hi, lo], axis=1), ex_ref[...], preferred_element_type=jnp.float32)

    w1, w2, w3 = [widen(e * inv) for e in es]
    for c in range(WA // LANES):
        cols = slice(c * LANES, (c + 1) * LANES)
        o2 = token_order(o4_ref, cols, s2_scr)
        o3 = token_order(o16_ref, cols, s3_scr)
        o_a = w1[:, cols] * o1_ref[:, cols] + w2[:, cols] * o2 + w3[:, cols] * o3
        oa_scr[:, cols] = o_a.astype(jnp.bfloat16)
    y = (jnp.dot(oa_scr[...], wo_ref[:WA, :], preferred_element_type=jnp.float32)
         + jnp.dot(ob_ref[...], wo_ref[WA:, :], preferred_element_type=jnp.float32))
    z = ALPHA * x_ref[...] + mod_ref[2:3, :] * y
    y_ref[...] = _ln(z) * g_ref[...] + b_ref[...]


def _outproj(x, pats, o_b, mod6, wo_bf16, ln_g, ln_b, tm):
    B, S, D = x.shape
    bs = lambda w: pl.BlockSpec((None, tm, w), lambda b, i: (b, i, 0))
    res = lambda d, w: pl.BlockSpec((None, d, tm // d, w), lambda b, i: (b, 0, i, 0))
    res1 = lambda w: pl.BlockSpec((None, None, tm, w), lambda b, i: (b, 0, i, 0))
    vec = pl.BlockSpec((1, D), lambda b, i: (0, 0))
    (o1, l1), (o4, l4), (o16, l16) = pats
    scr = pltpu.VMEM((tm, LANES), jnp.float32)
    expand = np.zeros((LANES, WA), np.float32)
    expand[(np.arange(WA) // HEAD_DIM) * LSE_LANES, np.arange(WA)] = 1.0
    expand = jnp.asarray(np.concatenate([expand, expand], axis=0), jnp.bfloat16)
    return pl.pallas_call(
        functools.partial(_outproj_kernel, tm=tm),
        grid=(B, S // tm),
        in_specs=[bs(D), res1(WA), res1(LANES), res(4, WA), res(4, LANES), res(16, WA), res(16, LANES), bs(WB),
                  pl.BlockSpec((None, 6, D), lambda b, i: (b, 0, 0)),
                  pl.BlockSpec((WA + WB, D), lambda b, i: (0, 0)), vec, vec,
                  pl.BlockSpec((2 * LANES, WA), lambda b, i: (0, 0))],
        out_specs=bs(D),
        out_shape=jax.ShapeDtypeStruct((B, S, D), jnp.float32),
        scratch_shapes=[pltpu.VMEM((tm, WA), jnp.bfloat16), scr, scr, scr, scr],
        compiler_params=_cparams(("parallel", "parallel")),
        name="outproj",
    )(x, o1, l1, o4, l4, o16, l16, o_b, mod6, wo_bf16,
      ln_g.reshape(1, D), ln_b.reshape(1, D), expand)


def _ffn_kernel(x_ref, mod_ref, wup_ref, cw_ref, cb_ref, wdn_ref, g_ref, b_ref, y_ref,
                u_scr, buf_scr, carry_scr, acc_scr, *, tm, fc):
    i = pl.program_id(1)
    F = wdn_ref.shape[0]
    pad = 8

    @pl.when(i == 0)
    def _():
        carry_scr[...] = jnp.zeros_like(carry_scr)

    th = tm // 2
    n_chunk = F // fc
    halves = (slice(0, th), slice(th, tm))

    def prologue(rows):
        u_scr[rows, :] = (_ln(x_ref[rows, :]) * (1.0 + mod_ref[4:5, :]) + mod_ref[3:4, :]).astype(jnp.bfloat16)

    def up(rows, c):
        u = u_scr[rows, :]
        return (jnp.dot(u, wup_ref[:, c * fc:(c + 1) * fc], preferred_element_type=jnp.float32),
                jnp.dot(u, wup_ref[:, F + c * fc:F + (c + 1) * fc], preferred_element_type=jnp.float32))

    def chunk(rows, c, a_gate, a_val):
        cols = slice(c * fc, (c + 1) * fc)
        buf_scr[0:pad, :] = carry_scr[:, cols]
        buf_scr[pad:pad + th, :] = a_gate
        carry_scr[:, cols] = a_gate[th - pad:th, :]
        conv = (cw_ref[0:1, cols] * buf_scr[pad - 2:pad - 2 + th, :]
                + cw_ref[1:2, cols] * buf_scr[pad - 1:pad - 1 + th, :]
                + cw_ref[2:3, cols] * a_gate + cb_ref[:, cols])
        h = (_gelu_tanh(conv) * a_val).astype(jnp.bfloat16)
        d = jnp.dot(h, wdn_ref[cols, :], preferred_element_type=jnp.float32)
        if c == 0:
            acc_scr[rows, :] = d
        else:
            acc_scr[rows, :] += d

    def epilogue(rows):
        z = ALPHA * x_ref[rows, :] + mod_ref[5:6, :] * acc_scr[rows, :]
        y_ref[rows, :] = _ln(z) * g_ref[...] + b_ref[...]

    items = [(rows, c) for rows in halves for c in range(n_chunk)]
    prologue(halves[0])
    nxt = up(*items[0])
    for k, (rows, c) in enumerate(items):
        cur = nxt
        if k == 0:
            prologue(halves[1])
        if k + 1 < len(items):
            nxt = up(*items[k + 1])
        if k == n_chunk + 1:
            epilogue(halves[0])
        chunk(rows, c, *cur)
    epilogue(halves[1])


def _ffn(x, mod6, wup_bf16, conv_w, conv_b, wdn_bf16, ln_g, ln_b, tm, fc):
    B, S, D = x.shape
    F = wdn_bf16.shape[0]
    bs = pl.BlockSpec((None, tm, D), lambda b, i: (b, i, 0))
    full = lambda a: pl.BlockSpec(a.shape, lambda b, i: (0,) * a.ndim)
    args = (wup_bf16, conv_w, conv_b.reshape(1, F), wdn_bf16, ln_g.reshape(1, D), ln_b.reshape(1, D))
    return pl.pallas_call(
        functools.partial(_ffn_kernel, tm=tm, fc=fc),
        grid=(B, S // tm),
        in_specs=[bs, pl.BlockSpec((None, 6, D), lambda b, i: (b, 0, 0))] + [full(a) for a in args],
        out_specs=bs,
        out_shape=jax.ShapeDtypeStruct((B, S, D), jnp.float32),
        scratch_shapes=[pltpu.VMEM((tm, D), jnp.bfloat16),
                        pltpu.VMEM((tm // 2 + 8, fc), jnp.float32),
                        pltpu.VMEM((8, F), jnp.float32),
                        pltpu.VMEM((tm, D), jnp.float32)],
        compiler_params=_cparams(("arbitrary", "arbitrary")),
        name="ffn",
    )(x, mod6, *args)


def kernel(x, c, w_ada, b_ada, w_in, pe_cmp, w_ck1, w_ck2, w_cv1, w_cv2, w_o, ln1_g, ln1_b, w_up, conv_w, conv_b,
           w_down, ln2_g, ln2_b):
    B, S, D = x.shape
    tables = _rope_tables(S)
    tm = min(512, S)
    tq = 128
    tk = min(512, S)
    for l in range(DEPTH):
        mod6 = _ada(c, w_ada[l], b_ada[l]).reshape(B, 6, D)
        w_nat, w_tr = _prep_w_in(w_in[l])
        (qa, ka, va, qbx, kc, vc, ksa, kw, vst, vwt, gates_t,
         q4, k4, v4, q16, k16, v16) = _inproj(x, mod6, w_nat, w_tr, tables, tm)
        pats = [_dilated(qa, ka, va, "dilated1"),
                _dilated(q4, k4, v4, "dilated4"), _dilated(q16, k16, v16, "dilated16")]
        kcc, vcct = _compress(kc, vc, pe_cmp[l], w_ck1[l], w_ck2[l], w_cv1[l], w_cv2[l])
        o_cmp_t, selb = _nsa_cmp(qbx, kcc, vcct, 4 * tq)
        o_b = _nsa_main(qbx, selb, ksa, vst, kw, vwt, o_cmp_t, gates_t, 2 * tq, tk)
        x = _outproj(x, pats, o_b, mod6, w_o[l].astype(jnp.bfloat16), ln1_g[l], ln1_b[l], tm)
        x = _ffn(x, mod6, w_up[l].astype(jnp.bfloat16), conv_w[l], conv_b[l], w_down[l].astype(jnp.bfloat16),
                 ln2_g[l], ln2_b[l], tm, 256)
    return x
```

```python
import functools
import math

import numpy as np
import jax
import jax.numpy as jnp
from jax import lax
from jax.experimental import pallas as pl
from jax.experimental.pallas import tpu as pltpu

HEAD_DIM = 64
N_HEADS_A = 8
N_HEADS_B = 8
DIL_PATTERNS = ((128, 1), (512, 4), (2048, 16))
BAND_BLOCK = 128
ROT_DIM = HEAD_DIM // 4
ROPE_THETA = 500000.0
NSA_GROUPS = 2
NSA_HPG = N_HEADS_B // NSA_GROUPS
CMP_LEN = 32
CMP_STRIDE = 16
CMP_HIDDEN = 4 * HEAD_DIM
SLC_BLOCK = 64
SLC_TOPK = 16
WIN = 512
D_FF = 2816
CONV_W = 3
DEPTH = 1
ALPHA = (2 * DEPTH) ** 0.25
LN_EPS = 1e-5
NEG = -1e30
LOG2E = math.log2(math.e)

LANES = 128
MXU_N = 256
DEN_ROWS = 16
LSE_LANES = LANES // N_HEADS_A
CMP_SPAN = LANES * CMP_STRIDE
WA = N_HEADS_A * HEAD_DIM
WB = N_HEADS_B * HEAD_DIM
WG = NSA_GROUPS * HEAD_DIM
N_GATES = N_HEADS_B * 3
SRC_VS = 3 * WA + WB + 3 * WG
SRC_KW = SRC_VS + WG
SRC_VW = SRC_KW + WG
SRC_GL = SRC_VW + WG
OFF_QA, OFF_KA, OFF_VA, OFF_QB = 0, WA, 2 * WA, 3 * WA
OFF_KC = OFF_QB + WB
OFF_VC = OFF_KC + WG
OFF_KS = OFF_VC + WG
OFF_KW = OFF_KS + WG
NAT_WIDTH = OFF_KW + WG
TR_VS, TR_VW, TR_GL = 0, WG, 2 * WG
TR_WIDTH = 3 * WG

VMEM_LIMIT = 56 * 1024 * 1024

_NT = (((1,), (1,)), ((), ()))


def _cparams(sem):
    return pltpu.CompilerParams(dimension_semantics=sem, vmem_limit_bytes=VMEM_LIMIT)


def _gelu_tanh(x):
    return 0.5 * x * (1.0 + jnp.tanh(math.sqrt(2.0 / math.pi) * (x + 0.044715 * (x * x * x))))


def _ln(x):
    mu = jnp.mean(x, -1, keepdims=True)
    xc = x - mu
    var = jnp.mean(xc * xc, -1, keepdims=True)
    return xc * lax.rsqrt(var + LN_EPS)


def _ada_kernel(c_ref, w_ref, b_ref, o_ref):
    c = c_ref[...]
    a = c * jax.nn.sigmoid(c)
    o_ref[...] = jnp.dot(a, w_ref[...], preferred_element_type=jnp.float32) + b_ref[...]


def _ada(c, w, b):
    B, D = c.shape
    N = w.shape[1]
    tn = D
    return pl.pallas_call(
        _ada_kernel,
        grid=(N // tn,),
        in_specs=[pl.BlockSpec((B, D), lambda j: (0, 0)),
                  pl.BlockSpec((D, tn), lambda j: (0, j)),
                  pl.BlockSpec((1, tn), lambda j: (0, j))],
        out_specs=pl.BlockSpec((B, tn), lambda j: (0, j)),
        out_shape=jax.ShapeDtypeStruct((B, N), jnp.float32),
        compiler_params=_cparams(("arbitrary",)),
        name="adaln",
    )(c, w, b.reshape(1, N))


def _rope_tile(t, cos, s_lo, s_hi):
    half = ROT_DIM // 2
    return t * cos + pltpu.roll(t, half, 1) * s_hi + pltpu.roll(t, LANES - half, 1) * s_lo


def _inproj_kernel(x_ref, mod_ref, w_ref, wt_ref, cos_ref, slo_ref, shi_ref,
                   qa_ref, ka_ref, va_ref, qbx_ref, kc_ref, vc_ref, ksa_ref, kw_ref, vst_ref, vwt_ref, gate_ref,
                   q4_ref, k4_ref, v4_ref, q16_ref, k16_ref, v16_ref,
                   u_ref, ra_ref, rb_ref, *, tm):
    i = pl.program_id(1)
    x = x_ref[...]
    u = _ln(x) * (1.0 + mod_ref[1:2, :]) + mod_ref[0:1, :]
    u_ref[...] = u.astype(jnp.bfloat16)
    cos, slo, shi = cos_ref[...], slo_ref[...], shi_ref[...]
    lane = lax.broadcasted_iota(jnp.int32, (1, LANES), 1)
    bf = jnp.bfloat16

    def proj(off):
        r = jnp.dot(u_ref[...], w_ref[:, off:off + MXU_N], preferred_element_type=jnp.float32)
        return r[:, :LANES], r[:, LANES:]

    def rope(t):
        return _rope_tile(t, cos, slo, shi)

    def emit(t, cols, nat_ref, r4_ref, r16_ref):
        nat_ref[:, cols] = t.astype(bf)
        ra_ref[...] = t
        n4 = tm // 4
        for r in range(4):
            part = ra_ref[pl.ds(r, n4, stride=4), :]
            r4_ref[r, :, cols] = part.astype(bf)
            rb_ref[r * n4:(r + 1) * n4, :] = part
        for r in range(4):
            for m in range(4):
                part = rb_ref[pl.ds(r * n4 + m, n4 // 4, stride=4), :]
                r16_ref[r + 4 * m, :, cols] = part.astype(bf)

    for j in range(WA // MXU_N):
        for k, t in enumerate(proj(OFF_QA + j * MXU_N)):
            emit(rope(t) * LOG2E, slice((2 * j + k) * LANES, (2 * j + k + 1) * LANES), qa_ref, q4_ref, q16_ref)
        for k, t in enumerate(proj(OFF_KA + j * MXU_N)):
            emit(rope(t), slice((2 * j + k) * LANES, (2 * j + k + 1) * LANES), ka_ref, k4_ref, k16_ref)
        for k, t in enumerate(proj(OFF_VA + j * MXU_N)):
            emit(t, slice((2 * j + k) * LANES, (2 * j + k + 1) * LANES), va_ref, v4_ref, v16_ref)
    lo = lane < HEAD_DIM
    for j in range(WB // MXU_N):
        for k, t in enumerate(proj(OFF_QB + j * MXU_N)):
            pair = 2 * j + k
            t = rope(t) * LOG2E
            g = (2 * pair) // NSA_HPG
            tr = pltpu.roll(t, HEAD_DIM, 1)
            in_g = lo if g == 0 else jnp.logical_not(lo)
            even = jnp.where(in_g, t if g == 0 else tr, 0.0)
            odd = jnp.where(in_g, tr if g == 0 else t, 0.0)
            qbx_ref[:, (2 * pair) * LANES:(2 * pair + 1) * LANES] = even.astype(bf)
            qbx_ref[:, (2 * pair + 1) * LANES:(2 * pair + 2) * LANES] = odd.astype(bf)
    kc, vc = proj(OFF_KC)
    kc_ref[...] = rope(kc)
    vc_ref[...] = vc
    ks, kw = proj(OFF_KS)
    tpos = i * tm + lax.broadcasted_iota(jnp.int32, (tm, LANES), 0)
    blk = lax.broadcasted_iota(jnp.int32, (tm, LANES), 1)
    ksa_ref[:, 0:LANES] = jnp.where((tpos // SLC_BLOCK) == blk, 1.0, 0.0).astype(bf)
    ksa_ref[:, LANES:2 * LANES] = rope(ks).astype(bf)
    kw_ref[...] = rope(kw).astype(bf)
    tr_out = lax.dot_general(wt_ref[...], u_ref[...], _NT, preferred_element_type=jnp.float32)
    vst_ref[...] = tr_out[TR_VS:TR_VS + WG].astype(bf)
    vwt_ref[...] = tr_out[TR_VW:TR_VW + WG].astype(bf)
    gate_ref[...] = jax.nn.sigmoid(tr_out[TR_GL:TR_GL + LANES])


def _rope_tables(S):
    inv = 1.0 / (ROPE_THETA ** (jnp.arange(0, ROT_DIM, 2, dtype=jnp.float32) / ROT_DIM))
    ang = jnp.arange(S, dtype=jnp.float32)[:, None] * inv[None, :]
    cos, sin = jnp.cos(ang), jnp.sin(ang)
    half = ROT_DIM // 2
    d = np.arange(LANES) % HEAD_DIM
    idx = jnp.asarray(d % half)
    rot = jnp.asarray(d < ROT_DIM)
    lo = jnp.asarray(d < half)
    hi = jnp.asarray((d >= half) & (d < ROT_DIM))
    c_t = jnp.where(rot[None], cos[:, idx], 1.0)
    s_lo = jnp.where(lo[None], -sin[:, idx], 0.0)
    s_hi = jnp.where(hi[None], sin[:, idx], 0.0)
    return c_t, s_lo, s_hi


def _prep_w_in(w):
    q_scale = HEAD_DIM ** -0.5
    w_nat = jnp.concatenate([w[:, 0:WA] * q_scale, w[:, WA:3 * WA], w[:, 3 * WA:3 * WA + WB] * q_scale,
                             w[:, 3 * WA + WB:SRC_VS], w[:, SRC_KW:SRC_VW]], axis=1)
    w_tr = jnp.concatenate([w[:, SRC_VS:SRC_KW], w[:, SRC_VW:SRC_GL],
                            jnp.pad(w[:, SRC_GL:], ((0, 0), (0, LANES - N_GATES)))], axis=1)
    return w_nat.astype(jnp.bfloat16), w_tr.T.astype(jnp.bfloat16)


def _inproj(x, mod6, w_nat, w_tr, tables, tm):
    B, S, D = x.shape
    cos, slo, shi = tables
    bs = lambda w: pl.BlockSpec((None, tm, w), lambda b, i: (b, i, 0))
    bst = pl.BlockSpec((None, LANES, tm), lambda b, i: (b, 0, i))
    tab = pl.BlockSpec((tm, LANES), lambda b, i: (i, 0))
    sd = lambda w, dt: jax.ShapeDtypeStruct((B, S, w), dt)
    sdt = lambda dt: jax.ShapeDtypeStruct((B, LANES, S), dt)
    bf, f32 = jnp.bfloat16, jnp.float32
    res = lambda d: pl.BlockSpec((None, d, tm // d, WA), lambda b, i: (b, 0, i, 0))
    res1 = pl.BlockSpec((None, None, tm, WA), lambda b, i: (b, 0, i, 0))
    sdr = lambda d: jax.ShapeDtypeStruct((B, d, S // d, WA), bf)
    return pl.pallas_call(
        functools.partial(_inproj_kernel, tm=tm),
        grid=(B, S // tm),
        in_specs=[bs(D),
                  pl.BlockSpec((None, 6, D), lambda b, i: (b, 0, 0)),
                  pl.BlockSpec((D, NAT_WIDTH), lambda b, i: (0, 0)),
                  pl.BlockSpec((TR_WIDTH, D), lambda b, i: (0, 0)),
                  tab, tab, tab],
        out_specs=[res1, res1, res1, bs(2 * WB), bs(WG), bs(WG), bs(2 * LANES), bs(WG), bst, bst, bst]
                  + [res(4)] * 3 + [res(16)] * 3,
        out_shape=[sdr(1), sdr(1), sdr(1), sd(2 * WB, bf), sd(WG, f32), sd(WG, f32),
                   sd(2 * LANES, bf), sd(WG, bf), sdt(bf), sdt(bf), sdt(f32)]
                  + [sdr(4)] * 3 + [sdr(16)] * 3,
        scratch_shapes=[pltpu.VMEM((tm, D), jnp.bfloat16),
                        pltpu.VMEM((tm, LANES), jnp.float32),
                        pltpu.VMEM((tm, LANES), jnp.float32)],
        compiler_params=_cparams(("parallel", "parallel")),
        name="inproj",
    )(x, mod6, w_nat, w_tr, cos, slo, shi)


def _dil_kernel(q_ref, kc_ref, kp_ref, vc_ref, vp_ref, o_ref, lse_ref, *, nblk):
    i = pl.program_id(2)
    blk_rows = BAND_BLOCK
    qi = lax.broadcasted_iota(jnp.int32, (blk_rows, 2 * blk_rows), 0)
    kj = lax.broadcasted_iota(jnp.int32, (blk_rows, 2 * blk_rows), 1)
    in_prev = kj < blk_rows
    bias = jnp.where(in_prev, jnp.where(kj >= qi, 0.0, NEG), jnp.where(kj - blk_rows <= qi, 0.0, NEG))
    bias0 = jnp.where(in_prev, jnp.where(i == 0, NEG, bias), bias)
    lane = lax.broadcasted_iota(jnp.int32, (1, LANES), 1)
    lo = lane < HEAD_DIM
    n_pair = WA // LANES
    for blk in range(nblk):
        rows = slice(blk * blk_rows, (blk + 1) * blk_rows)
        both = slice((blk - 1) * blk_rows, (blk + 1) * blk_rows)
        b = bias0 if blk == 0 else bias
        vals, ss = [], []
        for hp in range(n_pair):
            cols = slice(hp * LANES, (hp + 1) * LANES)
            qt = q_ref[rows, cols]
            if blk == 0:
                k2 = jnp.concatenate([kp_ref[:, cols], kc_ref[rows, cols]], axis=0)
                vals.append(jnp.concatenate([vp_ref[:, cols], vc_ref[rows, cols]], axis=0))
            else:
                k2 = kc_ref[both, cols]
                vals.append(vc_ref[both, cols])
            for h in range(2):
                qh = jnp.where(lo if h == 0 else jnp.logical_not(lo), qt, jnp.zeros_like(qt))
                ss.append(lax.dot_general(qh, k2, _NT, preferred_element_type=jnp.float32) + b)
        ms = [jnp.max(s, -1, keepdims=True) for s in ss]
        ps = [jnp.exp2(s - m) for s, m in zip(ss, ms)]
        dens = [jnp.sum(p, -1, keepdims=True) for p in ps]
        lse = jnp.zeros((blk_rows, LANES), jnp.float32)
        for hp in range(n_pair):
            cols = slice(hp * LANES, (hp + 1) * LANES)
            outs = []
            for h in range(2):
                n = 2 * hp + h
                o = jnp.dot(ps[n].astype(jnp.bfloat16), vals[hp], preferred_element_type=jnp.float32)
                outs.append(o * (1.0 / dens[n]))
                lse = jnp.where(lane // LSE_LANES == n, ms[n] + jnp.log2(dens[n]), lse)
            o_ref[rows, cols] = jnp.where(lo, outs[0], outs[1])
        lse_ref[rows, :] = lse


def _dilated(q, k, v, name):
    B, dil, L, W = q.shape
    rows = min(L, 8 * BAND_BLOCK)
    nblk = rows // BAND_BLOCK
    cur = pl.BlockSpec((None, None, rows, W), lambda b, r, i: (b, r, i, 0))
    cur_lse = pl.BlockSpec((None, None, rows, LANES), lambda b, r, i: (b, r, i, 0))
    prev = pl.BlockSpec((None, None, BAND_BLOCK, W), lambda b, r, i: (b, r, jnp.maximum(i * nblk - 1, 0), 0))
    return pl.pallas_call(
        functools.partial(_dil_kernel, nblk=nblk),
        grid=(B, dil, L // rows),
        in_specs=[cur, cur, prev, cur, prev],
        out_specs=[cur, cur_lse],
        out_shape=[jax.ShapeDtypeStruct((B, dil, L, W), jnp.float32),
                   jax.ShapeDtypeStruct((B, dil, L, LANES), jnp.float32)],
        compiler_params=_cparams(("parallel", "parallel", "arbitrary")),
        name=name,
    )(q, k, k, v, v)


def _cmp_kernel(ck_ref, cv_ref, pe_ref, wk1_ref, wk2_ref, wv1_ref, wv2t_ref, ok_ref, ovt_ref):
    n = ck_ref.shape[0] // CMP_STRIDE
    bf = jnp.bfloat16

    def hidden(c_ref, w1_ref):
        a = jnp.zeros((n, CMP_HIDDEN), jnp.float32)
        b = jnp.zeros((n, CMP_HIDDEN), jnp.float32)
        for j in range(CMP_STRIDE):
            t = c_ref[pl.ds(j, n, stride=CMP_STRIDE), :]
            a = a + jnp.dot((t + pe_ref[j:j + 1, :]).astype(bf), w1_ref[j], preferred_element_type=jnp.float32)
            b = b + jnp.dot((t + pe_ref[CMP_STRIDE + j:CMP_STRIDE + j + 1, :]).astype(bf), w1_ref[CMP_STRIDE + j],
                            preferred_element_type=jnp.float32)
        return _gelu_tanh(a + pltpu.roll(b, n - 1, 0)).astype(bf)

    ok_ref[...] = jnp.dot(hidden(ck_ref, wk1_ref), wk2_ref[...],
                          preferred_element_type=jnp.float32).astype(ok_ref.dtype)
    ovt_ref[...] = lax.dot_general(wv2t_ref[...], hidden(cv_ref, wv1_ref), _NT,
                                   preferred_element_type=jnp.float32).astype(ovt_ref.dtype)


def _compress(kc, vc, pe, w_ck1, w_ck2, w_cv1, w_cv2):
    B, S, _ = kc.shape
    n = S // CMP_STRIDE
    bf = jnp.bfloat16

    def slabs(w1):
        w = w1.reshape(CMP_LEN, HEAD_DIM, CMP_HIDDEN).astype(bf)
        z = jnp.zeros_like(w)
        return jnp.stack([jnp.concatenate([w, z], axis=1), jnp.concatenate([z, w], axis=1)], axis=0)

    seq = pl.BlockSpec((None, S, WG), lambda b, g: (b, 0, 0))
    full = lambda a: pl.BlockSpec(a.shape, lambda b, g: (0,) * a.ndim)
    slab = pl.BlockSpec((None, CMP_LEN, WG, CMP_HIDDEN), lambda b, g: (g, 0, 0, 0))
    pe2 = jnp.concatenate([pe, pe], axis=-1)
    args = (kc, vc, pe2, slabs(w_ck1), jnp.concatenate([w_ck2, w_ck2], axis=-1).astype(bf),
            slabs(w_cv1), w_cv2.T.astype(bf))
    return pl.pallas_call(
        _cmp_kernel,
        grid=(B, NSA_GROUPS),
        in_specs=[seq, seq, full(pe2), slab, full(args[4]), slab, full(args[6])],
        out_specs=[pl.BlockSpec((None, None, n, LANES), lambda b, g: (b, g, 0, 0)),
                   pl.BlockSpec((None, None, HEAD_DIM, n), lambda b, g: (b, g, 0, 0))],
        out_shape=[jax.ShapeDtypeStruct((B, NSA_GROUPS, n, LANES), bf),
                   jax.ShapeDtypeStruct((B, NSA_GROUPS, HEAD_DIM, n), bf)],
        compiler_params=_cparams(("parallel", "parallel")),
        name="compress",
    )(*args)


def _nsa_cmp_kernel(q_ref, kcc_ref, vcct_ref, ovt_ref, o_ref, sel_ref, *, tq, nsel, topk, span):
    qi = pl.program_id(2)
    t0 = qi * tq
    nc_all = kcc_ref.shape[0]
    n_var = max(1, (nc_all * CMP_STRIDE) // span)

    def body(nc, nb):
        tpos = t0 + lax.broadcasted_iota(jnp.int32, (nc, tq), 1)
        cend = lax.broadcasted_iota(jnp.int32, (nc, tq), 0) * CMP_STRIDE + (CMP_LEN - 1)
        cbias = jnp.where(cend <= tpos, 0.0, NEG)
        any_vis = (t0 + lax.broadcasted_iota(jnp.int32, (1, tq), 1)) >= CMP_LEN - 1
        kcc, vcct, ovt = kcc_ref[0:nc, :], vcct_ref[:, 0:nc], ovt_ref[0:nb, 0:nc]
        ss = [lax.dot_general(kcc, q_ref[:, h * LANES:(h + 1) * LANES], _NT,
                              preferred_element_type=jnp.float32) + cbias for h in range(NSA_HPG)]
        ms = [jnp.max(s, 0, keepdims=True) for s in ss]
        ps = [jnp.exp2(s - m).astype(jnp.bfloat16) for s, m in zip(ss, ms)]
        vaug = jnp.concatenate([vcct, jnp.ones((DEN_ROWS, nc), jnp.bfloat16)], axis=0)
        imp = jnp.zeros((nb, tq), jnp.float32)
        for h in range(NSA_HPG):
            oa = jnp.dot(vaug, ps[h], preferred_element_type=jnp.float32)
            inv = jnp.where(any_vis, 1.0 / oa[HEAD_DIM:HEAD_DIM + 1], 0.0)
            o_ref[h * HEAD_DIM:(h + 1) * HEAD_DIM, :] = oa[:HEAD_DIM] * inv
            imp = imp + jnp.dot(ovt, ps[h], preferred_element_type=jnp.float32) * inv
        blk = lax.broadcasted_iota(jnp.int32, (nb, tq), 0)
        tq_pos = t0 + lax.broadcasted_iota(jnp.int32, (nb, tq), 1)
        cur = tq_pos // SLC_BLOCK
        forced = (blk == cur) | (blk == cur - 1) | (blk == 0)
        valid = blk <= cur
        cand = valid & jnp.logical_not(forced) & (blk < nsel)
        score = jnp.where(cand, imp, -jnp.inf)
        blk_f = blk.astype(jnp.float32)
        for _ in range(topk - 3):
            mx = jnp.max(score, 0, keepdims=True)
            first = jnp.min(jnp.where(score == mx, blk_f, float(2 * LANES)), 0, keepdims=True)
            score = jnp.where(blk_f == first, -jnp.inf, score)
        sel = jnp.where(forced, 1.0, jnp.where(cand & (score == -jnp.inf), 1.0, 0.0))
        sel = jnp.where(valid, sel, 0.0)
        if nb < LANES:
            sel = jnp.concatenate([sel, jnp.zeros((LANES - nb, tq), jnp.float32)], axis=0)
        sel_ref[...] = ((sel.T - 1.0) * (-NEG)).astype(jnp.bfloat16)

    variant = jnp.minimum((t0 + tq - 1) // span, n_var - 1)
    for k in range(n_var):
        last_var = k == n_var - 1
        nc = nc_all if last_var else (k + 1) * span // CMP_STRIDE
        nb = LANES if last_var else (k + 1) * span // SLC_BLOCK
        pl.when(variant == k)(functools.partial(body, nc, nb))


def _overlap_t(S):
    nc = S // CMP_STRIDE
    ns = S // SLC_BLOCK
    cs = np.arange(nc) * CMP_STRIDE
    ss = np.arange(LANES) * SLC_BLOCK
    ov = np.clip(np.minimum(cs[None, :] + CMP_LEN, ss[:, None] + SLC_BLOCK)
                 - np.maximum(cs[None, :], ss[:, None]), 0, None).astype(np.float32) / CMP_LEN
    ov[ns:, :] = 0.0
    ov[:, nc - CMP_LEN // CMP_STRIDE + 1:] = 0.0
    return jnp.asarray(ov, jnp.bfloat16)


def _nsa_cmp(qbx, kcc, vcct, tq):
    B, S, _ = qbx.shape
    nc = kcc.shape[2]
    ns = S // SLC_BLOCK
    gw = NSA_HPG * LANES
    ovt = _overlap_t(S)
    return pl.pallas_call(
        functools.partial(_nsa_cmp_kernel, tq=tq, nsel=ns, topk=min(SLC_TOPK, ns), span=CMP_SPAN),
        grid=(B, NSA_GROUPS, S // tq),
        in_specs=[pl.BlockSpec((None, tq, gw), lambda b, g, i: (b, i, g)),
                  pl.BlockSpec((None, None, nc, LANES), lambda b, g, i: (b, g, 0, 0)),
                  pl.BlockSpec((None, None, HEAD_DIM, nc), lambda b, g, i: (b, g, 0, 0)),
                  pl.BlockSpec((LANES, nc), lambda b, g, i: (0, 0))],
        out_specs=[pl.BlockSpec((None, NSA_HPG * HEAD_DIM, tq), lambda b, g, i: (b, g, i)),
                   pl.BlockSpec((None, None, tq, LANES), lambda b, g, i: (b, g, i, 0))],
        out_shape=[jax.ShapeDtypeStruct((B, WB, S), jnp.float32),
                   jax.ShapeDtypeStruct((B, NSA_GROUPS, S, LANES), jnp.bfloat16)],
        compiler_params=_cparams(("parallel", "parallel", "parallel")),
        name="nsa_cmp",
    )(qbx, kcc, vcct, ovt)


def _nsa_main_kernel(q_ref, sel_ref, ksa_ref, vst_ref, kw_ref, vwt_ref, ocmp_ref, gate_ref, o_ref,
                     qa_scr, m_scr, acc_scr, win_scr, sa_scr, sb_scr, ma_scr, mb_scr, *, tq, tk):
    g = pl.program_id(1)
    qi = pl.program_id(2)
    t0 = qi * tq
    cols = NSA_HPG * tq
    bf = jnp.bfloat16
    selb = sel_ref[...]
    for h in range(NSA_HPG):
        qa_scr[h * tq:(h + 1) * tq, 0:LANES] = selb
        qa_scr[h * tq:(h + 1) * tq, LANES:2 * LANES] = q_ref[:, h * LANES:(h + 1) * LANES]
    m_scr[...] = jnp.full((1, cols), -1e37, jnp.float32)
    acc_scr[...] = jnp.zeros(acc_scr.shape, jnp.float32)

    def with_ones(vt):
        return jnp.concatenate([vt, jnp.ones((DEN_ROWS, vt.shape[1]), bf)], axis=0)

    def scores(tile, dst, dst_max):
        start = pl.multiple_of(tile * tk, tk)
        s = lax.dot_general(ksa_ref[pl.ds(start, tk), :], qa_scr[...], _NT,
                            preferred_element_type=jnp.float32)
        dst[...] = s
        dst_max[...] = jnp.max(s, 0, keepdims=True)

    def consume(tile, src, src_max):
        start = pl.multiple_of(tile * tk, tk)
        vt = with_ones(vst_ref[:, pl.ds(start, tk)])
        m_prev = m_scr[...]
        m_new = jnp.maximum(m_prev, src_max[...])
        alpha = jnp.exp2(m_prev - m_new)
        p = jnp.exp2(src[...] - m_new)
        acc_scr[...] = alpha * acc_scr[...] + jnp.dot(vt, p.astype(bf), preferred_element_type=jnp.float32)
        m_scr[...] = m_new

    last = t0 // tk

    wk = WIN + tq
    wstart = pl.multiple_of(jnp.maximum(t0 - WIN, 0), tq)
    sw = lax.dot_general(kw_ref[pl.ds(wstart, wk), :], qa_scr[:, LANES:2 * LANES], _NT,
                         preferred_element_type=jnp.float32)
    dist = ((lax.broadcasted_iota(jnp.int32, (wk, tq), 1) + t0)
            - (lax.broadcasted_iota(jnp.int32, (wk, tq), 0) + wstart))
    wb = jnp.where((dist >= 0) & (dist < WIN), 0.0, NEG)
    sw = sw + jnp.concatenate([wb] * NSA_HPG, axis=1)
    mw = jnp.max(sw, 0, keepdims=True)
    scores(0, sa_scr, ma_scr)
    pw = jnp.exp2(sw - mw).astype(bf)
    ow = jnp.dot(with_ones(vwt_ref[:, pl.ds(wstart, wk)]), pw, preferred_element_type=jnp.float32)
    win_scr[...] = ow[:HEAD_DIM] * (1.0 / ow[HEAD_DIM:HEAD_DIM + 1])

    def pair(first):
        scores(first + 1, sb_scr, mb_scr)
        consume(first, sa_scr, ma_scr)
        scores(first + 2, sa_scr, ma_scr)
        consume(first + 1, sb_scr, mb_scr)

    def quad_body(jj, carry):
        pair(4 * jj)
        pair(4 * jj + 2)
        return carry

    n_quad = last // 4
    lax.fori_loop(0, n_quad, quad_body, 0)

    @pl.when(last - 4 * n_quad >= 2)
    def _():
        pair(4 * n_quad)

    kk = lax.broadcasted_iota(jnp.int32, (tq, tq), 0)
    qq = lax.broadcasted_iota(jnp.int32, (tq, tq), 1)
    cb = jnp.where(kk <= qq, 0.0, NEG)
    cb = jnp.concatenate([cb] * NSA_HPG, axis=1)
    diag = pl.multiple_of(t0 - last * tk, tq)

    def finish(src):
        src[pl.ds(diag, tq), :] = src[pl.ds(diag, tq), :] + cb
        start = pl.multiple_of(last * tk, tk)
        m_prev = m_scr[...]
        m_new = jnp.maximum(m_prev, jnp.max(src[...], 0, keepdims=True))
        alpha = jnp.exp2(m_prev - m_new)
        p = jnp.exp2(src[...] - m_new).astype(bf)
        pv = jnp.dot(with_ones(vst_ref[:, pl.ds(start, tk)]), p, preferred_element_type=jnp.float32)
        acc_scr[...] = alpha * acc_scr[...] + pv

    @pl.when(last % 2 == 1)
    def _():
        scores(last, sb_scr, mb_scr)
        consume(last - 1, sa_scr, ma_scr)
        finish(sb_scr)

    @pl.when(last % 2 == 0)
    def _():
        finish(sa_scr)

    o_slc = acc_scr[0:HEAD_DIM, :] * (1.0 / acc_scr[HEAD_DIM:HEAD_DIM + 1, :])
    o_win = win_scr[...]

    base = g * (NSA_HPG * 3)
    heads = []
    for h in range(NSA_HPG):
        c = slice(h * tq, (h + 1) * tq)
        g_cmp = gate_ref[pl.ds(base + 3 * h, 1), :]
        g_slc = gate_ref[pl.ds(base + 3 * h + 1, 1), :]
        g_win = gate_ref[pl.ds(base + 3 * h + 2, 1), :]
        heads.append(g_cmp * ocmp_ref[h * HEAD_DIM:(h + 1) * HEAD_DIM, :] + g_slc * o_slc[:, c] + g_win * o_win[:, c])
    o_ref[...] = jnp.concatenate(heads, axis=0).T.astype(o_ref.dtype)


def _nsa_main(qbx, selb, ksa, vst, kw, vwt, o_cmp_t, gates_t, tq, tk):
    B, S, _ = qbx.shape
    gw = NSA_HPG * LANES
    ow = NSA_HPG * HEAD_DIM
    cols = NSA_HPG * tq
    seq = lambda w: pl.BlockSpec((None, S, w), lambda b, g, i: (b, 0, 0))
    seqt = pl.BlockSpec((None, HEAD_DIM, S), lambda b, g, i: (b, g, 0))
    return pl.pallas_call(
        functools.partial(_nsa_main_kernel, tq=tq, tk=tk),
        grid=(B, NSA_GROUPS, S // tq),
        in_specs=[pl.BlockSpec((None, tq, gw), lambda b, g, i: (b, i, g)),
                  pl.BlockSpec((None, None, tq, LANES), lambda b, g, i: (b, g, i, 0)),
                  seq(2 * LANES), seqt, seq(WG), seqt,
                  pl.BlockSpec((None, ow, tq), lambda b, g, i: (b, g, i)),
                  pl.BlockSpec((None, LANES, tq), lambda b, g, i: (b, 0, i))],
        out_specs=pl.BlockSpec((None, tq, ow), lambda b, g, i: (b, i, g)),
        out_shape=jax.ShapeDtypeStruct((B, S, WB), jnp.bfloat16),
        scratch_shapes=[pltpu.VMEM((cols, 2 * LANES), jnp.bfloat16),
                        pltpu.VMEM((1, cols), jnp.float32),
                        pltpu.VMEM((HEAD_DIM + DEN_ROWS, cols), jnp.float32),
                        pltpu.VMEM((HEAD_DIM, cols), jnp.float32),
                        pltpu.VMEM((tk, cols), jnp.float32),
                        pltpu.VMEM((tk, cols), jnp.float32),
                        pltpu.VMEM((1, cols), jnp.float32),
                        pltpu.VMEM((1, cols), jnp.float32)],
        compiler_params=_cparams(("parallel", "parallel", "arbitrary")),
        name="nsa_main",
    )(qbx, selb, ksa, vst, kw, vwt, o_cmp_t, gates_t)


def _outproj_kernel(x_ref, o1_ref, l1_ref, o4_ref, l4_ref, o16_ref, l16_ref, ob_ref, mod_ref, wo_ref, g_ref, b_ref,
                    ex_ref, y_ref, oa_scr, s0_scr, s1_scr, s2_scr, s3_scr, *, tm):
    def token_order(src, cols, scr):
        dil = src.shape[0]
        for r in range(dil):
            scr[pl.ds(r, tm // dil, stride=dil), :] = src[r, :, cols]
        return scr[...]

    l1 = l1_ref[...]
    l2 = token_order(l4_ref, slice(None), s0_scr)
    l3 = token_order(l16_ref, slice(None), s1_scr)
    m = jnp.maximum(jnp.maximum(l1, l2), l3)
    es = [jnp.exp2(l1 - m), jnp.exp2(l2 - m), jnp.exp2(l3 - m)]
    inv = 1.0 / (es[0] + es[1] + es[2])

    def widen(w):
        hi = w.astype(jnp.bfloat16)
        lo = (w - hi.astype(jnp.float32)).astype(jnp.bfloat16)
        return jnp.dot(jnp.concatenate([hi, lo], axis=1), ex_ref[...], preferred_element_type=jnp.float32)

    w1, w2, w3 = [widen(e * inv) for e in es]
    for c in range(WA // LANES):
        cols = slice(c * LANES, (c + 1) * LANES)
        o2 = token_order(o4_ref, cols, s2_scr)
        o3 = token_order(o16_ref, cols, s3_scr)
        o_a = w1[:, cols] * o1_ref[:, cols] + w2[:, cols] * o2 + w3[:, cols] * o3
        oa_scr[:, cols] = o_a.astype(jnp.bfloat16)
    y = (jnp.dot(oa_scr[...], wo_ref[:WA, :], preferred_element_type=jnp.float32)
         + jnp.dot(ob_ref[...], wo_ref[WA:, :], preferred_element_type=jnp.float32))
    z = ALPHA * x_ref[...] + mod_ref[2:3, :] * y
    y_ref[...] = _ln(z) * g_ref[...] + b_ref[...]


def _outproj(x, pats, o_b, mod6, wo_bf16, ln_g, ln_b, tm):
    B, S, D = x.shape
    bs = lambda w: pl.BlockSpec((None, tm, w), lambda b, i: (b, i, 0))
    res = lambda d, w: pl.BlockSpec((None, d, tm // d, w), lambda b, i: (b, 0, i, 0))
    res1 = lambda w: pl.BlockSpec((None, None, tm, w), lambda b, i: (b, 0, i, 0))
    vec = pl.BlockSpec((1, D), lambda b, i: (0, 0))
    (o1, l1), (o4, l4), (o16, l16) = pats
    scr = pltpu.VMEM((tm, LANES), jnp.float32)
    expand = np.zeros((LANES, WA), np.float32)
    expand[(np.arange(WA) // HEAD_DIM) * LSE_LANES, np.arange(WA)] = 1.0
    expand = jnp.asarray(np.concatenate([expand, expand], axis=0), jnp.bfloat16)
    return pl.pallas_call(
        functools.partial(_outproj_kernel, tm=tm),
        grid=(B, S // tm),
        in_specs=[bs(D), res1(WA), res1(LANES), res(4, WA), res(4, LANES), res(16, WA), res(16, LANES), bs(WB),
                  pl.BlockSpec((None, 6, D), lambda b, i: (b, 0, 0)),
                  pl.BlockSpec((WA + WB, D), lambda b, i: (0, 0)), vec, vec,
                  pl.BlockSpec((2 * LANES, WA), lambda b, i: (0, 0))],
        out_specs=bs(D),
        out_shape=jax.ShapeDtypeStruct((B, S, D), jnp.float32),
        scratch_shapes=[pltpu.VMEM((tm, WA), jnp.bfloat16), scr, scr, scr, scr],
        compiler_params=_cparams(("parallel", "parallel")),
        name="outproj",
    )(x, o1, l1, o4, l4, o16, l16, o_b, mod6, wo_bf16,
      ln_g.reshape(1, D), ln_b.reshape(1, D), expand)


def _ffn_kernel(x_ref, mod_ref, wup_ref, cw_ref, cb_ref, wdn_ref, g_ref, b_ref, y_ref,
                u_scr, buf_scr, carry_scr, acc_scr, *, tm, fc):
    i = pl.program_id(1)
    F = wdn_ref.shape[0]
    pad = 8

    @pl.when(i == 0)
    def _():
        carry_scr[...] = jnp.zeros_like(carry_scr)

    th = tm // 2
    n_chunk = F // fc
    halves = (slice(0, th), slice(th, tm))

    def prologue(rows):
        u_scr[rows, :] = (_ln(x_ref[rows, :]) * (1.0 + mod_ref[4:5, :]) + mod_ref[3:4, :]).astype(jnp.bfloat16)

    def up(rows, c):
        u = u_scr[rows, :]
        return (jnp.dot(u, wup_ref[:, c * fc:(c + 1) * fc], preferred_element_type=jnp.float32),
                jnp.dot(u, wup_ref[:, F + c * fc:F + (c + 1) * fc], preferred_element_type=jnp.float32))

    def chunk(rows, c, a_gate, a_val):
        cols = slice(c * fc, (c + 1) * fc)
        buf_scr[0:pad, :] = carry_scr[:, cols]
        buf_scr[pad:pad + th, :] = a_gate
        carry_scr[:, cols] = a_gate[th - pad:th, :]
        conv = (cw_ref[0:1, cols] * buf_scr[pad - 2:pad - 2 + th, :]
                + cw_ref[1:2, cols] * buf_scr[pad - 1:pad - 1 + th, :]
                + cw_ref[2:3, cols] * a_gate + cb_ref[:, cols])
        h = (_gelu_tanh(conv) * a_val).astype(jnp.bfloat16)
        d = jnp.dot(h, wdn_ref[cols, :], preferred_element_type=jnp.float32)
        if c == 0:
            acc_scr[rows, :] = d
        else:
            acc_scr[rows, :] += d

    def epilogue(rows):
        z = ALPHA * x_ref[rows, :] + mod_ref[5:6, :] * acc_scr[rows, :]
        y_ref[rows, :] = _ln(z) * g_ref[...] + b_ref[...]

    items = [(rows, c) for rows in halves for c in range(n_chunk)]
    prologue(halves[0])
    nxt = up(*items[0])
    for k, (rows, c) in enumerate(items):
        cur = nxt
        if k == 0:
            prologue(halves[1])
        if k + 1 < len(items):
            nxt = up(*items[k + 1])
        if k == n_chunk + 1:
            epilogue(halves[0])
        chunk(rows, c, *cur)
    epilogue(halves[1])


def _ffn(x, mod6, wup_bf16, conv_w, conv_b, wdn_bf16, ln_g, ln_b, tm, fc):
    B, S, D = x.shape
    F = wdn_bf16.shape[0]
    bs = pl.BlockSpec((None, tm, D), lambda b, i: (b, i, 0))
    full = lambda a: pl.BlockSpec(a.shape, lambda b, i: (0,) * a.ndim)
    args = (wup_bf16, conv_w, conv_b.reshape(1, F), wdn_bf16, ln_g.reshape(1, D), ln_b.reshape(1, D))
    return pl.pallas_call(
        functools.partial(_ffn_kernel, tm=tm, fc=fc),
        grid=(B, S // tm),
        in_specs=[bs, pl.BlockSpec((None, 6, D), lambda b, i: (b, 0, 0))] + [full(a) for a in args],
        out_specs=bs,
        out_shape=jax.ShapeDtypeStruct((B, S, D), jnp.float32),
        scratch_shapes=[pltpu.VMEM((tm, D), jnp.bfloat16),
                        pltpu.VMEM((tm // 2 + 8, fc), jnp.float32),
                        pltpu.VMEM((8, F), jnp.float32),
                        pltpu.VMEM((tm, D), jnp.float32)],
        compiler_params=_cparams(("arbitrary", "arbitrary")),
        name="ffn",
    )(x, mod6, *args)


def kernel(x, c, w_ada, b_ada, w_in, pe_cmp, w_ck1, w_ck2, w_cv1, w_cv2, w_o, ln1_g, ln1_b, w_up, conv_w, conv_b,
           w_down, ln2_g, ln2_b):
    B, S, D = x.shape
    tables = _rope_tables(S)
    tm = min(512, S)
    tq = 128
    tk = min(512, S)
    for l in range(DEPTH):
        mod6 = _ada(c, w_ada[l], b_ada[l]).reshape(B, 6, D)
        w_nat, w_tr = _prep_w_in(w_in[l])
        (qa, ka, va, qbx, kc, vc, ksa, kw, vst, vwt, gates_t,
         q4, k4, v4, q16, k16, v16) = _inproj(x, mod6, w_nat, w_tr, tables, tm)
        pats = [_dilated(qa, ka, va, "dilated1"),
                _dilated(q4, k4, v4, "dilated4"), _dilated(q16, k16, v16, "dilated16")]
        kcc, vcct = _compress(kc, vc, pe_cmp[l], w_ck1[l], w_ck2[l], w_cv1[l], w_cv2[l])
        o_cmp_t, selb = _nsa_cmp(qbx, kcc, vcct, 4 * tq)
        o_b = _nsa_main(qbx, selb, ksa, vst, kw, vwt, o_cmp_t, gates_t, 2 * tq, tk)
        x = _outproj(x, pats, o_b, mod6, w_o[l].astype(jnp.bfloat16), ln1_g[l], ln1_b[l], tm)
        x = _ffn(x, mod6, w_up[l].astype(jnp.bfloat16), conv_w[l], conv_b[l], w_down[l].astype(jnp.bfloat16),
                 ln2_g[l], ln2_b[l], tm, 256)
    return x
```

```python
import functools
import math

import numpy as np
import jax
import jax.numpy as jnp
from jax import lax
from jax.experimental import pallas as pl
from jax.experimental.pallas import tpu as pltpu

HEAD_DIM = 64
N_HEADS_A = 8
N_HEADS_B = 8
DIL_PATTERNS = ((128, 1), (512, 4), (2048, 16))
BAND_BLOCK = 128
ROT_DIM = HEAD_DIM // 4
ROPE_THETA = 500000.0
NSA_GROUPS = 2
NSA_HPG = N_HEADS_B // NSA_GROUPS
CMP_LEN = 32
CMP_STRIDE = 16
CMP_HIDDEN = 4 * HEAD_DIM
SLC_BLOCK = 64
SLC_TOPK = 16
WIN = 512
D_FF = 2816
CONV_W = 3
DEPTH = 1
ALPHA = (2 * DEPTH) ** 0.25
LN_EPS = 1e-5
NEG = -1e30
LOG2E = math.log2(math.e)

LANES = 128
MXU_N = 256
DEN_ROWS = 16
LSE_LANES = LANES // N_HEADS_A
CMP_SPAN = LANES * CMP_STRIDE
WA = N_HEADS_A * HEAD_DIM
WB = N_HEADS_B * HEAD_DIM
WG = NSA_GROUPS * HEAD_DIM
N_GATES = N_HEADS_B * 3
SRC_VS = 3 * WA + WB + 3 * WG
SRC_KW = SRC_VS + WG
SRC_VW = SRC_KW + WG
SRC_GL = SRC_VW + WG
OFF_QA, OFF_KA, OFF_VA, OFF_QB = 0, WA, 2 * WA, 3 * WA
OFF_KC = OFF_QB + WB
OFF_VC = OFF_KC + WG
OFF_KS = OFF_VC + WG
OFF_KW = OFF_KS + WG
NAT_WIDTH = OFF_KW + WG
TR_VS, TR_VW, TR_GL = 0, WG, 2 * WG
TR_WIDTH = 3 * WG

VMEM_LIMIT = 56 * 1024 * 1024

_NT = (((1,), (1,)), ((), ()))


def _cparams(sem):
    return pltpu.CompilerParams(dimension_semantics=sem, vmem_limit_bytes=VMEM_LIMIT)


def _gelu_tanh(x):
    return 0.5 * x * (1.0 + jnp.tanh(math.sqrt(2.0 / math.pi) * (x + 0.044715 * (x * x * x))))


def _ln(x):
    mu = jnp.mean(x, -1, keepdims=True)
    xc = x - mu
    var = jnp.mean(xc * xc, -1, keepdims=True)
    return xc * lax.rsqrt(var + LN_EPS)


def _ada_kernel(c_ref, w_ref, b_ref, o_ref):
    c = c_ref[...]
    a = c * jax.nn.sigmoid(c)
    o_ref[...] = jnp.dot(a, w_ref[...], preferred_element_type=jnp.float32) + b_ref[...]


def _ada(c, w, b):
    B, D = c.shape
    N = w.shape[1]
    tn = D
    return pl.pallas_call(
        _ada_kernel,
        grid=(N // tn,),
        in_specs=[pl.BlockSpec((B, D), lambda j: (0, 0)),
                  pl.BlockSpec((D, tn), lambda j: (0, j)),
                  pl.BlockSpec((1, tn), lambda j: (0, j))],
        out_specs=pl.BlockSpec((B, tn), lambda j: (0, j)),
        out_shape=jax.ShapeDtypeStruct((B, N), jnp.float32),
        compiler_params=_cparams(("arbitrary",)),
        name="adaln",
    )(c, w, b.reshape(1, N))


def _rope_tile(t, cos, s_lo, s_hi):
    half = ROT_DIM // 2
    return t * cos + pltpu.roll(t, half, 1) * s_hi + pltpu.roll(t, LANES - half, 1) * s_lo


def _inproj_kernel(x_ref, mod_ref, w_ref, wt_ref, cos_ref, slo_ref, shi_ref,
                   qa_ref, ka_ref, va_ref, qbx_ref, kc_ref, vc_ref, ksa_ref, kw_ref, vst_ref, vwt_ref, gate_ref,
                   q4_ref, k4_ref, v4_ref, q16_ref, k16_ref, v16_ref,
                   u_ref, ra_ref, rb_ref, *, tm):
    i = pl.program_id(1)
    x = x_ref[...]
    u = _ln(x) * (1.0 + mod_ref[1:2, :]) + mod_ref[0:1, :]
    u_ref[...] = u.astype(jnp.bfloat16)
    cos, slo, shi = cos_ref[...], slo_ref[...], shi_ref[...]
    lane = lax.broadcasted_iota(jnp.int32, (1, LANES), 1)
    bf = jnp.bfloat16

    def proj(off):
        r = jnp.dot(u_ref[...], w_ref[:, off:off + MXU_N], preferred_element_type=jnp.float32)
        return r[:, :LANES], r[:, LANES:]

    def rope(t):
        return _rope_tile(t, cos, slo, shi)

    def emit(t, cols, nat_ref, r4_ref, r16_ref):
        nat_ref[:, cols] = t.astype(bf)
        ra_ref[...] = t
        n4 = tm // 4
        for r in range(4):
            part = ra_ref[pl.ds(r, n4, stride=4), :]
            r4_ref[r, :, cols] = part.astype(bf)
            rb_ref[r * n4:(r + 1) * n4, :] = part
        for r in range(4):
            for m in range(4):
                part = rb_ref[pl.ds(r * n4 + m, n4 // 4, stride=4), :]
                r16_ref[r + 4 * m, :, cols] = part.astype(bf)

    for j in range(WA // MXU_N):
        for k, t in enumerate(proj(OFF_QA + j * MXU_N)):
            emit(rope(t) * LOG2E, slice((2 * j + k) * LANES, (2 * j + k + 1) * LANES), qa_ref, q4_ref, q16_ref)
        for k, t in enumerate(proj(OFF_KA + j * MXU_N)):
            emit(rope(t), slice((2 * j + k) * LANES, (2 * j + k + 1) * LANES), ka_ref, k4_ref, k16_ref)
        for k, t in enumerate(proj(OFF_VA + j * MXU_N)):
            emit(t, slice((2 * j + k) * LANES, (2 * j + k + 1) * LANES), va_ref, v4_ref, v16_ref)
    lo = lane < HEAD_DIM
    for j in range(WB // MXU_N):
        for k, t in enumerate(proj(OFF_QB + j * MXU_N)):
            pair = 2 * j + k
            t = rope(t) * LOG2E
            g = (2 * pair) // NSA_HPG
            tr = pltpu.roll(t, HEAD_DIM, 1)
            in_g = lo if g == 0 else jnp.logical_not(lo)
            even = jnp.where(in_g, t if g == 0 else tr, 0.0)
            odd = jnp.where(in_g, tr if g == 0 else t, 0.0)
            qbx_ref[:, (2 * pair) * LANES:(2 * pair + 1) * LANES] = even.astype(bf)
            qbx_ref[:, (2 * pair + 1) * LANES:(2 * pair + 2) * LANES] = odd.astype(bf)
    kc, vc = proj(OFF_KC)
    kc_ref[...] = rope(kc)
    vc_ref[...] = vc
    ks, kw = proj(OFF_KS)
    tpos = i * tm + lax.broadcasted_iota(jnp.int32, (tm, LANES), 0)
    blk = lax.broadcasted_iota(jnp.int32, (tm, LANES), 1)
    ksa_ref[:, 0:LANES] = jnp.where((tpos // SLC_BLOCK) == blk, 1.0, 0.0).astype(bf)
    ksa_ref[:, LANES:2 * LANES] = rope(ks).astype(bf)
    kw_ref[...] = rope(kw).astype(bf)
    tr_out = lax.dot_general(wt_ref[...], u_ref[...], _NT, preferred_element_type=jnp.float32)
    vst_ref[...] = tr_out[TR_VS:TR_VS + WG].astype(bf)
    vwt_ref[...] = tr_out[TR_VW:TR_VW + WG].astype(bf)
    gate_ref[...] = jax.nn.sigmoid(tr_out[TR_GL:TR_GL + LANES])


def _rope_tables(S):
    inv = 1.0 / (ROPE_THETA ** (jnp.arange(0, ROT_DIM, 2, dtype=jnp.float32) / ROT_DIM))
    ang = jnp.arange(S, dtype=jnp.float32)[:, None] * inv[None, :]
    cos, sin = jnp.cos(ang), jnp.sin(ang)
    half = ROT_DIM // 2
    d = np.arange(LANES) % HEAD_DIM
    idx = jnp.asarray(d % half)
    rot = jnp.asarray(d < ROT_DIM)
    lo = jnp.asarray(d < half)
    hi = jnp.asarray((d >= half) & (d < ROT_DIM))
    c_t = jnp.where(rot[None], cos[:, idx], 1.0)
    s_lo = jnp.where(lo[None], -sin[:, idx], 0.0)
    s_hi = jnp.where(hi[None], sin[:, idx], 0.0)
    return c_t, s_lo, s_hi


def _prep_w_in(w):
    q_scale = HEAD_DIM ** -0.5
    w_nat = jnp.concatenate([w[:, 0:WA] * q_scale, w[:, WA:3 * WA], w[:, 3 * WA:3 * WA + WB] * q_scale,
                             w[:, 3 * WA + WB:SRC_VS], w[:, SRC_KW:SRC_VW]], axis=1)
    w_tr = jnp.concatenate([w[:, SRC_VS:SRC_KW], w[:, SRC_VW:SRC_GL],
                            jnp.pad(w[:, SRC_GL:], ((0, 0), (0, LANES - N_GATES)))], axis=1)
    return w_nat.astype(jnp.bfloat16), w_tr.T.astype(jnp.bfloat16)


def _inproj(x, mod6, w_nat, w_tr, tables, tm):
    B, S, D = x.shape
    cos, slo, shi = tables
    bs = lambda w: pl.BlockSpec((None, tm, w), lambda b, i: (b, i, 0))
    bst = pl.BlockSpec((None, LANES, tm), lambda b, i: (b, 0, i))
    tab = pl.BlockSpec((tm, LANES), lambda b, i: (i, 0))
    sd = lambda w, dt: jax.ShapeDtypeStruct((B, S, w), dt)
    sdt = lambda dt: jax.ShapeDtypeStruct((B, LANES, S), dt)
    bf, f32 = jnp.bfloat16, jnp.float32
    res = lambda d: pl.BlockSpec((None, d, tm // d, WA), lambda b, i: (b, 0, i, 0))
    res1 = pl.BlockSpec((None, None, tm, WA), lambda b, i: (b, 0, i, 0))
    sdr = lambda d: jax.ShapeDtypeStruct((B, d, S // d, WA), bf)
    return pl.pallas_call(
        functools.partial(_inproj_kernel, tm=tm),
        grid=(B, S // tm),
        in_specs=[bs(D),
                  pl.BlockSpec((None, 6, D), lambda b, i: (b, 0, 0)),
                  pl.BlockSpec((D, NAT_WIDTH), lambda b, i: (0, 0)),
                  pl.BlockSpec((TR_WIDTH, D), lambda b, i: (0, 0)),
                  tab, tab, tab],
        out_specs=[res1, res1, res1, bs(2 * WB), bs(WG), bs(WG), bs(2 * LANES), bs(WG), bst, bst, bst]
                  + [res(4)] * 3 + [res(16)] * 3,
        out_shape=[sdr(1), sdr(1), sdr(1), sd(2 * WB, bf), sd(WG, f32), sd(WG, f32),
                   sd(2 * LANES, bf), sd(WG, bf), sdt(bf), sdt(bf), sdt(f32)]
                  + [sdr(4)] * 3 + [sdr(16)] * 3,
        scratch_shapes=[pltpu.VMEM((tm, D), jnp.bfloat16),
                        pltpu.VMEM((tm, LANES), jnp.float32),
                        pltpu.VMEM((tm, LANES), jnp.float32)],
        compiler_params=_cparams(("parallel", "parallel")),
        name="inproj",
    )(x, mod6, w_nat, w_tr, cos, slo, shi)


def _dil_kernel(q_ref, kc_ref, kp_ref, vc_ref, vp_ref, o_ref, lse_ref, *, nblk):
    i = pl.program_id(2)
    blk_rows = BAND_BLOCK
    qi = lax.broadcasted_iota(jnp.int32, (blk_rows, 2 * blk_rows), 0)
    kj = lax.broadcasted_iota(jnp.int32, (blk_rows, 2 * blk_rows), 1)
    in_prev = kj < blk_rows
    bias = jnp.where(in_prev, jnp.where(kj >= qi, 0.0, NEG), jnp.where(kj - blk_rows <= qi, 0.0, NEG))
    bias0 = jnp.where(in_prev, jnp.where(i == 0, NEG, bias), bias)
    lane = lax.broadcasted_iota(jnp.int32, (1, LANES), 1)
    lo = lane < HEAD_DIM
    n_pair = WA // LANES
    for blk in range(nblk):
        rows = slice(blk * blk_rows, (blk + 1) * blk_rows)
        both = slice((blk - 1) * blk_rows, (blk + 1) * blk_rows)
        b = bias0 if blk == 0 else bias
        vals, ss = [], []
        for hp in range(n_pair):
            cols = slice(hp * LANES, (hp + 1) * LANES)
            qt = q_ref[rows, cols]
            if blk == 0:
                k2 = jnp.concatenate([kp_ref[:, cols], kc_ref[rows, cols]], axis=0)
                vals.append(jnp.concatenate([vp_ref[:, cols], vc_ref[rows, cols]], axis=0))
            else:
                k2 = kc_ref[both, cols]
                vals.append(vc_ref[both, cols])
            for h in range(2):
                qh = jnp.where(lo if h == 0 else jnp.logical_not(lo), qt, jnp.zeros_like(qt))
                ss.append(lax.dot_general(qh, k2, _NT, preferred_element_type=jnp.float32) + b)
        ms = [jnp.max(s, -1, keepdims=True) for s in ss]
        ps = [jnp.exp2(s - m) for s, m in zip(ss, ms)]
        dens = [jnp.sum(p, -1, keepdims=True) for p in ps]
        lse = jnp.zeros((blk_rows, LANES), jnp.float32)
        for hp in range(n_pair):
            cols = slice(hp * LANES, (hp + 1) * LANES)
            outs = []
            for h in range(2):
                n = 2 * hp + h
                o = jnp.dot(ps[n].astype(jnp.bfloat16), vals[hp], preferred_element_type=jnp.float32)
                outs.append(o * (1.0 / dens[n]))
                lse = jnp.where(lane // LSE_LANES == n, ms[n] + jnp.log2(dens[n]), lse)
            o_ref[rows, cols] = jnp.where(lo, outs[0], outs[1])
        lse_ref[rows, :] = lse


def _dilated(q, k, v, name):
    B, dil, L, W = q.shape
    rows = min(L, 8 * BAND_BLOCK)
    nblk = rows // BAND_BLOCK
    cur = pl.BlockSpec((None, None, rows, W), lambda b, r, i: (b, r, i, 0))
    cur_lse = pl.BlockSpec((None, None, rows, LANES), lambda b, r, i: (b, r, i, 0))
    prev = pl.BlockSpec((None, None, BAND_BLOCK, W), lambda b, r, i: (b, r, jnp.maximum(i * nblk - 1, 0), 0))
    return pl.pallas_call(
        functools.partial(_dil_kernel, nblk=nblk),
        grid=(B, dil, L // rows),
        in_specs=[cur, cur, prev, cur, prev],
        out_specs=[cur, cur_lse],
        out_shape=[jax.ShapeDtypeStruct((B, dil, L, W), jnp.float32),
                   jax.ShapeDtypeStruct((B, dil, L, LANES), jnp.float32)],
        compiler_params=_cparams(("parallel", "parallel", "arbitrary")),
        name=name,
    )(q, k, k, v, v)


def _cmp_kernel(ck_ref, cv_ref, pe_ref, wk1_ref, wk2_ref, wv1_ref, wv2t_ref, ok_ref, ovt_ref):
    n = ck_ref.shape[0] // CMP_STRIDE
    bf = jnp.bfloat16

    def hidden(c_ref, w1_ref):
        a = jnp.zeros((n, CMP_HIDDEN), jnp.float32)
        b = jnp.zeros((n, CMP_HIDDEN), jnp.float32)
        for j in range(CMP_STRIDE):
            t = c_ref[pl.ds(j, n, stride=CMP_STRIDE), :]
            a = a + jnp.dot((t + pe_ref[j:j + 1, :]).astype(bf), w1_ref[j], preferred_element_type=jnp.float32)
            b = b + jnp.dot((t + pe_ref[CMP_STRIDE + j:CMP_STRIDE + j + 1, :]).astype(bf), w1_ref[CMP_STRIDE + j],
                            preferred_element_type=jnp.float32)
        return _gelu_tanh(a + pltpu.roll(b, n - 1, 0)).astype(bf)

    ok_ref[...] = jnp.dot(hidden(ck_ref, wk1_ref), wk2_ref[...],
                          preferred_element_type=jnp.float32).astype(ok_ref.dtype)
    ovt_ref[...] = lax.dot_general(wv2t_ref[...], hidden(cv_ref, wv1_ref), _NT,
                                   preferred_element_type=jnp.float32).astype(ovt_ref.dtype)


def _compress(kc, vc, pe, w_ck1, w_ck2, w_cv1, w_cv2):
    B, S, _ = kc.shape
    n = S // CMP_STRIDE
    bf = jnp.bfloat16

    def slabs(w1):
        w = w1.reshape(CMP_LEN, HEAD_DIM, CMP_HIDDEN).astype(bf)
        z = jnp.zeros_like(w)
        return jnp.stack([jnp.concatenate([w, z], axis=1), jnp.concatenate([z, w], axis=1)], axis=0)

    seq = pl.BlockSpec((None, S, WG), lambda b, g: (b, 0, 0))
    full = lambda a: pl.BlockSpec(a.shape, lambda b, g: (0,) * a.ndim)
    slab = pl.BlockSpec((None, CMP_LEN, WG, CMP_HIDDEN), lambda b, g: (g, 0, 0, 0))
    pe2 = jnp.concatenate([pe, pe], axis=-1)
    args = (kc, vc, pe2, slabs(w_ck1), jnp.concatenate([w_ck2, w_ck2], axis=-1).astype(bf),
            slabs(w_cv1), w_cv2.T.astype(bf))
    return pl.pallas_call(
        _cmp_kernel,
        grid=(B, NSA_GROUPS),
        in_specs=[seq, seq, full(pe2), slab, full(args[4]), slab, full(args[6])],
        out_specs=[pl.BlockSpec((None, None, n, LANES), lambda b, g: (b, g, 0, 0)),
                   pl.BlockSpec((None, None, HEAD_DIM, n), lambda b, g: (b, g, 0, 0))],
        out_shape=[jax.ShapeDtypeStruct((B, NSA_GROUPS, n, LANES), bf),
                   jax.ShapeDtypeStruct((B, NSA_GROUPS, HEAD_DIM, n), bf)],
        compiler_params=_cparams(("parallel", "parallel")),
        name="compress",
    )(*args)


def _nsa_cmp_kernel(q_ref, kcc_ref, vcct_ref, ovt_ref, o_ref, sel_ref, *, tq, nsel, topk, span):
    qi = pl.program_id(2)
    t0 = qi * tq
    nc_all = kcc_ref.shape[0]
    n_var = max(1, (nc_all * CMP_STRIDE) // span)

    def body(nc, nb):
        tpos = t0 + lax.broadcasted_iota(jnp.int32, (nc, tq), 1)
        cend = lax.broadcasted_iota(jnp.int32, (nc, tq), 0) * CMP_STRIDE + (CMP_LEN - 1)
        cbias = jnp.where(cend <= tpos, 0.0, NEG)
        any_vis = (t0 + lax.broadcasted_iota(jnp.int32, (1, tq), 1)) >= CMP_LEN - 1
        kcc, vcct, ovt = kcc_ref[0:nc, :], vcct_ref[:, 0:nc], ovt_ref[0:nb, 0:nc]
        ss = [lax.dot_general(kcc, q_ref[:, h * LANES:(h + 1) * LANES], _NT,
                              preferred_element_type=jnp.float32) + cbias for h in range(NSA_HPG)]
        ms = [jnp.max(s, 0, keepdims=True) for s in ss]
        ps = [jnp.exp2(s - m).astype(jnp.bfloat16) for s, m in zip(ss, ms)]
        vaug = jnp.concatenate([vcct, jnp.ones((DEN_ROWS, nc), jnp.bfloat16)], axis=0)
        imp = jnp.zeros((nb, tq), jnp.float32)
        for h in range(NSA_HPG):
            oa = jnp.dot(vaug, ps[h], preferred_element_type=jnp.float32)
            inv = jnp.where(any_vis, 1.0 / oa[HEAD_DIM:HEAD_DIM + 1], 0.0)
            o_ref[h * HEAD_DIM:(h + 1) * HEAD_DIM, :] = oa[:HEAD_DIM] * inv
            imp = imp + jnp.dot(ovt, ps[h], preferred_element_type=jnp.float32) * inv
        blk = lax.broadcasted_iota(jnp.int32, (nb, tq), 0)
        tq_pos = t0 + lax.broadcasted_iota(jnp.int32, (nb, tq), 1)
        cur = tq_pos // SLC_BLOCK
        forced = (blk == cur) | (blk == cur - 1) | (blk == 0)
        valid = blk <= cur
        cand = valid & jnp.logical_not(forced) & (blk < nsel)
        score = jnp.where(cand, imp, -jnp.inf)
        blk_f = blk.astype(jnp.float32)
        for _ in range(topk - 3):
            mx = jnp.max(score, 0, keepdims=True)
            first = jnp.min(jnp.where(score == mx, blk_f, float(2 * LANES)), 0, keepdims=True)
            score = jnp.where(blk_f == first, -jnp.inf, score)
        sel = jnp.where(forced, 1.0, jnp.where(cand & (score == -jnp.inf), 1.0, 0.0))
        sel = jnp.where(valid, sel, 0.0)
        if nb < LANES:
            sel = jnp.concatenate([sel, jnp.zeros((LANES - nb, tq), jnp.float32)], axis=0)
        sel_ref[...] = ((sel.T - 1.0) * (-NEG)).astype(jnp.bfloat16)

    variant = jnp.minimum((t0 + tq - 1) // span, n_var - 1)
    for k in range(n_var):
        last_var = k == n_var - 1
        nc = nc_all if last_var else (k + 1) * span // CMP_STRIDE
        nb = LANES if last_var else (k + 1) * span // SLC_BLOCK
        pl.when(variant == k)(functools.partial(body, nc, nb))


def _overlap_t(S):
    nc = S // CMP_STRIDE
    ns = S // SLC_BLOCK
    cs = np.arange(nc) * CMP_STRIDE
    ss = np.arange(LANES) * SLC_BLOCK
    ov = np.clip(np.minimum(cs[None, :] + CMP_LEN, ss[:, None] + SLC_BLOCK)
                 - np.maximum(cs[None, :], ss[:, None]), 0, None).astype(np.float32) / CMP_LEN
    ov[ns:, :] = 0.0
    ov[:, nc - CMP_LEN // CMP_STRIDE + 1:] = 0.0
    return jnp.asarray(ov, jnp.bfloat16)


def _nsa_cmp(qbx, kcc, vcct, tq):
    B, S, _ = qbx.shape
    nc = kcc.shape[2]
    ns = S // SLC_BLOCK
    gw = NSA_HPG * LANES
    ovt = _overlap_t(S)
    return pl.pallas_call(
        functools.partial(_nsa_cmp_kernel, tq=tq, nsel=ns, topk=min(SLC_TOPK, ns), span=CMP_SPAN),
        grid=(B, NSA_GROUPS, S // tq),
        in_specs=[pl.BlockSpec((None, tq, gw), lambda b, g, i: (b, i, g)),
                  pl.BlockSpec((None, None, nc, LANES), lambda b, g, i: (b, g, 0, 0)),
                  pl.BlockSpec((None, None, HEAD_DIM, nc), lambda b, g, i: (b, g, 0, 0)),
                  pl.BlockSpec((LANES, nc), lambda b, g, i: (0, 0))],
        out_specs=[pl.BlockSpec((None, NSA_HPG * HEAD_DIM, tq), lambda b, g, i: (b, g, i)),
                   pl.BlockSpec((None, None, tq, LANES), lambda b, g, i: (b, g, i, 0))],
        out_shape=[jax.ShapeDtypeStruct((B, WB, S), jnp.float32),
                   jax.ShapeDtypeStruct((B, NSA_GROUPS, S, LANES), jnp.bfloat16)],
        compiler_params=_cparams(("parallel", "parallel", "parallel")),
        name="nsa_cmp",
    )(qbx, kcc, vcct, ovt)


def _nsa_main_kernel(q_ref, sel_ref, ksa_ref, vst_ref, kw_ref, vwt_ref, ocmp_ref, gate_ref, o_ref,
                     qa_scr, m_scr, acc_scr, win_scr, sa_scr, sb_scr, ma_scr, mb_scr, *, tq, tk):
    g = pl.program_id(1)
    qi = pl.program_id(2)
    t0 = qi * tq
    cols = NSA_HPG * tq
    bf = jnp.bfloat16
    selb = sel_ref[...]
    for h in range(NSA_HPG):
        qa_scr[h * tq:(h + 1) * tq, 0:LANES] = selb
        qa_scr[h * tq:(h + 1) * tq, LANES:2 * LANES] = q_ref[:, h * LANES:(h + 1) * LANES]
    m_scr[...] = jnp.full((1, cols), -1e37, jnp.float32)
    acc_scr[...] = jnp.zeros(acc_scr.shape, jnp.float32)

    def with_ones(vt):
        return jnp.concatenate([vt, jnp.ones((DEN_ROWS, vt.shape[1]), bf)], axis=0)

    def scores(tile, dst, dst_max):
        start = pl.multiple_of(tile * tk, tk)
        s = lax.dot_general(ksa_ref[pl.ds(start, tk), :], qa_scr[...], _NT,
                            preferred_element_type=jnp.float32)
        dst[...] = s
        dst_max[...] = jnp.max(s, 0, keepdims=True)

    def consume(tile, src, src_max):
        start = pl.multiple_of(tile * tk, tk)
        vt = with_ones(vst_ref[:, pl.ds(start, tk)])
        m_prev = m_scr[...]
        m_new = jnp.maximum(m_prev, src_max[...])
        alpha = jnp.exp2(m_prev - m_new)
        p = jnp.exp2(src[...] - m_new)
        acc_scr[...] = alpha * acc_scr[...] + jnp.dot(vt, p.astype(bf), preferred_element_type=jnp.float32)
        m_scr[...] = m_new

    last = t0 // tk

    wk = WIN + tq
    wstart = pl.multiple_of(jnp.maximum(t0 - WIN, 0), tq)
    sw = lax.dot_general(kw_ref[pl.ds(wstart, wk), :], qa_scr[:, LANES:2 * LANES], _NT,
                         preferred_element_type=jnp.float32)
    dist = ((lax.broadcasted_iota(jnp.int32, (wk, tq), 1) + t0)
            - (lax.broadcasted_iota(jnp.int32, (wk, tq), 0) + wstart))
    wb = jnp.where((dist >= 0) & (dist < WIN), 0.0, NEG)
    sw = sw + jnp.concatenate([wb] * NSA_HPG, axis=1)
    mw = jnp.max(sw, 0, keepdims=True)
    scores(0, sa_scr, ma_scr)
    pw = jnp.exp2(sw - mw).astype(bf)
    ow = jnp.dot(with_ones(vwt_ref[:, pl.ds(wstart, wk)]), pw, preferred_element_type=jnp.float32)
    win_scr[...] = ow[:HEAD_DIM] * (1.0 / ow[HEAD_DIM:HEAD_DIM + 1])

    def pair(first):
        scores(first + 1, sb_scr, mb_scr)
        consume(first, sa_scr, ma_scr)
        scores(first + 2, sa_scr, ma_scr)
        consume(first + 1, sb_scr, mb_scr)

    def quad_body(jj, carry):
        pair(4 * jj)
        pair(4 * jj + 2)
        return carry

    n_quad = last // 4
    lax.fori_loop(0, n_quad, quad_body, 0)

    @pl.when(last - 4 * n_quad >= 2)
    def _():
        pair(4 * n_quad)

    kk = lax.broadcasted_iota(jnp.int32, (tq, tq), 0)
    qq = lax.broadcasted_iota(jnp.int32, (tq, tq), 1)
    cb = jnp.where(kk <= qq, 0.0, NEG)
    cb = jnp.concatenate([cb] * NSA_HPG, axis=1)
    diag = pl.multiple_of(t0 - last * tk, tq)

    def finish(src):
        src[pl.ds(diag, tq), :] = src[pl.ds(diag, tq), :] + cb
        start = pl.multiple_of(last * tk, tk)
        m_prev = m_scr[...]
        m_new = jnp.maximum(m_prev, jnp.max(src[...], 0, keepdims=True))
        alpha = jnp.exp2(m_prev - m_new)
        p = jnp.exp2(src[...] - m_new).astype(bf)
        pv = jnp.dot(with_ones(vst_ref[:, pl.ds(start, tk)]), p, preferred_element_type=jnp.float32)
        acc_scr[...] = alpha * acc_scr[...] + pv

    @pl.when(last % 2 == 1)
    def _():
        scores(last, sb_scr, mb_scr)
        consume(last - 1, sa_scr, ma_scr)
        finish(sb_scr)

    @pl.when(last % 2 == 0)
    def _():
        finish(sa_scr)

    o_slc = acc_scr[0:HEAD_DIM, :] * (1.0 / acc_scr[HEAD_DIM:HEAD_DIM + 1, :])
    o_win = win_scr[...]

    base = g * (NSA_HPG * 3)
    heads = []
    for h in range(NSA_HPG):
        c = slice(h * tq, (h + 1) * tq)
        g_cmp = gate_ref[pl.ds(base + 3 * h, 1), :]
        g_slc = gate_ref[pl.ds(base + 3 * h + 1, 1), :]
        g_win = gate_ref[pl.ds(base + 3 * h + 2, 1), :]
        heads.append(g_cmp * ocmp_ref[h * HEAD_DIM:(h + 1) * HEAD_DIM, :] + g_slc * o_slc[:, c] + g_win * o_win[:, c])
    o_ref[...] = jnp.concatenate(heads, axis=0).T.astype(o_ref.dtype)


def _nsa_main(qbx, selb, ksa, vst, kw, vwt, o_cmp_t, gates_t, tq, tk):
    B, S, _ = qbx.shape
    gw = NSA_HPG * LANES
    ow = NSA_HPG * HEAD_DIM
    cols = NSA_HPG * tq
    seq = lambda w: pl.BlockSpec((None, S, w), lambda b, g, i: (b, 0, 0))
    seqt = pl.BlockSpec((None, HEAD_DIM, S), lambda b, g, i: (b, g, 0))
    return pl.pallas_call(
        functools.partial(_nsa_main_kernel, tq=tq, tk=tk),
        grid=(B, NSA_GROUPS, S // tq),
        in_specs=[pl.BlockSpec((None, tq, gw), lambda b, g, i: (b, i, g)),
                  pl.BlockSpec((None, None, tq, LANES), lambda b, g, i: (b, g, i, 0)),
                  seq(2 * LANES), seqt, seq(WG), seqt,
                  pl.BlockSpec((None, ow, tq), lambda b, g, i: (b, g, i)),
                  pl.BlockSpec((None, LANES, tq), lambda b, g, i: (b, 0, i))],
        out_specs=pl.BlockSpec((None, tq, ow), lambda b, g, i: (b, i, g)),
        out_shape=jax.ShapeDtypeStruct((B, S, WB), jnp.bfloat16),
        scratch_shapes=[pltpu.VMEM((cols, 2 * LANES), jnp.bfloat16),
                        pltpu.VMEM((1, cols), jnp.float32),
                        pltpu.VMEM((HEAD_DIM + DEN_ROWS, cols), jnp.float32),
                        pltpu.VMEM((HEAD_DIM, cols), jnp.float32),
                        pltpu.VMEM((tk, cols), jnp.float32),
                        pltpu.VMEM((tk, cols), jnp.float32),
                        pltpu.VMEM((1, cols), jnp.float32),
                        pltpu.VMEM((1, cols), jnp.float32)],
        compiler_params=_cparams(("parallel", "parallel", "arbitrary")),
        name="nsa_main",
    )(qbx, selb, ksa, vst, kw, vwt, o_cmp_t, gates_t)


def _outproj_kernel(x_ref, o1_ref, l1_ref, o4_ref, l4_ref, o16_ref, l16_ref, ob_ref, mod_ref, wo_ref, g_ref, b_ref,
                    ex_ref, y_ref, oa_scr, s0_scr, s1_scr, s2_scr, s3_scr, *, tm):
    def token_order(src, cols, scr):
        dil = src.shape[0]
        for r in range(dil):
            scr[pl.ds(r, tm // dil, stride=dil), :] = src[r, :, cols]
        return scr[...]

    l1 = l1_ref[...]
    l2 = token_order(l4_ref, slice(None), s0_scr)
    l3 = token_order(l16_ref, slice(None), s1_scr)
    m = jnp.maximum(jnp.maximum(l1, l2), l3)
    es = [jnp.exp2(l1 - m), jnp.exp2(l2 - m), jnp.exp2(l3 - m)]
    inv = 1.0 / (es[0] + es[1] + es[2])

    def widen(w):
        hi = w.astype(jnp.bfloat16)
        lo = (w - hi.astype(jnp.float32)).astype(jnp.bfloat16)
        return jnp.dot(jnp.concatenate([hi, lo], axis=1), ex_ref[...], preferred_element_type=jnp.float32)

    w1, w2, w3 = [widen(e * inv) for e in es]
    for c in range(WA // LANES):
        cols = slice(c * LANES, (c + 1) * LANES)
        o2 = token_order(o4_ref, cols, s2_scr)
        o3 = token_order(o16_ref, cols, s3_scr)
        o_a = w1[:, cols] * o1_ref[:, cols] + w2[:, cols] * o2 + w3[:, cols] * o3
        oa_scr[:, cols] = o_a.astype(jnp.bfloat16)
    y = (jnp.dot(oa_scr[...], wo_ref[:WA, :], preferred_element_type=jnp.float32)
         + jnp.dot(ob_ref[...], wo_ref[WA:, :], preferred_element_type=jnp.float32))
    z = ALPHA * x_ref[...] + mod_ref[2:3, :] * y
    y_ref[...] = _ln(z) * g_ref[...] + b_ref[...]


def _outproj(x, pats, o_b, mod6, wo_bf16, ln_g, ln_b, tm):
    B, S, D = x.shape
    bs = lambda w: pl.BlockSpec((None, tm, w), lambda b, i: (b, i, 0))
    res = lambda d, w: pl.BlockSpec((None, d, tm // d, w), lambda b, i: (b, 0, i, 0))
    res1 = lambda w: pl.BlockSpec((None, None, tm, w), lambda b, i: (b, 0, i, 0))
    vec = pl.BlockSpec((1, D), lambda b, i: (0, 0))
    (o1, l1), (o4, l4), (o16, l16) = pats
    scr = pltpu.VMEM((tm, LANES), jnp.float32)
    expand = np.zeros((LANES, WA), np.float32)
    expand[(np.arange(WA) // HEAD_DIM) * LSE_LANES, np.arange(WA)] = 1.0
    expand = jnp.asarray(np.concatenate([expand, expand], axis=0), jnp.bfloat16)
    return pl.pallas_call(
        functools.partial(_outproj_kernel, tm=tm),
        grid=(B, S // tm),
        in_specs=[bs(D), res1(WA), res1(LANES), res(4, WA), res(4, LANES), res(16, WA), res(16, LANES), bs(WB),
                  pl.BlockSpec((None, 6, D), lambda b, i: (b, 0, 0)),
                  pl.BlockSpec((WA + WB, D), lambda b, i: (0, 0)), vec, vec,
                  pl.BlockSpec((2 * LANES, WA), lambda b, i: (0, 0))],
        out_specs=bs(D),
        out_shape=jax.ShapeDtypeStruct((B, S, D), jnp.float32),
        scratch_shapes=[pltpu.VMEM((tm, WA), jnp.bfloat16), scr, scr, scr, scr],
        compiler_params=_cparams(("parallel", "parallel")),
        name="outproj",
    )(x, o1, l1, o4, l4, o16, l16, o_b, mod6, wo_bf16,
      ln_g.reshape(1, D), ln_b.reshape(1, D), expand)


def _ffn_kernel(x_ref, mod_ref, wup_ref, cw_ref, cb_ref, wdn_ref, g_ref, b_ref, y_ref,
                u_scr, buf_scr, carry_scr, acc_scr, *, tm, fc):
    i = pl.program_id(1)
    F = wdn_ref.shape[0]
    pad = 8

    @pl.when(i == 0)
    def _():
        carry_scr[...] = jnp.zeros_like(carry_scr)

    th = tm // 2
    n_chunk = F // fc
    halves = (slice(0, th), slice(th, tm))

    def prologue(rows):
        u_scr[rows, :] = (_ln(x_ref[rows, :]) * (1.0 + mod_ref[4:5, :]) + mod_ref[3:4, :]).astype(jnp.bfloat16)

    def up(rows, c):
        u = u_scr[rows, :]
        return (jnp.dot(u, wup_ref[:, c * fc:(c + 1) * fc], preferred_element_type=jnp.float32),
                jnp.dot(u, wup_ref[:, F + c * fc:F + (c + 1) * fc], preferred_element_type=jnp.float32))

    def chunk(rows, c, a_gate, a_val):
        cols = slice(c * fc, (c + 1) * fc)
        buf_scr[0:pad, :] = carry_scr[:, cols]
        buf_scr[pad:pad + th, :] = a_gate
        carry_scr[:, cols] = a_gate[th - pad:th, :]
        conv = (cw_ref[0:1, cols] * buf_scr[pad - 2:pad - 2 + th, :]
                + cw_ref[1:2, cols] * buf_scr[pad - 1:pad - 1 + th, :]
                + cw_ref[2:3, cols] * a_gate + cb_ref[:, cols])
        h = (_gelu_tanh(conv) * a_val).astype(jnp.bfloat16)
        d = jnp.dot(h, wdn_ref[cols, :], preferred_element_type=jnp.float32)
        if c == 0:
            acc_scr[rows, :] = d
        else:
            acc_scr[rows, :] += d

    def epilogue(rows):
        z = ALPHA * x_ref[rows, :] + mod_ref[5:6, :] * acc_scr[rows, :]
        y_ref[rows, :] = _ln(z) * g_ref[...] + b_ref[...]

    items = [(rows, c) for rows in halves for c in range(n_chunk)]
    prologue(halves[0])
    nxt = up(*items[0])
    for k, (rows, c) in enumerate(items):
        cur = nxt
        if k == 0:
            prologue(halves[1])
        if k + 1 < len(items):
            nxt = up(*items[k + 1])
        if k == n_chunk + 1:
            epilogue(halves[0])
        chunk(rows, c, *cur)
    epilogue(halves[1])


def _ffn(x, mod6, wup_bf16, conv_w, conv_b, wdn_bf16, ln_g, ln_b, tm, fc):
    B, S, D = x.shape
    F = wdn_bf16.shape[0]
    bs = pl.BlockSpec((None, tm, D), lambda b, i: (b, i, 0))
    full = lambda a: pl.BlockSpec(a.shape, lambda b, i: (0,) * a.ndim)
    args = (wup_bf16, conv_w, conv_b.reshape(1, F), wdn_bf16, ln_g.reshape(1, D), ln_b.reshape(1, D))
    return pl.pallas_call(
        functools.partial(_ffn_kernel, tm=tm, fc=fc),
        grid=(B, S // tm),
        in_specs=[bs, pl.BlockSpec((None, 6, D), lambda b, i: (b, 0, 0))] + [full(a) for a in args],
        out_specs=bs,
        out_shape=jax.ShapeDtypeStruct((B, S, D), jnp.float32),
        scratch_shapes=[pltpu.VMEM((tm, D), jnp.bfloat16),
                        pltpu.VMEM((tm // 2 + 8, fc), jnp.float32),
                        pltpu.VMEM((8, F), jnp.float32),
                        pltpu.VMEM((tm, D), jnp.float32)],
        compiler_params=_cparams(("arbitrary", "arbitrary")),
        name="ffn",
    )(x, mod6, *args)


def kernel(x, c, w_ada, b_ada, w_in, pe_cmp, w_ck1, w_ck2, w_cv1, w_cv2, w_o, ln1_g, ln1_b, w_up, conv_w, conv_b,
           w_down, ln2_g, ln2_b):
    B, S, D = x.shape
    tables = _rope_tables(S)
    tm = min(512, S)
    tq = 128
    tk = min(512, S)
    for l in range(DEPTH):
        mod6 = _ada(c, w_ada[l], b_ada[l]).reshape(B, 6, D)
        w_nat, w_tr = _prep_w_in(w_in[l])
        (qa, ka, va, qbx, kc, vc, ksa, kw, vst, vwt, gates_t,
         q4, k4, v4, q16, k16, v16) = _inproj(x, mod6, w_nat, w_tr, tables, tm)
        pats = [_dilated(qa, ka, va, "dilated1"),
                _dilated(q4, k4, v4, "dilated4"), _dilated(q16, k16, v16, "dilated16")]
        kcc, vcct = _compress(kc, vc, pe_cmp[l], w_ck1[l], w_ck2[l], w_cv1[l], w_cv2[l])
        o_cmp_t, selb = _nsa_cmp(qbx, kcc, vcct, 4 * tq)
        o_b = _nsa_main(qbx, selb, ksa, vst, kw, vwt, o_cmp_t, gates_t, 4 * tq, tk)
        x = _outproj(x, pats, o_b, mod6, w_o[l].astype(jnp.bfloat16), ln1_g[l], ln1_b[l], tm)
        x = _ffn(x, mod6, w_up[l].astype(jnp.bfloat16), conv_w[l], conv_b[l], w_down[l].astype(jnp.bfloat16),
                 ln2_g[l], ln2_b[l], tm, 256)
    return x
```

```python
import functools
import math

import numpy as np
import jax
import jax.numpy as jnp
from jax import lax
from jax.experimental import pallas as pl
from jax.experimental.pallas import tpu as pltpu

HEAD_DIM = 64
N_HEADS_A = 8
N_HEADS_B = 8
DIL_PATTERNS = ((128, 1), (512, 4), (2048, 16))
BAND_BLOCK = 128
ROT_DIM = HEAD_DIM // 4
ROPE_THETA = 500000.0
NSA_GROUPS = 2
NSA_HPG = N_HEADS_B // NSA_GROUPS
CMP_LEN = 32
CMP_STRIDE = 16
CMP_HIDDEN = 4 * HEAD_DIM
SLC_BLOCK = 64
SLC_TOPK = 16
WIN = 512
D_FF = 2816
CONV_W = 3
DEPTH = 1
ALPHA = (2 * DEPTH) ** 0.25
LN_EPS = 1e-5
NEG = -1e30
LOG2E = math.log2(math.e)

LANES = 128
MXU_N = 256
DEN_ROWS = 16
LSE_LANES = LANES // N_HEADS_A
CMP_SPAN = LANES * CMP_STRIDE
WA = N_HEADS_A * HEAD_DIM
WB = N_HEADS_B * HEAD_DIM
WG = NSA_GROUPS * HEAD_DIM
N_GATES = N_HEADS_B * 3
SRC_VS = 3 * WA + WB + 3 * WG
SRC_KW = SRC_VS + WG
SRC_VW = SRC_KW + WG
SRC_GL = SRC_VW + WG
OFF_QA, OFF_KA, OFF_VA, OFF_QB = 0, WA, 2 * WA, 3 * WA
OFF_KC = OFF_QB + WB
OFF_VC = OFF_KC + WG
OFF_KS = OFF_VC + WG
OFF_KW = OFF_KS + WG
NAT_WIDTH = OFF_KW + WG
TR_VS, TR_VW, TR_GL = 0, WG, 2 * WG
TR_WIDTH = 3 * WG

VMEM_LIMIT = 56 * 1024 * 1024

_NT = (((1,), (1,)), ((), ()))


def _cparams(sem):
    return pltpu.CompilerParams(dimension_semantics=sem, vmem_limit_bytes=VMEM_LIMIT)


def _gelu_tanh(x):
    return 0.5 * x * (1.0 + jnp.tanh(math.sqrt(2.0 / math.pi) * (x + 0.044715 * (x * x * x))))


def _ln(x):
    mu = jnp.mean(x, -1, keepdims=True)
    xc = x - mu
    var = jnp.mean(xc * xc, -1, keepdims=True)
    return xc * lax.rsqrt(var + LN_EPS)


def _ada_kernel(c_ref, w_ref, b_ref, o_ref):
    c = c_ref[...]
    a = c * jax.nn.sigmoid(c)
    o_ref[...] = jnp.dot(a, w_ref[...], preferred_element_type=jnp.float32) + b_ref[...]


def _ada(c, w, b):
    B, D = c.shape
    N = w.shape[1]
    tn = D
    return pl.pallas_call(
        _ada_kernel,
        grid=(N // tn,),
        in_specs=[pl.BlockSpec((B, D), lambda j: (0, 0)),
                  pl.BlockSpec((D, tn), lambda j: (0, j)),
                  pl.BlockSpec((1, tn), lambda j: (0, j))],
        out_specs=pl.BlockSpec((B, tn), lambda j: (0, j)),
        out_shape=jax.ShapeDtypeStruct((B, N), jnp.float32),
        compiler_params=_cparams(("arbitrary",)),
        name="adaln",
    )(c, w, b.reshape(1, N))


def _rope_tile(t, cos, s_lo, s_hi):
    half = ROT_DIM // 2
    return t * cos + pltpu.roll(t, half, 1) * s_hi + pltpu.roll(t, LANES - half, 1) * s_lo


def _inproj_kernel(x_ref, mod_ref, w_ref, wt_ref, cos_ref, slo_ref, shi_ref,
                   qa_ref, ka_ref, va_ref, qbx_ref, kc_ref, vc_ref, ksa_ref, kw_ref, vst_ref, vwt_ref, gate_ref,
                   q4_ref, k4_ref, v4_ref, q16_ref, k16_ref, v16_ref,
                   u_ref, ra_ref, rb_ref, *, tm):
    i = pl.program_id(1)
    x = x_ref[...]
    u = _ln(x) * (1.0 + mod_ref[1:2, :]) + mod_ref[0:1, :]
    u_ref[...] = u.astype(jnp.bfloat16)
    cos, slo, shi = cos_ref[...], slo_ref[...], shi_ref[...]
    lane = lax.broadcasted_iota(jnp.int32, (1, LANES), 1)
    bf = jnp.bfloat16

    def proj(off):
        r = jnp.dot(u_ref[...], w_ref[:, off:off + MXU_N], preferred_element_type=jnp.float32)
        return r[:, :LANES], r[:, LANES:]

    def rope(t):
        return _rope_tile(t, cos, slo, shi)

    def emit(t, cols, nat_ref, r4_ref, r16_ref):
        nat_ref[:, cols] = t.astype(bf)
        ra_ref[...] = t
        n4 = tm // 4
        for r in range(4):
            part = ra_ref[pl.ds(r, n4, stride=4), :]
            r4_ref[r, :, cols] = part.astype(bf)
            rb_ref[r * n4:(r + 1) * n4, :] = part
        for r in range(4):
            for m in range(4):
                part = rb_ref[pl.ds(r * n4 + m, n4 // 4, stride=4), :]
                r16_ref[r + 4 * m, :, cols] = part.astype(bf)

    for j in range(WA // MXU_N):
        for k, t in enumerate(proj(OFF_QA + j * MXU_N)):
            emit(rope(t) * LOG2E, slice((2 * j + k) * LANES, (2 * j + k + 1) * LANES), qa_ref, q4_ref, q16_ref)
        for k, t in enumerate(proj(OFF_KA + j * MXU_N)):
            emit(rope(t), slice((2 * j + k) * LANES, (2 * j + k + 1) * LANES), ka_ref, k4_ref, k16_ref)
        for k, t in enumerate(proj(OFF_VA + j * MXU_N)):
            emit(t, slice((2 * j + k) * LANES, (2 * j + k + 1) * LANES), va_ref, v4_ref, v16_ref)
    lo = lane < HEAD_DIM
    for j in range(WB // MXU_N):
        for k, t in enumerate(proj(OFF_QB + j * MXU_N)):
            pair = 2 * j + k
            t = rope(t) * LOG2E
            g = (2 * pair) // NSA_HPG
            tr = pltpu.roll(t, HEAD_DIM, 1)
            in_g = lo if g == 0 else jnp.logical_not(lo)
            even = jnp.where(in_g, t if g == 0 else tr, 0.0)
            odd = jnp.where(in_g, tr if g == 0 else t, 0.0)
            qbx_ref[:, (2 * pair) * LANES:(2 * pair + 1) * LANES] = even.astype(bf)
            qbx_ref[:, (2 * pair + 1) * LANES:(2 * pair + 2) * LANES] = odd.astype(bf)
    kc, vc = proj(OFF_KC)
    kc_ref[...] = rope(kc)
    vc_ref[...] = vc
    ks, kw = proj(OFF_KS)
    tpos = i * tm + lax.broadcasted_iota(jnp.int32, (tm, LANES), 0)
    blk = lax.broadcasted_iota(jnp.int32, (tm, LANES), 1)
    ksa_ref[:, 0:LANES] = jnp.where((tpos // SLC_BLOCK) == blk, 1.0, 0.0).astype(bf)
    ksa_ref[:, LANES:2 * LANES] = rope(ks).astype(bf)
    kw_ref[...] = rope(kw).astype(bf)
    tr_out = lax.dot_general(wt_ref[...], u_ref[...], _NT, preferred_element_type=jnp.float32)
    vst_ref[...] = tr_out[TR_VS:TR_VS + WG].astype(bf)
    vwt_ref[...] = tr_out[TR_VW:TR_VW + WG].astype(bf)
    gate_ref[...] = jax.nn.sigmoid(tr_out[TR_GL:TR_GL + LANES])


def _rope_tables(S):
    inv = 1.0 / (ROPE_THETA ** (jnp.arange(0, ROT_DIM, 2, dtype=jnp.float32) / ROT_DIM))
    ang = jnp.arange(S, dtype=jnp.float32)[:, None] * inv[None, :]
    cos, sin = jnp.cos(ang), jnp.sin(ang)
    half = ROT_DIM // 2
    d = np.arange(LANES) % HEAD_DIM
    idx = jnp.asarray(d % half)
    rot = jnp.asarray(d < ROT_DIM)
    lo = jnp.asarray(d < half)
    hi = jnp.asarray((d >= half) & (d < ROT_DIM))
    c_t = jnp.where(rot[None], cos[:, idx], 1.0)
    s_lo = jnp.where(lo[None], -sin[:, idx], 0.0)
    s_hi = jnp.where(hi[None], sin[:, idx], 0.0)
    return c_t, s_lo, s_hi


def _prep_w_in(w):
    q_scale = HEAD_DIM ** -0.5
    w_nat = jnp.concatenate([w[:, 0:WA] * q_scale, w[:, WA:3 * WA], w[:, 3 * WA:3 * WA + WB] * q_scale,
                             w[:, 3 * WA + WB:SRC_VS], w[:, SRC_KW:SRC_VW]], axis=1)
    w_tr = jnp.concatenate([w[:, SRC_VS:SRC_KW], w[:, SRC_VW:SRC_GL],
                            jnp.pad(w[:, SRC_GL:], ((0, 0), (0, LANES - N_GATES)))], axis=1)
    return w_nat.astype(jnp.bfloat16), w_tr.T.astype(jnp.bfloat16)


def _inproj(x, mod6, w_nat, w_tr, tables, tm):
    B, S, D = x.shape
    cos, slo, shi = tables
    bs = lambda w: pl.BlockSpec((None, tm, w), lambda b, i: (b, i, 0))
    bst = pl.BlockSpec((None, LANES, tm), lambda b, i: (b, 0, i))
    tab = pl.BlockSpec((tm, LANES), lambda b, i: (i, 0))
    sd = lambda w, dt: jax.ShapeDtypeStruct((B, S, w), dt)
    sdt = lambda dt: jax.ShapeDtypeStruct((B, LANES, S), dt)
    bf, f32 = jnp.bfloat16, jnp.float32
    res = lambda d: pl.BlockSpec((None, d, tm // d, WA), lambda b, i: (b, 0, i, 0))
    res1 = pl.BlockSpec((None, None, tm, WA), lambda b, i: (b, 0, i, 0))
    sdr = lambda d: jax.ShapeDtypeStruct((B, d, S // d, WA), bf)
    return pl.pallas_call(
        functools.partial(_inproj_kernel, tm=tm),
        grid=(B, S // tm),
        in_specs=[bs(D),
                  pl.BlockSpec((None, 6, D), lambda b, i: (b, 0, 0)),
                  pl.BlockSpec((D, NAT_WIDTH), lambda b, i: (0, 0)),
                  pl.BlockSpec((TR_WIDTH, D), lambda b, i: (0, 0)),
                  tab, tab, tab],
        out_specs=[res1, res1, res1, bs(2 * WB), bs(WG), bs(WG), bs(2 * LANES), bs(WG), bst, bst, bst]
                  + [res(4)] * 3 + [res(16)] * 3,
        out_shape=[sdr(1), sdr(1), sdr(1), sd(2 * WB, bf), sd(WG, f32), sd(WG, f32),
                   sd(2 * LANES, bf), sd(WG, bf), sdt(bf), sdt(bf), sdt(f32)]
                  + [sdr(4)] * 3 + [sdr(16)] * 3,
        scratch_shapes=[pltpu.VMEM((tm, D), jnp.bfloat16),
                        pltpu.VMEM((tm, LANES), jnp.float32),
                        pltpu.VMEM((tm, LANES), jnp.float32)],
        compiler_params=_cparams(("parallel", "parallel")),
        name="inproj",
    )(x, mod6, w_nat, w_tr, cos, slo, shi)


def _dil_kernel(q_ref, kc_ref, kp_ref, vc_ref, vp_ref, o_ref, lse_ref, *, nblk):
    i = pl.program_id(2)
    blk_rows = BAND_BLOCK
    qi = lax.broadcasted_iota(jnp.int32, (blk_rows, 2 * blk_rows), 0)
    kj = lax.broadcasted_iota(jnp.int32, (blk_rows, 2 * blk_rows), 1)
    in_prev = kj < blk_rows
    bias = jnp.where(in_prev, jnp.where(kj >= qi, 0.0, NEG), jnp.where(kj - blk_rows <= qi, 0.0, NEG))
    bias0 = jnp.where(in_prev, jnp.where(i == 0, NEG, bias), bias)
    lane = lax.broadcasted_iota(jnp.int32, (1, LANES), 1)
    lo = lane < HEAD_DIM
    n_pair = WA // LANES
    for blk in range(nblk):
        rows = slice(blk * blk_rows, (blk + 1) * blk_rows)
        both = slice((blk - 1) * blk_rows, (blk + 1) * blk_rows)
        b = bias0 if blk == 0 else bias
        vals, ss = [], []
        for hp in range(n_pair):
            cols = slice(hp * LANES, (hp + 1) * LANES)
            qt = q_ref[rows, cols]
            if blk == 0:
                k2 = jnp.concatenate([kp_ref[:, cols], kc_ref[rows, cols]], axis=0)
                vals.append(jnp.concatenate([vp_ref[:, cols], vc_ref[rows, cols]], axis=0))
            else:
                k2 = kc_ref[both, cols]
                vals.append(vc_ref[both, cols])
            for h in range(2):
                qh = jnp.where(lo if h == 0 else jnp.logical_not(lo), qt, jnp.zeros_like(qt))
                ss.append(lax.dot_general(qh, k2, _NT, preferred_element_type=jnp.float32) + b)
        ms = [jnp.max(s, -1, keepdims=True) for s in ss]
        ps = [jnp.exp2(s - m) for s, m in zip(ss, ms)]
        dens = [jnp.sum(p, -1, keepdims=True) for p in ps]
        lse = jnp.zeros((blk_rows, LANES), jnp.float32)
        for hp in range(n_pair):
            cols = slice(hp * LANES, (hp + 1) * LANES)
            outs = []
            for h in range(2):
                n = 2 * hp + h
                o = jnp.dot(ps[n].astype(jnp.bfloat16), vals[hp], preferred_element_type=jnp.float32)
                outs.append(o * (1.0 / dens[n]))
                lse = jnp.where(lane // LSE_LANES == n, ms[n] + jnp.log2(dens[n]), lse)
            o_ref[rows, cols] = jnp.where(lo, outs[0], outs[1])
        lse_ref[rows, :] = lse


def _dilated(q, k, v, name):
    B, dil, L, W = q.shape
    rows = min(L, 8 * BAND_BLOCK)
    nblk = rows // BAND_BLOCK
    cur = pl.BlockSpec((None, None, rows, W), lambda b, r, i: (b, r, i, 0))
    cur_lse = pl.BlockSpec((None, None, rows, LANES), lambda b, r, i: (b, r, i, 0))
    prev = pl.BlockSpec((None, None, BAND_BLOCK, W), lambda b, r, i: (b, r, jnp.maximum(i * nblk - 1, 0), 0))
    return pl.pallas_call(
        functools.partial(_dil_kernel, nblk=nblk),
        grid=(B, dil, L // rows),
        in_specs=[cur, cur, prev, cur, prev],
        out_specs=[cur, cur_lse],
        out_shape=[jax.ShapeDtypeStruct((B, dil, L, W), jnp.float32),
                   jax.ShapeDtypeStruct((B, dil, L, LANES), jnp.float32)],
        compiler_params=_cparams(("parallel", "parallel", "arbitrary")),
        name=name,
    )(q, k, k, v, v)


def _cmp_kernel(ck_ref, cv_ref, pe_ref, wk1_ref, wk2_ref, wv1_ref, wv2t_ref, ok_ref, ovt_ref):
    n = ck_ref.shape[0] // CMP_STRIDE
    bf = jnp.bfloat16

    def hidden(c_ref, w1_ref):
        a = jnp.zeros((n, CMP_HIDDEN), jnp.float32)
        b = jnp.zeros((n, CMP_HIDDEN), jnp.float32)
        for j in range(CMP_STRIDE):
            t = c_ref[pl.ds(j, n, stride=CMP_STRIDE), :]
            a = a + jnp.dot((t + pe_ref[j:j + 1, :]).astype(bf), w1_ref[j], preferred_element_type=jnp.float32)
            b = b + jnp.dot((t + pe_ref[CMP_STRIDE + j:CMP_STRIDE + j + 1, :]).astype(bf), w1_ref[CMP_STRIDE + j],
                            preferred_element_type=jnp.float32)
        return _gelu_tanh(a + pltpu.roll(b, n - 1, 0)).astype(bf)

    ok_ref[...] = jnp.dot(hidden(ck_ref, wk1_ref), wk2_ref[...],
                          preferred_element_type=jnp.float32).astype(ok_ref.dtype)
    ovt_ref[...] = lax.dot_general(wv2t_ref[...], hidden(cv_ref, wv1_ref), _NT,
                                   preferred_element_type=jnp.float32).astype(ovt_ref.dtype)


def _compress(kc, vc, pe, w_ck1, w_ck2, w_cv1, w_cv2):
    B, S, _ = kc.shape
    n = S // CMP_STRIDE
    bf = jnp.bfloat16

    def slabs(w1):
        w = w1.reshape(CMP_LEN, HEAD_DIM, CMP_HIDDEN).astype(bf)
        z = jnp.zeros_like(w)
        return jnp.stack([jnp.concatenate([w, z], axis=1), jnp.concatenate([z, w], axis=1)], axis=0)

    seq = pl.BlockSpec((None, S, WG), lambda b, g: (b, 0, 0))
    full = lambda a: pl.BlockSpec(a.shape, lambda b, g: (0,) * a.ndim)
    slab = pl.BlockSpec((None, CMP_LEN, WG, CMP_HIDDEN), lambda b, g: (g, 0, 0, 0))
    pe2 = jnp.concatenate([pe, pe], axis=-1)
    args = (kc, vc, pe2, slabs(w_ck1), jnp.concatenate([w_ck2, w_ck2], axis=-1).astype(bf),
            slabs(w_cv1), w_cv2.T.astype(bf))
    return pl.pallas_call(
        _cmp_kernel,
        grid=(B, NSA_GROUPS),
        in_specs=[seq, seq, full(pe2), slab, full(args[4]), slab, full(args[6])],
        out_specs=[pl.BlockSpec((None, None, n, LANES), lambda b, g: (b, g, 0, 0)),
                   pl.BlockSpec((None, None, HEAD_DIM, n), lambda b, g: (b, g, 0, 0))],
        out_shape=[jax.ShapeDtypeStruct((B, NSA_GROUPS, n, LANES), bf),
                   jax.ShapeDtypeStruct((B, NSA_GROUPS, HEAD_DIM, n), bf)],
        compiler_params=_cparams(("parallel", "parallel")),
        name="compress",
    )(*args)


def _nsa_cmp_kernel(q_ref, kcc_ref, vcct_ref, ovt_ref, o_ref, sel_ref, *, tq, nsel, topk, span):
    qi = pl.program_id(2)
    t0 = qi * tq
    nc_all = kcc_ref.shape[0]
    n_var = max(1, (nc_all * CMP_STRIDE) // span)

    def body(nc, nb):
        tpos = t0 + lax.broadcasted_iota(jnp.int32, (nc, tq), 1)
        cend = lax.broadcasted_iota(jnp.int32, (nc, tq), 0) * CMP_STRIDE + (CMP_LEN - 1)
        cbias = jnp.where(cend <= tpos, 0.0, NEG)
        any_vis = (t0 + lax.broadcasted_iota(jnp.int32, (1, tq), 1)) >= CMP_LEN - 1
        kcc, vcct, ovt = kcc_ref[0:nc, :], vcct_ref[:, 0:nc], ovt_ref[0:nb, 0:nc]
        ss = [lax.dot_general(kcc, q_ref[:, h * LANES:(h + 1) * LANES], _NT,
                              preferred_element_type=jnp.float32) + cbias for h in range(NSA_HPG)]
        ms = [jnp.max(s, 0, keepdims=True) for s in ss]
        ps = [jnp.exp2(s - m).astype(jnp.bfloat16) for s, m in zip(ss, ms)]
        vaug = jnp.concatenate([vcct, jnp.ones((DEN_ROWS, nc), jnp.bfloat16)], axis=0)
        imp = jnp.zeros((nb, tq), jnp.float32)
        for h in range(NSA_HPG):
            oa = jnp.dot(vaug, ps[h], preferred_element_type=jnp.float32)
            inv = jnp.where(any_vis, 1.0 / oa[HEAD_DIM:HEAD_DIM + 1], 0.0)
            o_ref[h * HEAD_DIM:(h + 1) * HEAD_DIM, :] = oa[:HEAD_DIM] * inv
            imp = imp + jnp.dot(ovt, ps[h], preferred_element_type=jnp.float32) * inv
        blk = lax.broadcasted_iota(jnp.int32, (nb, tq), 0)
        tq_pos = t0 + lax.broadcasted_iota(jnp.int32, (nb, tq), 1)
        cur = tq_pos // SLC_BLOCK
        forced = (blk == cur) | (blk == cur - 1) | (blk == 0)
        valid = blk <= cur
        cand = valid & jnp.logical_not(forced) & (blk < nsel)
        score = jnp.where(cand, imp, -jnp.inf)
        blk_f = blk.astype(jnp.float32)
        for _ in range(topk - 3):
            mx = jnp.max(score, 0, keepdims=True)
            first = jnp.min(jnp.where(score == mx, blk_f, float(2 * LANES)), 0, keepdims=True)
            score = jnp.where(blk_f == first, -jnp.inf, score)
        sel = jnp.where(forced, 1.0, jnp.where(cand & (score == -jnp.inf), 1.0, 0.0))
        sel = jnp.where(valid, sel, 0.0)
        if nb < LANES:
            sel = jnp.concatenate([sel, jnp.zeros((LANES - nb, tq), jnp.float32)], axis=0)
        sel_ref[...] = ((sel.T - 1.0) * (-NEG)).astype(jnp.bfloat16)

    variant = jnp.minimum((t0 + tq - 1) // span, n_var - 1)
    for k in range(n_var):
        last_var = k == n_var - 1
        nc = nc_all if last_var else (k + 1) * span // CMP_STRIDE
        nb = LANES if last_var else (k + 1) * span // SLC_BLOCK
        pl.when(variant == k)(functools.partial(body, nc, nb))


def _overlap_t(S):
    nc = S // CMP_STRIDE
    ns = S // SLC_BLOCK
    cs = np.arange(nc) * CMP_STRIDE
    ss = np.arange(LANES) * SLC_BLOCK
    ov = np.clip(np.minimum(cs[None, :] + CMP_LEN, ss[:, None] + SLC_BLOCK)
                 - np.maximum(cs[None, :], ss[:, None]), 0, None).astype(np.float32) / CMP_LEN
    ov[ns:, :] = 0.0
    ov[:, nc - CMP_LEN // CMP_STRIDE + 1:] = 0.0
    return jnp.asarray(ov, jnp.bfloat16)


def _nsa_cmp(qbx, kcc, vcct, tq):
    B, S, _ = qbx.shape
    nc = kcc.shape[2]
    ns = S // SLC_BLOCK
    gw = NSA_HPG * LANES
    ovt = _overlap_t(S)
    return pl.pallas_call(
        functools.partial(_nsa_cmp_kernel, tq=tq, nsel=ns, topk=min(SLC_TOPK, ns), span=CMP_SPAN),
        grid=(B, NSA_GROUPS, S // tq),
        in_specs=[pl.BlockSpec((None, tq, gw), lambda b, g, i: (b, i, g)),
                  pl.BlockSpec((None, None, nc, LANES), lambda b, g, i: (b, g, 0, 0)),
                  pl.BlockSpec((None, None, HEAD_DIM, nc), lambda b, g, i: (b, g, 0, 0)),
                  pl.BlockSpec((LANES, nc), lambda b, g, i: (0, 0))],
        out_specs=[pl.BlockSpec((None, NSA_HPG * HEAD_DIM, tq), lambda b, g, i: (b, g, i)),
                   pl.BlockSpec((None, None, tq, LANES), lambda b, g, i: (b, g, i, 0))],
        out_shape=[jax.ShapeDtypeStruct((B, WB, S), jnp.float32),
                   jax.ShapeDtypeStruct((B, NSA_GROUPS, S, LANES), jnp.bfloat16)],
        compiler_params=_cparams(("parallel", "parallel", "parallel")),
        name="nsa_cmp",
    )(qbx, kcc, vcct, ovt)


def _nsa_main_kernel(q_ref, sel_ref, ksa_ref, vst_ref, kw_ref, vwt_ref, ocmp_ref, gate_ref, o_ref,
                     qa_scr, m_scr, acc_scr, win_scr, sa_scr, sb_scr, ma_scr, mb_scr, *, tq, tk):
    g = pl.program_id(1)
    qi = pl.program_id(2)
    t0 = qi * tq
    cols = NSA_HPG * tq
    bf = jnp.bfloat16
    selb = sel_ref[...]
    for h in range(NSA_HPG):
        qa_scr[h * tq:(h + 1) * tq, 0:LANES] = selb
        qa_scr[h * tq:(h + 1) * tq, LANES:2 * LANES] = q_ref[:, h * LANES:(h + 1) * LANES]
    m_scr[...] = jnp.full((1, cols), -1e37, jnp.float32)
    acc_scr[...] = jnp.zeros(acc_scr.shape, jnp.float32)

    def with_ones(vt):
        return jnp.concatenate([vt, jnp.ones((DEN_ROWS, vt.shape[1]), bf)], axis=0)

    def scores(tile, dst, dst_max):
        start = pl.multiple_of(tile * tk, tk)
        s = lax.dot_general(ksa_ref[pl.ds(start, tk), :], qa_scr[...], _NT,
                            preferred_element_type=jnp.float32)
        dst[...] = s
        dst_max[...] = jnp.max(s, 0, keepdims=True)

    def consume(tile, src, src_max):
        start = pl.multiple_of(tile * tk, tk)
        vt = with_ones(vst_ref[:, pl.ds(start, tk)])
        m_prev = m_scr[...]
        m_new = jnp.maximum(m_prev, src_max[...])
        alpha = jnp.exp2(m_prev - m_new)
        p = jnp.exp2(src[...] - m_new)
        acc_scr[...] = alpha * acc_scr[...] + jnp.dot(vt, p.astype(bf), preferred_element_type=jnp.float32)
        m_scr[...] = m_new

    last = t0 // tk

    wk = WIN + tq
    wstart = pl.multiple_of(jnp.maximum(t0 - WIN, 0), tq)
    sw = lax.dot_general(kw_ref[pl.ds(wstart, wk), :], qa_scr[:, LANES:2 * LANES], _NT,
                         preferred_element_type=jnp.float32)
    dist = ((lax.broadcasted_iota(jnp.int32, (wk, tq), 1) + t0)
            - (lax.broadcasted_iota(jnp.int32, (wk, tq), 0) + wstart))
    wb = jnp.where((dist >= 0) & (dist < WIN), 0.0, NEG)
    sw = sw + jnp.concatenate([wb] * NSA_HPG, axis=1)
    mw = jnp.max(sw, 0, keepdims=True)
    scores(0, sa_scr, ma_scr)
    pw = jnp.exp2(sw - mw).astype(bf)
    ow = jnp.dot(with_ones(vwt_ref[:, pl.ds(wstart, wk)]), pw, preferred_element_type=jnp.float32)
    win_scr[...] = ow[:HEAD_DIM] * (1.0 / ow[HEAD_DIM:HEAD_DIM + 1])

    def pair(first):
        scores(first + 1, sb_scr, mb_scr)
        consume(first, sa_scr, ma_scr)
        scores(first + 2, sa_scr, ma_scr)
        consume(first + 1, sb_scr, mb_scr)

    def quad_body(jj, carry):
        pair(4 * jj)
        pair(4 * jj + 2)
        return carry

    n_quad = last // 4
    lax.fori_loop(0, n_quad, quad_body, 0)

    @pl.when(last - 4 * n_quad >= 2)
    def _():
        pair(4 * n_quad)

    kk = lax.broadcasted_iota(jnp.int32, (tq, tq), 0)
    qq = lax.broadcasted_iota(jnp.int32, (tq, tq), 1)
    cb = jnp.where(kk <= qq, 0.0, NEG)
    cb = jnp.concatenate([cb] * NSA_HPG, axis=1)
    diag = pl.multiple_of(t0 - last * tk, tq)

    def finish(src):
        src[pl.ds(diag, tq), :] = src[pl.ds(diag, tq), :] + cb
        start = pl.multiple_of(last * tk, tk)
        m_prev = m_scr[...]
        m_new = jnp.maximum(m_prev, jnp.max(src[...], 0, keepdims=True))
        alpha = jnp.exp2(m_prev - m_new)
        p = jnp.exp2(src[...] - m_new).astype(bf)
        pv = jnp.dot(with_ones(vst_ref[:, pl.ds(start, tk)]), p, preferred_element_type=jnp.float32)
        acc_scr[...] = alpha * acc_scr[...] + pv

    @pl.when(last % 2 == 1)
    def _():
        scores(last, sb_scr, mb_scr)
        consume(last - 1, sa_scr, ma_scr)
        finish(sb_scr)

    @pl.when(last % 2 == 0)
    def _():
        finish(sa_scr)

    o_slc = acc_scr[0:HEAD_DIM, :] * (1.0 / acc_scr[HEAD_DIM:HEAD_DIM + 1, :])
    o_win = win_scr[...]

    base = g * (NSA_HPG * 3)
    heads = []
    for h in range(NSA_HPG):
        c = slice(h * tq, (h + 1) * tq)
        g_cmp = gate_ref[pl.ds(base + 3 * h, 1), :]
        g_slc = gate_ref[pl.ds(base + 3 * h + 1, 1), :]
        g_win = gate_ref[pl.ds(base + 3 * h + 2, 1), :]
        heads.append(g_cmp * ocmp_ref[h * HEAD_DIM:(h + 1) * HEAD_DIM, :] + g_slc * o_slc[:, c] + g_win * o_win[:, c])
    o_ref[...] = jnp.concatenate(heads, axis=0).T.astype(o_ref.dtype)


def _nsa_main(qbx, selb, ksa, vst, kw, vwt, o_cmp_t, gates_t, tq, tk):
    B, S, _ = qbx.shape
    gw = NSA_HPG * LANES
    ow = NSA_HPG * HEAD_DIM
    cols = NSA_HPG * tq
    seq = lambda w: pl.BlockSpec((None, S, w), lambda b, g, i: (b, 0, 0))
    seqt = pl.BlockSpec((None, HEAD_DIM, S), lambda b, g, i: (b, g, 0))
    return pl.pallas_call(
        functools.partial(_nsa_main_kernel, tq=tq, tk=tk),
        grid=(B, NSA_GROUPS, S // tq),
        in_specs=[pl.BlockSpec((None, tq, gw), lambda b, g, i: (b, i, g)),
                  pl.BlockSpec((None, None, tq, LANES), lambda b, g, i: (b, g, i, 0)),
                  seq(2 * LANES), seqt, seq(WG), seqt,
                  pl.BlockSpec((None, ow, tq), lambda b, g, i: (b, g, i)),
                  pl.BlockSpec((None, LANES, tq), lambda b, g, i: (b, 0, i))],
        out_specs=pl.BlockSpec((None, tq, ow), lambda b, g, i: (b, i, g)),
        out_shape=jax.ShapeDtypeStruct((B, S, WB), jnp.bfloat16),
        scratch_shapes=[pltpu.VMEM((cols, 2 * LANES), jnp.bfloat16),
                        pltpu.VMEM((1, cols), jnp.float32),
                        pltpu.VMEM((HEAD_DIM + DEN_ROWS, cols), jnp.float32),
                        pltpu.VMEM((HEAD_DIM, cols), jnp.float32),
                        pltpu.VMEM((tk, cols), jnp.float32),
                        pltpu.VMEM((tk, cols), jnp.float32),
                        pltpu.VMEM((1, cols), jnp.float32),
                        pltpu.VMEM((1, cols), jnp.float32)],
        compiler_params=_cparams(("parallel", "parallel", "arbitrary")),
        name="nsa_main",
    )(qbx, selb, ksa, vst, kw, vwt, o_cmp_t, gates_t)


def _outproj_kernel(x_ref, o1_ref, l1_ref, o4_ref, l4_ref, o16_ref, l16_ref, ob_ref, mod_ref, wo_ref, g_ref, b_ref,
                    ex_ref, y_ref, oa_scr, s0_scr, s1_scr, s2_scr, s3_scr, *, tm):
    def token_order(src, cols, scr):
        dil = src.shape[0]
        for r in range(dil):
            scr[pl.ds(r, tm // dil, stride=dil), :] = src[r, :, cols]
        return scr[...]

    l1 = l1_ref[...]
    l2 = token_order(l4_ref, slice(None), s0_scr)
    l3 = token_order(l16_ref, slice(None), s1_scr)
    m = jnp.maximum(jnp.maximum(l1, l2), l3)
    es = [jnp.exp2(l1 - m), jnp.exp2(l2 - m), jnp.exp2(l3 - m)]
    inv = 1.0 / (es[0] + es[1] + es[2])

    def widen(w):
        hi = w.astype(jnp.bfloat16)
        lo = (w - hi.astype(jnp.float32)).astype(jnp.bfloat16)
        return jnp.dot(jnp.concatenate([hi, lo], axis=1), ex_ref[...], preferred_element_type=jnp.float32)

    w1, w2, w3 = [widen(e * inv) for e in es]
    for c in range(WA // LANES):
        cols = slice(c * LANES, (c + 1) * LANES)
        o2 = token_order(o4_ref, cols, s2_scr)
        o3 = token_order(o16_ref, cols, s3_scr)
        o_a = w1[:, cols] * o1_ref[:, cols] + w2[:, cols] * o2 + w3[:, cols] * o3
        oa_scr[:, cols] = o_a.astype(jnp.bfloat16)
    y = (jnp.dot(oa_scr[...], wo_ref[:WA, :], preferred_element_type=jnp.float32)
         + jnp.dot(ob_ref[...], wo_ref[WA:, :], preferred_element_type=jnp.float32))
    z = ALPHA * x_ref[...] + mod_ref[2:3, :] * y
    y_ref[...] = _ln(z) * g_ref[...] + b_ref[...]


def _outproj(x, pats, o_b, mod6, wo_bf16, ln_g, ln_b, tm):
    B, S, D = x.shape
    bs = lambda w: pl.BlockSpec((None, tm, w), lambda b, i: (b, i, 0))
    res = lambda d, w: pl.BlockSpec((None, d, tm // d, w), lambda b, i: (b, 0, i, 0))
    res1 = lambda w: pl.BlockSpec((None, None, tm, w), lambda b, i: (b, 0, i, 0))
    vec = pl.BlockSpec((1, D), lambda b, i: (0, 0))
    (o1, l1), (o4, l4), (o16, l16) = pats
    scr = pltpu.VMEM((tm, LANES), jnp.float32)
    expand = np.zeros((LANES, WA), np.float32)
    expand[(np.arange(WA) // HEAD_DIM) * LSE_LANES, np.arange(WA)] = 1.0
    expand = jnp.asarray(np.concatenate([expand, expand], axis=0), jnp.bfloat16)
    return pl.pallas_call(
        functools.partial(_outproj_kernel, tm=tm),
        grid=(B, S // tm),
        in_specs=[bs(D), res1(WA), res1(LANES), res(4, WA), res(4, LANES), res(16, WA), res(16, LANES), bs(WB),
                  pl.BlockSpec((None, 6, D), lambda b, i: (b, 0, 0)),
                  pl.BlockSpec((WA + WB, D), lambda b, i: (0, 0)), vec, vec,
                  pl.BlockSpec((2 * LANES, WA), lambda b, i: (0, 0))],
        out_specs=bs(D),
        out_shape=jax.ShapeDtypeStruct((B, S, D), jnp.float32),
        scratch_shapes=[pltpu.VMEM((tm, WA), jnp.bfloat16), scr, scr, scr, scr],
        compiler_params=_cparams(("parallel", "parallel")),
        name="outproj",
    )(x, o1, l1, o4, l4, o16, l16, o_b, mod6, wo_bf16,
      ln_g.reshape(1, D), ln_b.reshape(1, D), expand)


def _ffn_kernel(x_ref, mod_ref, wup_ref, cw_ref, cb_ref, wdn_ref, g_ref, b_ref, y_ref,
                u_scr, buf_scr, carry_scr, acc_scr, *, tm, fc):
    i = pl.program_id(1)
    F = wdn_ref.shape[0]
    pad = 8

    @pl.when(i == 0)
    def _():
        carry_scr[...] = jnp.zeros_like(carry_scr)

    th = tm // 2
    n_chunk = F // fc
    halves = (slice(0, th), slice(th, tm))

    def prologue(rows):
        u_scr[rows, :] = (_ln(x_ref[rows, :]) * (1.0 + mod_ref[4:5, :]) + mod_ref[3:4, :]).astype(jnp.bfloat16)

    def up(rows, c):
        u = u_scr[rows, :]
        return (jnp.dot(u, wup_ref[:, c * fc:(c + 1) * fc], preferred_element_type=jnp.float32),
                jnp.dot(u, wup_ref[:, F + c * fc:F + (c + 1) * fc], preferred_element_type=jnp.float32))

    def chunk(rows, c, a_gate, a_val):
        cols = slice(c * fc, (c + 1) * fc)
        buf_scr[0:pad, :] = carry_scr[:, cols]
        buf_scr[pad:pad + th, :] = a_gate
        carry_scr[:, cols] = a_gate[th - pad:th, :]
        conv = (cw_ref[0:1, cols] * buf_scr[pad - 2:pad - 2 + th, :]
                + cw_ref[1:2, cols] * buf_scr[pad - 1:pad - 1 + th, :]
                + cw_ref[2:3, cols] * a_gate + cb_ref[:, cols])
        h = (_gelu_tanh(conv) * a_val).astype(jnp.bfloat16)
        d = jnp.dot(h, wdn_ref[cols, :], preferred_element_type=jnp.float32)
        if c == 0:
            acc_scr[rows, :] = d
        else:
            acc_scr[rows, :] += d

    def epilogue(rows):
        z = ALPHA * x_ref[rows, :] + mod_ref[5:6, :] * acc_scr[rows, :]
        y_ref[rows, :] = _ln(z) * g_ref[...] + b_ref[...]

    items = [(rows, c) for rows in halves for c in range(n_chunk)]
    prologue(halves[0])
    nxt = up(*items[0])
    for k, (rows, c) in enumerate(items):
        cur = nxt
        if k == 0:
            prologue(halves[1])
        if k + 1 < len(items):
            nxt = up(*items[k + 1])
        if k == n_chunk + 1:
            epilogue(halves[0])
        chunk(rows, c, *cur)
    epilogue(halves[1])


def _ffn(x, mod6, wup_bf16, conv_w, conv_b, wdn_bf16, ln_g, ln_b, tm, fc):
    B, S, D = x.shape
    F = wdn_bf16.shape[0]
    bs = pl.BlockSpec((None, tm, D), lambda b, i: (b, i, 0))
    full = lambda a: pl.BlockSpec(a.shape, lambda b, i: (0,) * a.ndim, pipeline_mode=pl.Buffered(1))
    args = (wup_bf16, conv_w, conv_b.reshape(1, F), wdn_bf16, ln_g.reshape(1, D), ln_b.reshape(1, D))
    return pl.pallas_call(
        functools.partial(_ffn_kernel, tm=tm, fc=fc),
        grid=(B, S // tm),
        in_specs=[bs, pl.BlockSpec((None, 6, D), lambda b, i: (b, 0, 0))] + [full(a) for a in args],
        out_specs=bs,
        out_shape=jax.ShapeDtypeStruct((B, S, D), jnp.float32),
        scratch_shapes=[pltpu.VMEM((tm, D), jnp.bfloat16),
                        pltpu.VMEM((tm // 2 + 8, fc), jnp.float32),
                        pltpu.VMEM((8, F), jnp.float32),
                        pltpu.VMEM((tm, D), jnp.float32)],
        compiler_params=_cparams(("arbitrary", "arbitrary")),
        name="ffn",
    )(x, mod6, *args)


def kernel(x, c, w_ada, b_ada, w_in, pe_cmp, w_ck1, w_ck2, w_cv1, w_cv2, w_o, ln1_g, ln1_b, w_up, conv_w, conv_b,
           w_down, ln2_g, ln2_b):
    B, S, D = x.shape
    tables = _rope_tables(S)
    tm = min(512, S)
    tq = 128
    tk = min(512, S)
    for l in range(DEPTH):
        mod6 = _ada(c, w_ada[l], b_ada[l]).reshape(B, 6, D)
        w_nat, w_tr = _prep_w_in(w_in[l])
        (qa, ka, va, qbx, kc, vc, ksa, kw, vst, vwt, gates_t,
         q4, k4, v4, q16, k16, v16) = _inproj(x, mod6, w_nat, w_tr, tables, tm)
        pats = [_dilated(qa, ka, va, "dilated1"),
                _dilated(q4, k4, v4, "dilated4"), _dilated(q16, k16, v16, "dilated16")]
        kcc, vcct = _compress(kc, vc, pe_cmp[l], w_ck1[l], w_ck2[l], w_cv1[l], w_cv2[l])
        o_cmp_t, selb = _nsa_cmp(qbx, kcc, vcct, 4 * tq)
        o_b = _nsa_main(qbx, selb, ksa, vst, kw, vwt, o_cmp_t, gates_t, 4 * tq, tk)
        x = _outproj(x, pats, o_b, mod6, w_o[l].astype(jnp.bfloat16), ln1_g[l], ln1_b[l], tm)
        x = _ffn(x, mod6, w_up[l].astype(jnp.bfloat16), conv_w[l], conv_b[l], w_down[l].astype(jnp.bfloat16),
                 ln2_g[l], ln2_b[l], tm, 256)
    return x
```
